```python
import math
import jax, jax.numpy as jnp
from jax import lax
import numpy as np

D_MODEL = 1024
BATCH = 4
SEQ = 8192
DEPTH = 1

SB_WIDTH = D_MODEL // 2
SB_HEADS = 8
SB_HEAD_DIM = SB_WIDTH // SB_HEADS
ML_WIDTH = D_MODEL - SB_WIDTH
ML_HEADS = 4
ML_HEAD_DIM = ML_WIDTH // ML_HEADS
MIX_WIDTH = SB_WIDTH + ML_WIDTH
Q_BLOCK = 128
ML_CHUNK = 64
CONV_K = 4
PROJ_COLS = 3 * SB_WIDTH + 4 * ML_WIDTH + 2 * ML_HEADS
N_EXPERTS = 256
TOP_K = 8
N_GROUPS = 8
TOPK_GROUPS = 4
EXPERT_FF = D_MODEL // 4
SHARED_FF = D_MODEL // 4
ROUTED_SCALE = 2.5
DISPATCH_BLOCK = 128
DN_ALPHA = (2 * DEPTH) ** 0.25
DN_BETA = (8 * DEPTH) ** -0.25
LN_EPS = 1e-5

kernel_name = 'hymba_sb_mlstm_moe_deepnorm_adaln'


def layer_norm(x, g=None, b=None):
    xf = x.astype(jnp.float32)
    mu = xf.mean(-1, keepdims=True)
    var = jnp.square(xf - mu).mean(-1, keepdims=True)
    y = (xf - mu) * lax.rsqrt(var + LN_EPS)
    if g is not None:
        y = y * g.astype(jnp.float32) + b.astype(jnp.float32)
    return y.astype(x.dtype)


def causal_depthwise_conv(x, w, b):
    C = x.shape[-1]
    y = lax.conv_general_dilated(x, w[:, None, :].astype(x.dtype), window_strides=(1,),
                                 padding=[(CONV_K - 1, 0)],
                                 dimension_numbers=('NWC', 'WIO', 'NWC'),
                                 feature_group_count=C)
    return y + b


def stick_breaking_attention(q, k, v):
    B, H, S, dh = q.shape
    nq = S // Q_BLOCK
    scale = dh ** -0.5
    kf = k.astype(jnp.float32)
    vf = v.astype(jnp.float32)
    qb = jnp.moveaxis(q.astype(jnp.float32).reshape(B, H, nq, Q_BLOCK, dh), 2, 0)
    key_pos = jnp.arange(S)

    def block(args):
        q_blk, q0 = args
        z = jnp.einsum('bhqd,bhkd->bhqk', q_blk, kf) * scale
        q_pos = q0 + jnp.arange(Q_BLOCK)
        mask = key_pos[None, :] < q_pos[:, None]
        log_beta = jax.nn.log_sigmoid(z)
        log_1m = jnp.where(mask, log_beta - z, 0.0)
        after = lax.cumsum(log_1m, axis=3, reverse=True) - log_1m
        a = jnp.where(mask, jnp.exp(log_beta + after), 0.0)
        return jnp.einsum('bhqk,bhkd->bhqd', a, vf)

    out = lax.map(block, (qb, jnp.arange(nq) * Q_BLOCK))
    return jnp.moveaxis(out, 0, 2).reshape(B, H, S, dh)


def mlstm_chunkwise(q, k, v, i_pre, logf):
    B, H, S, dk = q.shape
    dv = v.shape[-1]
    L = ML_CHUNK
    nc = S // L
    k = k * dk ** -0.5

    def chunks(t):
        return jnp.moveaxis(t.reshape((B, H, nc, L) + t.shape[3:]), 2, 0)

    bcum = lax.cumsum(logf.reshape(B, H, nc, L), axis=3)
    causal = jnp.tril(jnp.ones((L, L), dtype=bool))

    def step(carry, xs):
        C, n, m = carry
        qc, kc, vc, ic, bc = xs
        log_d = jnp.where(causal, bc[..., :, None] - bc[..., None, :] + ic[..., None, :], -jnp.inf)
        inter = bc + m[..., None]
        m_t = jnp.maximum(inter, log_d.max(-1))
        w = jnp.einsum('bhtd,bhsd->bhts', qc, kc) * jnp.exp(log_d - m_t[..., None])
        s_inter = jnp.exp(inter - m_t)
        num = (s_inter[..., None] * jnp.einsum('bhtk,bhvk->bhtv', qc, C)
               + jnp.einsum('bhts,bhsv->bhtv', w, vc))
        den = s_inter * jnp.einsum('bhtk,bhk->bht', qc, n) + w.sum(-1)
        h = num / jnp.maximum(jnp.abs(den), jnp.exp(-m_t))[..., None]
        b_last = bc[..., -1]
        log_w = b_last[..., None] - bc + ic
        m_new = jnp.maximum(b_last + m, log_w.max(-1))
        wk = jnp.exp(log_w - m_new[..., None])
        decay = jnp.exp(b_last + m - m_new)
        C = decay[..., None, None] * C + jnp.einsum('bhs,bhsv,bhsk->bhvk', wk, vc, kc)
        n = decay[..., None] * n + jnp.einsum('bhs,bhsk->bhk', wk, kc)
        return (C, n, m_new), h

    init = (jnp.zeros((B, H, dv, dk), jnp.float32), jnp.zeros((B, H, dk), jnp.float32),
            jnp.zeros((B, H), jnp.float32))
    _, h = lax.scan(step, init, (chunks(q), chunks(k), chunks(v), chunks(i_pre), jnp.moveaxis(bcum, 2, 0)))
    return jnp.moveaxis(h, 0, 2).reshape(B, H, S, dv)


def hybrid_mixer(u, w_in, ml_conv_w, ml_conv_b, ml_gate_b, ml_norm_g, w_out):
    B, S, _ = u.shape
    proj = u @ w_in
    cuts = [SB_WIDTH, 2 * SB_WIDTH, 3 * SB_WIDTH, 3 * SB_WIDTH + ML_WIDTH,
            3 * SB_WIDTH + 2 * ML_WIDTH, 3 * SB_WIDTH + 3 * ML_WIDTH, 3 * SB_WIDTH + 4 * ML_WIDTH]
    sb_q, sb_k, sb_v, ml_q, ml_k, ml_v, ml_o, ml_if = jnp.split(proj, cuts, axis=-1)

    def heads(t, h):
        return t.reshape(B, S, h, -1).transpose(0, 2, 1, 3)

    sb = stick_breaking_attention(heads(sb_q, SB_HEADS), heads(sb_k, SB_HEADS), heads(sb_v, SB_HEADS))
    sb = sb.transpose(0, 2, 1, 3).reshape(B, S, SB_WIDTH).astype(u.dtype)

    qk = jax.nn.silu(causal_depthwise_conv(jnp.concatenate([ml_q, ml_k], -1), ml_conv_w, ml_conv_b))
    ml_q, ml_k = jnp.split(qk, 2, axis=-1)
    gates = ml_if.astype(jnp.float32) + ml_gate_b.astype(jnp.float32)
    i_pre = gates[..., :ML_HEADS].transpose(0, 2, 1)
    logf = jax.nn.log_sigmoid(gates[..., ML_HEADS:]).transpose(0, 2, 1)
    h = mlstm_chunkwise(heads(ml_q, ML_HEADS).astype(jnp.float32), heads(ml_k, ML_HEADS).astype(jnp.float32),
                        heads(ml_v, ML_HEADS).astype(jnp.float32), i_pre, logf)
    h = layer_norm(h).transpose(0, 2, 1, 3).reshape(B, S, ML_WIDTH) * ml_norm_g.astype(jnp.float32)
    ml = jax.nn.sigmoid(ml_o) * h.astype(u.dtype)

    return jnp.concatenate([sb, ml], axis=-1) @ w_out


def moe_ffn(u, w_router, router_bias, moe_w1, moe_w3, moe_w2, sh_w1, sh_w3, sh_w2):
    B, S, D = u.shape
    T = B * S
    xt = u.reshape(T, D)
    scores = jax.nn.sigmoid((xt @ w_router).astype(jnp.float32))
    sel = scores + router_bias.astype(jnp.float32)
    grp_score = lax.top_k(sel.reshape(T, N_GROUPS, -1), 2)[0].sum(-1)
    _, top_g = lax.top_k(grp_score, TOPK_GROUPS)
    gmask = jnp.any(top_g[..., None] == jnp.arange(N_GROUPS), axis=1)
    sel = jnp.where(jnp.repeat(gmask, N_EXPERTS // N_GROUPS, axis=1), sel, -jnp.inf)
    _, top_e = lax.top_k(sel, TOP_K)
    g = jnp.take_along_axis(scores, top_e, axis=1)
    g = g / g.sum(-1, keepdims=True) * ROUTED_SCALE

    TK = T * TOP_K
    e_flat = top_e.reshape(-1)
    tok_flat = jnp.repeat(jnp.arange(T, dtype=jnp.int32), TOP_K)
    order = jnp.argsort(e_flat)
    se, stok, sw = e_flat[order], tok_flat[order], g.reshape(-1)[order]
    counts = jax.ops.segment_sum(jnp.ones((TK,), jnp.int32), e_flat, num_segments=N_EXPERTS)
    offsets = jnp.cumsum(counts) - counts
    pcounts = (counts + DISPATCH_BLOCK - 1) // DISPATCH_BLOCK * DISPATCH_BLOCK
    pend = jnp.cumsum(pcounts)
    poffsets = pend - pcounts
    dest = poffsets[se] + (jnp.arange(TK, dtype=jnp.int32) - offsets[se])
    R = -(-TK // DISPATCH_BLOCK) * DISPATCH_BLOCK + N_EXPERTS * DISPATCH_BLOCK
    nb = R // DISPATCH_BLOCK
    buf_tok = jnp.full((R,), T, jnp.int32).at[dest].set(stok)
    buf_w = jnp.zeros((R,), xt.dtype).at[dest].set(sw.astype(xt.dtype))
    block_e = jnp.minimum(jnp.searchsorted(pend, jnp.arange(nb) * DISPATCH_BLOCK, side='right'),
                          N_EXPERTS - 1)
    x_pad = jnp.concatenate([xt, jnp.zeros((1, D), xt.dtype)], axis=0)

    def expert_block(y, blk):
        tok_b, w_b, e_b = blk
        xb = x_pad[tok_b]
        h = jax.nn.silu(xb @ moe_w1[e_b]) * (xb @ moe_w3[e_b])
        return y.at[tok_b].add((h @ moe_w2[e_b]) * w_b[:, None]), None

    y, _ = lax.scan(expert_block, jnp.zeros((T + 1, D), xt.dtype),
                    (buf_tok.reshape(nb, DISPATCH_BLOCK), buf_w.reshape(nb, DISPATCH_BLOCK), block_e))
    shared = (jax.nn.silu(xt @ sh_w1) * (xt @ sh_w3)) @ sh_w2
    return (y[:T] + shared).reshape(B, S, D)


def setup_inputs(seed: int = 0) -> dict:
    key = jax.random.key(seed)
    ks = jax.random.split(key, 24)
    f32 = jnp.float32
    D, L = D_MODEL, DEPTH

    def nrm(k, shape, scale):
        return jax.random.normal(k, shape, f32) * scale

    x = nrm(ks[0], (BATCH, SEQ, D), 1.0)
    c = nrm(ks[1], (BATCH, D), 1.0)
    w_ada = nrm(ks[2], (L, D, 6 * D), 0.5 * D ** -0.5)
    b_ada = nrm(ks[3], (L, 6 * D), 0.02)
    s_in = D ** -0.5
    w_in = jnp.concatenate([
        nrm(ks[4], (L, D, 2 * SB_WIDTH), s_in),
        nrm(ks[5], (L, D, SB_WIDTH), s_in * DN_BETA),
        nrm(ks[6], (L, D, 2 * ML_WIDTH), s_in),
        nrm(ks[7], (L, D, ML_WIDTH), s_in * DN_BETA),
        nrm(ks[8], (L, D, ML_WIDTH + 2 * ML_HEADS), s_in),
    ], axis=-1)
    ml_conv_w = nrm(ks[9], (L, CONV_K, 2 * ML_WIDTH), CONV_K ** -0.5)
    ml_conv_b = nrm(ks[10], (L, 2 * ML_WIDTH), 0.02)
    f_bias = jnp.linspace(3.0, 6.0, ML_HEADS, dtype=f32)
    ml_gate_b = jnp.concatenate([nrm(ks[11], (L, ML_HEADS), 0.1),
                                 f_bias[None, :] + nrm(ks[12], (L, ML_HEADS), 0.01)], axis=-1)
    ml_norm_g = 1.0 + nrm(ks[13], (L, ML_WIDTH), 0.02)
    w_out = nrm(ks[14], (L, MIX_WIDTH, D), MIX_WIDTH ** -0.5 * DN_BETA)
    ln1_g = 1.0 + nrm(ks[15], (L, D), 0.02)
    ln1_b = nrm(ks[16], (L, D), 0.02)
    w_router = nrm(ks[17], (L, D, N_EXPERTS), D ** -0.5)
    router_bias = nrm(ks[18], (L, N_EXPERTS), 0.01)
    kk = jax.random.split(ks[19], 6)
    moe_w1 = nrm(kk[0], (L, N_EXPERTS, D, EXPERT_FF), D ** -0.5)
    moe_w3 = nrm(kk[1], (L, N_EXPERTS, D, EXPERT_FF), D ** -0.5)
    moe_w2 = nrm(kk[2], (L, N_EXPERTS, EXPERT_FF, D), EXPERT_FF ** -0.5 * DN_BETA)
    sh_w1 = nrm(kk[3], (L, D, SHARED_FF), D ** -0.5)
    sh_w3 = nrm(kk[4], (L, D, SHARED_FF), D ** -0.5)
    sh_w2 = nrm(kk[5], (L, SHARED_FF, D), SHARED_FF ** -0.5 * DN_BETA)
    ln2_g = 1.0 + nrm(ks[20], (L, D), 0.02)
    ln2_b = nrm(ks[21], (L, D), 0.02)
    return {'x': x, 'c': c, 'w_ada': w_ada, 'b_ada': b_ada, 'w_in': w_in,
            'ml_conv_w': ml_conv_w, 'ml_conv_b': ml_conv_b, 'ml_gate_b': ml_gate_b,
            'ml_norm_g': ml_norm_g, 'w_out': w_out, 'ln1_g': ln1_g, 'ln1_b': ln1_b,
            'w_router': w_router, 'router_bias': router_bias, 'moe_w1': moe_w1,
            'moe_w3': moe_w3, 'moe_w2': moe_w2, 'sh_w1': sh_w1, 'sh_w3': sh_w3, 'sh_w2': sh_w2,
            'ln2_g': ln2_g, 'ln2_b': ln2_b}


def reference(x, c, w_ada, b_ada, w_in, ml_conv_w, ml_conv_b, ml_gate_b, ml_norm_g, w_out,
              ln1_g, ln1_b, w_router, router_bias, moe_w1, moe_w3, moe_w2, sh_w1, sh_w3, sh_w2,
              ln2_g, ln2_b):
    for l in range(DEPTH):
        mod = jax.nn.silu(c) @ w_ada[l] + b_ada[l]
        sh1, sc1, g1, sh2, sc2, g2 = [m[:, None, :] for m in jnp.split(mod, 6, axis=-1)]
        u = layer_norm(x) * (1.0 + sc1) + sh1
        mix = hybrid_mixer(u, w_in[l], ml_conv_w[l], ml_conv_b[l], ml_gate_b[l], ml_norm_g[l], w_out[l])
        x = layer_norm(DN_ALPHA * x + g1 * mix, ln1_g[l], ln1_b[l])
        u = layer_norm(x) * (1.0 + sc2) + sh2
        ffn = moe_ffn(u, w_router[l], router_bias[l], moe_w1[l], moe_w3[l], moe_w2[l],
                      sh_w1[l], sh_w3[l], sh_w2[l])
        x = layer_norm(DN_ALPHA * x + g2 * ffn, ln2_g[l], ln2_b[l])
    return x
```

```python
import functools
import math

import jax
import jax.numpy as jnp
from jax import lax
from jax.experimental import pallas as pl
from jax.experimental.pallas import tpu as pltpu

F32 = jnp.float32
BF16 = jnp.bfloat16
HIGHEST = lax.Precision.HIGHEST

SB_HEADS = 8
SB_HEAD_DIM = 64
ML_HEADS = 4
ML_HEAD_DIM = 128
CONV_K = 4
N_EXPERTS = 256
TOP_K = 8
N_GROUPS = 8
TOPK_GROUPS = 4
GROUP_SIZE = N_EXPERTS // N_GROUPS
ROUTED_SCALE = 2.5
DISPATCH_BLOCK = 128
DEPTH = 1
DN_ALPHA = (2 * DEPTH) ** 0.25
LN_EPS = 1e-5
LANES = 128
NEG_INF = float("-inf")
VMEM_LIMIT = 56 * 1024 * 1024


def _cparams(sem):
    return pltpu.CompilerParams(dimension_semantics=sem, vmem_limit_bytes=VMEM_LIMIT)


def _ln(x):
    mu = jnp.mean(x, axis=-1, keepdims=True)
    xc = x - mu
    var = jnp.mean(xc * xc, axis=-1, keepdims=True)
    return xc * lax.rsqrt(var + LN_EPS)


def _dot(a, b):
    return jnp.dot(a, b, preferred_element_type=F32)


def _dot_nt(a, b):
    return lax.dot_general(a, b, (((1,), (1,)), ((), ())), preferred_element_type=F32)


def _dot_tn(a, b):
    return lax.dot_general(a, b, (((0,), (0,)), ((), ())), preferred_element_type=F32)


def _adaln_kernel(c_ref, w_ref, b_ref, o_ref):
    c = c_ref[...]
    s = c * jax.nn.sigmoid(c)
    o_ref[...] = jnp.dot(s, w_ref[...], preferred_element_type=F32, precision=HIGHEST) + b_ref[...]


def _adaln(c, w_ada, b_ada):
    B, D = c.shape
    N = w_ada.shape[1]
    tn = 1024
    return pl.pallas_call(
        _adaln_kernel,
        grid=(N // tn,),
        in_specs=[pl.BlockSpec((B, D), lambda j: (0, 0)),
                  pl.BlockSpec((D, tn), lambda j: (0, j)),
                  pl.BlockSpec((1, tn), lambda j: (0, j))],
        out_specs=pl.BlockSpec((B, tn), lambda j: (0, j)),
        out_shape=jax.ShapeDtypeStruct((B, N), F32),
        compiler_params=_cparams(("arbitrary",)),
        name="adaln",
    )(c, w_ada, b_ada.reshape(1, N))


def _inproj_kernel(x_ref, mod_ref, wsb_ref, wqk_ref, wvo_ref, wg_ref,
                   sb_ref, mqk_ref, mvo_ref, g_ref):
    y = _ln(x_ref[...])
    sh = mod_ref[0:1, :]
    sc = mod_ref[1:2, :]
    u = (y * (1.0 + sc) + sh).astype(BF16)
    sb_ref[...] = _dot(u, wsb_ref[...]).astype(BF16)
    mqk_ref[...] = _dot(u, wqk_ref[...])
    mvo_ref[...] = _dot(u, wvo_ref[...]).astype(BF16)
    g_ref[...] = _dot(u, wg_ref[...])


def _inproj(x, mod, w_sb, w_mqk, w_mvo, w_g, tm):
    B, S, D = x.shape
    nsb, nqk, nvo, ng = w_sb.shape[1], w_mqk.shape[1], w_mvo.shape[1], w_g.shape[1]
    row = lambda n: pl.BlockSpec((None, tm, n), lambda b, i: (b, i, 0))
    full = lambda n: pl.BlockSpec((D, n), lambda b, i: (0, 0))
    return pl.pallas_call(
        _inproj_kernel,
        grid=(B, S // tm),
        in_specs=[row(D), pl.BlockSpec((None, 6, D), lambda b, i: (b, 0, 0)),
                  full(nsb), full(nqk), full(nvo), full(ng)],
        out_specs=[row(nsb), row(nqk), row(nvo), row(ng)],
        out_shape=[jax.ShapeDtypeStruct((B, S, nsb), BF16),
                   jax.ShapeDtypeStruct((B, S, nqk), F32),
                   jax.ShapeDtypeStruct((B, S, nvo), BF16),
                   jax.ShapeDtypeStruct((B, S, ng), F32)],
        compiler_params=_cparams(("arbitrary", "arbitrary")),
        name="inproj",
    )(x, mod, w_sb, w_mqk, w_mvo, w_g)


def _sb_block(z):
    sp = jnp.log(1.0 + jnp.exp(-jnp.abs(z)))
    log_beta = jnp.minimum(z, 0.0) - sp
    log_1m = -jnp.maximum(z, 0.0) - sp
    return log_beta, log_1m


def _sb_kernel(q_ref, k_ref, v_ref, o_ref, *, tq, scale):
    qi = pl.program_id(2)
    q2 = q_ref[...]
    lane = lax.broadcasted_iota(jnp.int32, (1, LANES), 1)
    r = lax.broadcasted_iota(jnp.int32, (tq, tq), 0)
    c = lax.broadcasted_iota(jnp.int32, (tq, tq), 1)
    upper = (r > c).astype(BF16)
    strict = c < r
    out = jnp.zeros((tq, LANES), F32)
    for h in range(2):
        hmask = (lane // SB_HEAD_DIM) == h
        qh = jnp.where(hmask, q2, jnp.zeros_like(q2))

        def block(kb, carry, masked):
            off = pl.multiple_of(kb * tq, tq)
            kblk = k_ref[pl.ds(off, tq), :]
            vblk = v_ref[pl.ds(off, tq), :]
            z = _dot_nt(qh, kblk) * scale
            log_beta, log_1m = _sb_block(z)
            if masked:
                log_1m = jnp.where(strict, log_1m, 0.0)
            after = _dot(log_1m.astype(BF16), upper)
            a = jnp.exp(log_beta + after + carry)
            if masked:
                a = jnp.where(strict, a, 0.0)
            pv = _dot(a.astype(BF16), vblk)
            return pv, carry + jnp.sum(log_1m, axis=1, keepdims=True)

        acc, carry = block(qi, jnp.zeros((tq, 1), F32), True)

        def body(i, st):
            acc, carry = st
            pv, carry = block(qi - 1 - i, carry, False)
            return acc + pv, carry

        acc, carry = lax.fori_loop(0, qi, body, (acc, carry))
        out = jnp.where(hmask, acc, out)
    o_ref[...] = out.astype(o_ref.dtype)


def _sb_attention(sbp, tq):
    B, S, W3 = sbp.shape
    W = W3 // 3
    npair = W // LANES
    kern = functools.partial(_sb_kernel, tq=tq, scale=SB_HEAD_DIM ** -0.5)
    return pl.pallas_call(
        kern,
        grid=(B, npair, S // tq),
        in_specs=[pl.BlockSpec((None, tq, LANES), lambda b, p, i: (b, i, p)),
                  pl.BlockSpec((None, S, LANES), lambda b, p, i: (b, 0, npair + p)),
                  pl.BlockSpec((None, S, LANES), lambda b, p, i: (b, 0, 2 * npair + p))],
        out_specs=pl.BlockSpec((None, tq, LANES), lambda b, p, i: (b, i, p)),
        out_shape=jax.ShapeDtypeStruct((B, S, W), BF16),
        compiler_params=_cparams(("arbitrary", "arbitrary", "arbitrary")),
        name="sb_attention",
    )(sbp, sbp, sbp)


def _mlstm_kernel(qk_ref, vo_ref, g_ref, cw_ref, cb_ref, gb_ref, ng_ref, o_ref,
                  xbuf, ct_ref, m_ref, *, L):
    H, dk = ML_HEADS, ML_HEAD_DIM
    W = H * dk
    ci = pl.program_id(1)

    @pl.when(ci == 0)
    def _():
        xbuf[0:8, :] = jnp.zeros((8, 2 * W), F32)
        ct_ref[...] = jnp.zeros_like(ct_ref)
        m_ref[...] = jnp.zeros_like(m_ref)

    xbuf[8:8 + L, :] = qk_ref[...]
    y = cb_ref[...] + cw_ref[CONV_K - 1:CONV_K, :] * xbuf[8:8 + L, :]
    for j in range(1, CONV_K):
        y = y + cw_ref[CONV_K - 1 - j:CONV_K - j, :] * xbuf[8 - j:8 - j + L, :]
    xbuf[0:8, :] = xbuf[L:L + 8, :]
    qk = y * jax.nn.sigmoid(y)

    g = g_ref[...] + gb_ref[...]
    logf = jax.nn.log_sigmoid(g)
    r = lax.broadcasted_iota(jnp.int32, (L, L), 0)
    c = lax.broadcasted_iota(jnp.int32, (L, L), 1)
    causal = c <= r
    tri = causal.astype(BF16)
    lf_hi = logf.astype(BF16)
    lf_lo = (logf - lf_hi.astype(F32)).astype(BF16)
    bc = _dot(tri, lf_hi) + _dot(tri, lf_lo)
    g_t = g.T
    bc_t = bc.T
    e0 = (lax.broadcasted_iota(jnp.int32, (L, LANES), 1) == 0).astype(F32)

    for h in range(H):
        qh = qk[:, h * dk:(h + 1) * dk].astype(BF16)
        kh = (qk[:, W + h * dk:W + (h + 1) * dk] * (dk ** -0.5)).astype(BF16)
        vh = vo_ref[:, h * dk:(h + 1) * dk].astype(F32)
        oh = vo_ref[:, W + h * dk:W + (h + 1) * dk].astype(F32)
        vaug = jnp.concatenate([vh, e0], axis=1)
        ic_col = g[:, h:h + 1]
        ic_row = g_t[h:h + 1, :]
        bc_col = bc[:, H + h:H + h + 1]
        bc_row = bc_t[H + h:H + h + 1, :]
        m = m_ref[h][:, 0:1]
        ct = ct_ref[h]

        log_d = jnp.where(causal, bc_col - bc_row + ic_row, NEG_INF)
        inter = bc_col + m
        m_t = jnp.maximum(inter, jnp.max(log_d, axis=1, keepdims=True))
        w = _dot_nt(qh, kh) * jnp.exp(log_d - m_t)
        s_inter = jnp.exp(inter - m_t)
        tot = s_inter * _dot(qh, ct.astype(BF16)) + _dot(w.astype(BF16), vaug.astype(BF16))
        num = tot[:, :dk]
        den = tot[:, dk:dk + 1]
        hh = num / jnp.maximum(jnp.abs(den), jnp.exp(-m_t))

        b_last = bc_col[L - 1:L, :]
        log_w = b_last - bc_col + ic_col
        m_new = jnp.maximum(b_last + m, jnp.max(log_w, axis=0, keepdims=True))
        wk = jnp.exp(log_w - m_new)
        decay = jnp.exp(b_last + m - m_new)
        ct_ref[h] = decay * ct + _dot_tn(kh, (wk * vaug).astype(BF16))
        m_ref[h] = jnp.broadcast_to(m_new, (1, LANES))

        hn = _ln(hh) * ng_ref[:, h * dk:(h + 1) * dk]
        o_ref[:, h * dk:(h + 1) * dk] = (jax.nn.sigmoid(oh) * hn).astype(o_ref.dtype)


def _mlstm(mqk, mvo, gates, conv_w, conv_b, gate_b, norm_g, L):
    B, S, W2 = mqk.shape
    W = W2 // 2
    kern = functools.partial(_mlstm_kernel, L=L)
    row = lambda n: pl.BlockSpec((None, L, n), lambda b, i: (b, i, 0))
    cst = lambda a: pl.BlockSpec(a.shape, lambda b, i: (0, 0))
    return pl.pallas_call(
        kern,
        grid=(B, S // L),
        in_specs=[row(W2), row(W2), row(LANES), cst(conv_w), cst(conv_b), cst(gate_b), cst(norm_g)],
        out_specs=row(W),
        out_shape=jax.ShapeDtypeStruct((B, S, W), BF16),
        scratch_shapes=[pltpu.VMEM((L + 8, W2), F32),
                        pltpu.VMEM((ML_HEADS, ML_HEAD_DIM, 2 * ML_HEAD_DIM), F32),
                        pltpu.VMEM((ML_HEADS, 1, LANES), F32)],
        compiler_params=_cparams(("arbitrary", "arbitrary")),
        name="mlstm",
    )(mqk, mvo, gates, conv_w, conv_b, gate_b, norm_g)


def _outproj_kernel(sb_ref, ml_ref, x_ref, mod_ref, wo1_ref, wo2_ref, lg_ref, lb_ref, wr_ref,
                    x1_ref, u2_ref, lo_ref):
    mix = _dot(sb_ref[...], wo1_ref[...]) + _dot(ml_ref[...], wo2_ref[...])
    g1 = mod_ref[2:3, :]
    x1 = _ln(DN_ALPHA * x_ref[...] + g1 * mix) * lg_ref[...] + lb_ref[...]
    x1_ref[...] = x1
    u2 = _ln(x1) * (1.0 + mod_ref[4:5, :]) + mod_ref[3:4, :]
    u2_ref[...] = u2
    lo_ref[...] = jnp.dot(u2, wr_ref[...], preferred_element_type=F32, precision=HIGHEST)


def _outproj(sb, ml, x, mod, wo1, wo2, ln_g, ln_b, w_router, tm):
    B, S, D = x.shape
    W = sb.shape[2]
    E = w_router.shape[1]
    row = lambda n: pl.BlockSpec((None, tm, n), lambda b, i: (b, i, 0))
    cst = lambda a: pl.BlockSpec(a.shape, lambda b, i: (0, 0))
    return pl.pallas_call(
        _outproj_kernel,
        grid=(B, S // tm),
        in_specs=[row(W), row(W), row(D), pl.BlockSpec((None, 6, D), lambda b, i: (b, 0, 0)),
                  cst(wo1), cst(wo2), cst(ln_g), cst(ln_b), cst(w_router)],
        out_specs=[row(D), row(D), row(E)],
        out_shape=[jax.ShapeDtypeStruct((B, S, D), F32),
                   jax.ShapeDtypeStruct((B, S, D), F32),
                   jax.ShapeDtypeStruct((B, S, E), F32)],
        compiler_params=_cparams(("arbitrary", "arbitrary")),
        name="outproj",
    )(sb, ml, x, mod, wo1, wo2, ln_g, ln_b, w_router)


def _group_allreduce(x, lane, op):
    n = x.shape[1]
    sh = 1
    while sh < GROUP_SIZE:
        down = pltpu.roll(x, sh, axis=1)
        up = pltpu.roll(x, n - sh, axis=1)
        x = op(x, jnp.where((lane & sh) != 0, down, up))
        sh *= 2
    return x


def _route_kernel(lo_ref, rb_ref, e_ref, g_ref):
    tm, E = lo_ref.shape
    scores = jax.nn.sigmoid(lo_ref[...])
    sel = scores + rb_ref[...]
    lane = lax.broadcasted_iota(jnp.int32, (tm, E), 1)
    lane_f = lane.astype(F32)
    m1 = _group_allreduce(sel, lane, jnp.maximum)
    is_max = sel == m1
    cnt = _group_allreduce(is_max.astype(F32), lane, jnp.add)
    m2 = _group_allreduce(jnp.where(is_max, NEG_INF, sel), lane, jnp.maximum)
    gscore = m1 + jnp.where(cnt >= 2.0, m1, m2)
    grp = lane // GROUP_SIZE
    rank = jnp.zeros((tm, E), jnp.int32)
    for s in range(1, N_GROUPS):
        other = pltpu.roll(gscore, s * GROUP_SIZE, axis=1)
        beats = (other > gscore) | ((other == gscore) & (grp >= s))
        rank = rank + beats.astype(jnp.int32)
    cur = jnp.where(rank < TOPK_GROUPS, sel, NEG_INF)
    col = lax.broadcasted_iota(jnp.int32, (tm, LANES), 1)
    e_out = jnp.zeros((tm, LANES), jnp.int32)
    g_out = jnp.zeros((tm, LANES), F32)
    gsum = jnp.zeros((tm, 1), F32)
    for k in range(TOP_K):
        m = jnp.max(cur, axis=1, keepdims=True)
        idx = jnp.min(jnp.where(cur == m, lane_f, float(E)), axis=1, keepdims=True)
        pick = lane_f == idx
        gk = jnp.sum(jnp.where(pick, scores, 0.0), axis=1, keepdims=True)
        cur = jnp.where(pick, NEG_INF, cur)
        e_out = jnp.where(col == k, idx.astype(jnp.int32), e_out)
        g_out = jnp.where(col == k, gk, g_out)
        gsum = gsum + gk
    e_ref[...] = e_out
    g_ref[...] = g_out / gsum * ROUTED_SCALE


def _route(logits, router_bias, tm):
    T, E = logits.shape
    return pl.pallas_call(
        _route_kernel,
        grid=(T // tm,),
        in_specs=[pl.BlockSpec((tm, E), lambda i: (i, 0)), pl.BlockSpec((1, E), lambda i: (0, 0))],
        out_specs=[pl.BlockSpec((tm, LANES), lambda i: (i, 0)), pl.BlockSpec((tm, LANES), lambda i: (i, 0))],
        out_shape=[jax.ShapeDtypeStruct((T, LANES), jnp.int32), jax.ShapeDtypeStruct((T, LANES), F32)],
        compiler_params=_cparams(("arbitrary",)),
        name="route",
    )(logits, router_bias.reshape(1, E))


def _expert_kernel(be_ref, nb_ref, tok_ref, u_hbm, w1_ref, w3_ref, w2_ref, y_ref, xbuf, sem):
    i = pl.program_id(0)
    R = DISPATCH_BLOCK

    @pl.when(i < nb_ref[0])
    def _():
        def row_copy(r):
            return pltpu.make_async_copy(u_hbm.at[pl.ds(tok_ref[0, 0, r], 1), :],
                                         xbuf.at[pl.ds(r, 1), :], sem.at[0])

        def start(r, carry):
            row_copy(r).start()
            return carry

        def wait(r, carry):
            row_copy(r).wait()
            return carry

        lax.fori_loop(0, R, start, 0)
        lax.fori_loop(0, R, wait, 0)
        xb = xbuf[...].astype(BF16)
        a = _dot(xb, w1_ref[0].astype(BF16))
        b = _dot(xb, w3_ref[0].astype(BF16))
        hmid = (a * jax.nn.sigmoid(a) * b).astype(BF16)
        y_ref[...] = _dot(hmid, w2_ref[0].astype(BF16))

    @pl.when(i >= nb_ref[0])
    def _():
        y_ref[...] = jnp.zeros_like(y_ref)


def _experts(block_e, nb_used, buf_tok, u2, w1, w3, w2):
    T, D = u2.shape
    nb = block_e.shape[0]
    E, _, FF = w1.shape
    R = DISPATCH_BLOCK
    grid_spec = pltpu.PrefetchScalarGridSpec(
        num_scalar_prefetch=2,
        grid=(nb,),
        in_specs=[pl.BlockSpec((1, 1, R), lambda i, be, n: (i, 0, 0), memory_space=pltpu.SMEM),
                  pl.BlockSpec(memory_space=pl.ANY),
                  pl.BlockSpec((1, D, FF), lambda i, be, n: (be[i], 0, 0)),
                  pl.BlockSpec((1, D, FF), lambda i, be, n: (be[i], 0, 0)),
                  pl.BlockSpec((1, FF, D), lambda i, be, n: (be[i], 0, 0))],
        out_specs=pl.BlockSpec((R, D), lambda i, be, n: (i, 0)),
        scratch_shapes=[pltpu.VMEM((R, D), F32), pltpu.SemaphoreType.DMA((1,))],
    )
    return pl.pallas_call(
        _expert_kernel,
        grid_spec=grid_spec,
        out_shape=jax.ShapeDtypeStruct((nb * R, D), F32),
        compiler_params=_cparams(("arbitrary",)),
        name="experts",
    )(block_e, nb_used, buf_tok.reshape(nb, 1, R), u2, w1, w3, w2)


def _combine_kernel(pos_ref, y_hbm, gw_ref, u_ref, x1_ref, mod_ref, s1_ref, s3_ref, s2_ref,
                    lg_ref, lb_ref, o_ref, ybuf, sem, *, tm):
    def row_copy(j):
        t = j // TOP_K
        k = j - t * TOP_K
        return pltpu.make_async_copy(y_hbm.at[pl.ds(pos_ref[0, 0, j], 1), :],
                                     ybuf.at[k, pl.ds(t, 1), :], sem.at[0])

    def start(j, carry):
        row_copy(j).start()
        return carry

    def wait(j, carry):
        row_copy(j).wait()
        return carry

    lax.fori_loop(0, tm * TOP_K, start, 0)
    u = u_ref[...].astype(BF16)
    a = _dot(u, s1_ref[...])
    b = _dot(u, s3_ref[...])
    ffn = _dot((a * jax.nn.sigmoid(a) * b).astype(BF16), s2_ref[...])
    lax.fori_loop(0, tm * TOP_K, wait, 0)
    gw = gw_ref[...]
    for k in range(TOP_K):
        ffn = ffn + gw[:, k:k + 1] * ybuf[k]
    g2 = mod_ref[5:6, :]
    o_ref[...] = _ln(DN_ALPHA * x1_ref[...] + g2 * ffn) * lg_ref[...] + lb_ref[...]


def _combine(pos, ys, gw, u2, x1, mod, s1, s3, s2, ln_g, ln_b, tm):
    B, S, D = x1.shape
    nt = S // tm
    kern = functools.partial(_combine_kernel, tm=tm)
    row = lambda n: pl.BlockSpec((None, tm, n), lambda b, i: (b, i, 0))
    cst = lambda a: pl.BlockSpec(a.shape, lambda b, i: (0, 0))
    return pl.pallas_call(
        kern,
        grid=(B, nt),
        in_specs=[pl.BlockSpec((1, 1, tm * TOP_K), lambda b, i: (b * nt + i, 0, 0), memory_space=pltpu.SMEM),
                  pl.BlockSpec(memory_space=pl.ANY),
                  row(LANES), row(D), row(D), pl.BlockSpec((None, 6, D), lambda b, i: (b, 0, 0)),
                  cst(s1), cst(s3), cst(s2), cst(ln_g), cst(ln_b)],
        out_specs=row(D),
        out_shape=jax.ShapeDtypeStruct((B, S, D), F32),
        scratch_shapes=[pltpu.VMEM((TOP_K, tm, D), F32), pltpu.SemaphoreType.DMA((1,))],
        compiler_params=_cparams(("arbitrary", "arbitrary")),
        name="combine",
    )(pos.reshape(B * nt, 1, tm * TOP_K), ys, gw, u2, x1, mod, s1, s3, s2, ln_g, ln_b)


def _dispatch_plan(top_e, T):
    TK = T * TOP_K
    R = DISPATCH_BLOCK
    e_flat = top_e.reshape(-1)
    tok_flat = jnp.repeat(jnp.arange(T, dtype=jnp.int32), TOP_K)
    order = jnp.argsort(e_flat)
    se, stok = e_flat[order], tok_flat[order]
    counts = jnp.zeros((N_EXPERTS,), jnp.int32).at[e_flat].add(1)
    offsets = jnp.cumsum(counts) - counts
    pcounts = (counts + R - 1) // R * R
    pend = jnp.cumsum(pcounts)
    poffsets = pend - pcounts
    dest = poffsets[se] + (jnp.arange(TK, dtype=jnp.int32) - offsets[se])
    nrows = -(-TK // R) * R + N_EXPERTS * R
    nb = nrows // R
    buf_tok = jnp.zeros((nrows,), jnp.int32).at[dest].set(stok)
    pos = jnp.zeros((TK,), jnp.int32).at[order].set(dest)
    block_e = jnp.minimum(jnp.searchsorted(pend, jnp.arange(nb, dtype=jnp.int32) * R, side='right'),
                          N_EXPERTS - 1).astype(jnp.int32)
    nb_used = (pend[-1] // R).astype(jnp.int32).reshape(1)
    return buf_tok, block_e, nb_used, pos


def kernel(x, c, w_ada, b_ada, w_in, ml_conv_w, ml_conv_b, ml_gate_b, ml_norm_g, w_out, ln1_g, ln1_b,
           w_router, router_bias, moe_w1, moe_w3, moe_w2, sh_w1, sh_w3, sh_w2, ln2_g, ln2_b):
    B, S, D = x.shape
    T = B * S
    SBW = SB_HEADS * SB_HEAD_DIM
    MLW = ML_HEADS * ML_HEAD_DIM
    for l in range(DEPTH):
        mod = _adaln(c, w_ada[l], b_ada[l]).reshape(B, 6, D)

        wi = w_in[l]
        c0 = 3 * SBW
        w_sb = wi[:, :c0].astype(BF16)
        w_mqk = wi[:, c0:c0 + 2 * MLW].astype(BF16)
        w_mvo = wi[:, c0 + 2 * MLW:c0 + 4 * MLW].astype(BF16)
        w_g = jnp.pad(wi[:, c0 + 4 * MLW:], ((0, 0), (0, LANES - 2 * ML_HEADS))).astype(BF16)
        sbp, mqk, mvo, gates = _inproj(x, mod, w_sb, w_mqk, w_mvo, w_g, tm=min(512, S))

        sb = _sb_attention(sbp, tq=min(256, S))

        gate_b = jnp.pad(ml_gate_b[l], (0, LANES - 2 * ML_HEADS)).reshape(1, LANES)
        ml = _mlstm(mqk, mvo, gates, ml_conv_w[l], ml_conv_b[l].reshape(1, -1), gate_b,
                    ml_norm_g[l].reshape(1, -1), L=min(256, S))

        wo = w_out[l].astype(BF16)
        x1, u2, logits = _outproj(sb, ml, x, mod, wo[:SBW], wo[SBW:], ln1_g[l].reshape(1, D),
                                  ln1_b[l].reshape(1, D), w_router[l], tm=min(512, S))

        top_e, gw = _route(logits.reshape(T, N_EXPERTS), router_bias[l], tm=min(512, T))
        buf_tok, block_e, nb_used, pos = _dispatch_plan(top_e[:, :TOP_K], T)
        ys = _experts(block_e, nb_used, buf_tok, u2.reshape(T, D), moe_w1[l], moe_w3[l], moe_w2[l])
        x = _combine(pos, ys, gw.reshape(B, S, LANES), u2, x1, mod, sh_w1[l].astype(BF16),
                     sh_w3[l].astype(BF16), sh_w2[l].astype(BF16), ln2_g[l].reshape(1, D),
                     ln2_b[l].reshape(1, D), tm=min(128, S))
    return x
```

```python
import functools
import math

import jax
import jax.numpy as jnp
from jax import lax
from jax.experimental import pallas as pl
from jax.experimental.pallas import tpu as pltpu

F32 = jnp.float32
BF16 = jnp.bfloat16
HIGHEST = lax.Precision.HIGHEST

SB_HEADS = 8
SB_HEAD_DIM = 64
ML_HEADS = 4
ML_HEAD_DIM = 128
CONV_K = 4
N_EXPERTS = 256
TOP_K = 8
N_GROUPS = 8
TOPK_GROUPS = 4
GROUP_SIZE = N_EXPERTS // N_GROUPS
ROUTED_SCALE = 2.5
EXPERT_BLOCK_ROWS = 256
DEPTH = 1
DN_ALPHA = (2 * DEPTH) ** 0.25
LN_EPS = 1e-5
LANES = 128
SUBLANES = 8
NEG_INF = float("-inf")
SB_CUTOFF = 104.0
VMEM_LIMIT = 56 * 1024 * 1024


def _cparams(sem):
    return pltpu.CompilerParams(dimension_semantics=sem, vmem_limit_bytes=VMEM_LIMIT)


def _ln(x):
    mu = jnp.mean(x, axis=-1, keepdims=True)
    xc = x - mu
    var = jnp.mean(xc * xc, axis=-1, keepdims=True)
    return xc * lax.rsqrt(var + LN_EPS)


def _dot(a, b):
    return jnp.dot(a, b, preferred_element_type=F32)


def _dot_nt(a, b):
    return lax.dot_general(a, b, (((1,), (1,)), ((), ())), preferred_element_type=F32)


def _dot_tn(a, b):
    return lax.dot_general(a, b, (((0,), (0,)), ((), ())), preferred_element_type=F32)


def _pack_halves(v):
    w = v.shape[1] // 2
    lo = lax.bitcast_convert_type(v[:, :w].astype(BF16).astype(F32), jnp.uint32) >> 16
    hi = lax.bitcast_convert_type(v[:, w:].astype(BF16).astype(F32), jnp.uint32) & jnp.uint32(0xFFFF0000)
    return hi | lo


def _unpack_halves(p):
    lo = lax.bitcast_convert_type(p << 16, F32).astype(BF16)
    hi = lax.bitcast_convert_type(p & jnp.uint32(0xFFFF0000), F32).astype(BF16)
    return lo, hi


def _adaln_kernel(c_ref, w_ref, b_ref, o_ref):
    c = c_ref[...]
    s = c * jax.nn.sigmoid(c)
    o_ref[...] = jnp.dot(s, w_ref[...], preferred_element_type=F32, precision=HIGHEST) + b_ref[...]


def _adaln(c, w_ada, b_ada):
    B, D = c.shape
    N = w_ada.shape[1]
    tn = 1024
    return pl.pallas_call(
        _adaln_kernel,
        grid=(N // tn,),
        in_specs=[pl.BlockSpec((B, D), lambda j: (0, 0)),
                  pl.BlockSpec((D, tn), lambda j: (0, j)),
                  pl.BlockSpec((1, tn), lambda j: (0, j))],
        out_specs=pl.BlockSpec((B, tn), lambda j: (0, j)),
        out_shape=jax.ShapeDtypeStruct((B, N), F32),
        compiler_params=_cparams(("arbitrary",)),
        name="adaln",
    )(c, w_ada, b_ada.reshape(1, N))


def _inproj_kernel(x_ref, mod_ref, wsb_ref, wqk_ref, wvo_ref, wg_ref,
                   sb_ref, mqk_ref, mvo_ref, g_ref):
    y = _ln(x_ref[...])
    sh = mod_ref[0:1, :]
    sc = mod_ref[1:2, :]
    u = (y * (1.0 + sc) + sh).astype(BF16)
    sb_ref[...] = _dot(u, wsb_ref[...]).astype(BF16)
    mqk_ref[...] = _dot(u, wqk_ref[...])
    mvo_ref[...] = _dot(u, wvo_ref[...]).astype(BF16)
    g_ref[...] = _dot(u, wg_ref[...])


def _inproj(x, mod, w_sb, w_mqk, w_mvo, w_g, tm):
    B, S, D = x.shape
    nsb, nqk, nvo, ng = w_sb.shape[1], w_mqk.shape[1], w_mvo.shape[1], w_g.shape[1]
    row = lambda n: pl.BlockSpec((None, tm, n), lambda b, i: (b, i, 0))
    full = lambda n: pl.BlockSpec((D, n), lambda b, i: (0, 0))
    return pl.pallas_call(
        _inproj_kernel,
        grid=(B, S // tm),
        in_specs=[row(D), pl.BlockSpec((None, 6, D), lambda b, i: (b, 0, 0)),
                  full(nsb), full(nqk), full(nvo), full(ng)],
        out_specs=[row(nsb), row(nqk), row(nvo), row(ng)],
        out_shape=[jax.ShapeDtypeStruct((B, S, nsb), BF16),
                   jax.ShapeDtypeStruct((B, S, nqk), F32),
                   jax.ShapeDtypeStruct((B, S, nvo), BF16),
                   jax.ShapeDtypeStruct((B, S, ng), F32)],
        compiler_params=_cparams(("arbitrary", "arbitrary")),
        name="inproj",
    )(x, mod, w_sb, w_mqk, w_mvo, w_g)


def _sb_block(z):
    sp = jnp.log(1.0 + jnp.exp(-jnp.abs(z)))
    log_beta = jnp.minimum(z, 0.0) - sp
    log_1m = -jnp.maximum(z, 0.0) - sp
    return log_beta, log_1m


def _sb_kernel(q_ref, k_ref, v_ref, o_ref, *, tq, scale):
    qi = pl.program_id(2)
    q2 = q_ref[...]
    lane = lax.broadcasted_iota(jnp.int32, (1, LANES), 1)
    r = lax.broadcasted_iota(jnp.int32, (tq, tq), 0)
    c = lax.broadcasted_iota(jnp.int32, (tq, tq), 1)
    upper = (r > c).astype(BF16)
    strict = c < r
    out = jnp.zeros((tq, LANES), F32)
    for h in range(2):
        hmask = (lane // SB_HEAD_DIM) == h
        qh = jnp.where(hmask, q2, jnp.zeros_like(q2))

        def block(kb, carry, masked):
            off = pl.multiple_of(kb * tq, tq)
            kblk = k_ref[pl.ds(off, tq), :]
            vblk = v_ref[pl.ds(off, tq), :]
            z = _dot_nt(qh, kblk) * scale
            log_beta, log_1m = _sb_block(z)
            if masked:
                log_1m = jnp.where(strict, log_1m, 0.0)
            after = _dot(log_1m.astype(BF16), upper)
            a = jnp.exp(log_beta + after + carry)
            if masked:
                a = jnp.where(strict, a, 0.0)
            pv = _dot(a.astype(BF16), vblk)
            return pv, carry + jnp.sum(log_1m, axis=1, keepdims=True)

        acc, carry = block(qi, jnp.zeros((tq, 1), F32), True)

        def cond(st):
            i, _, _, top = st
            return jnp.logical_and(i < qi, top > -SB_CUTOFF)

        def body(st):
            i, acc, carry, _ = st
            pv, carry = block(qi - 1 - i, carry, False)
            return i + 1, acc + pv, carry, jnp.max(carry)

        _, acc, carry, _ = lax.while_loop(cond, body, (jnp.int32(0), acc, carry, jnp.max(carry)))
        out = jnp.where(hmask, acc, out)
    o_ref[...] = out.astype(o_ref.dtype)


def _sb_attention(sbp, tq):
    B, S, W3 = sbp.shape
    W = W3 // 3
    npair = W // LANES
    kern = functools.partial(_sb_kernel, tq=tq, scale=SB_HEAD_DIM ** -0.5)
    return pl.pallas_call(
        kern,
        grid=(B, npair, S // tq),
        in_specs=[pl.BlockSpec((None, tq, LANES), lambda b, p, i: (b, i, p)),
                  pl.BlockSpec((None, S, LANES), lambda b, p, i: (b, 0, npair + p)),
                  pl.BlockSpec((None, S, LANES), lambda b, p, i: (b, 0, 2 * npair + p))],
        out_specs=pl.BlockSpec((None, tq, LANES), lambda b, p, i: (b, i, p)),
        out_shape=jax.ShapeDtypeStruct((B, S, W), BF16),
        compiler_params=_cparams(("arbitrary", "arbitrary", "arbitrary")),
        name="sb_attention",
    )(sbp, sbp, sbp)


def _mlstm_kernel(qk_ref, vo_ref, g_ref, cw_ref, cb_ref, gb_ref, ng_ref, o_ref,
                  xbuf, ct_ref, m_ref, *, L):
    H, dk = ML_HEADS, ML_HEAD_DIM
    W = H * dk
    ci = pl.program_id(1)

    @pl.when(ci == 0)
    def _():
        xbuf[0:8, :] = jnp.zeros((8, 2 * W), F32)
        ct_ref[...] = jnp.zeros_like(ct_ref)
        m_ref[...] = jnp.zeros_like(m_ref)

    xbuf[8:8 + L, :] = qk_ref[...]
    y = cb_ref[...] + cw_ref[CONV_K - 1:CONV_K, :] * xbuf[8:8 + L, :]
    for j in range(1, CONV_K):
        y = y + cw_ref[CONV_K - 1 - j:CONV_K - j, :] * xbuf[8 - j:8 - j + L, :]
    xbuf[0:8, :] = xbuf[L:L + 8, :]
    qk = y * jax.nn.sigmoid(y)

    g = g_ref[...] + gb_ref[...]
    logf = jax.nn.log_sigmoid(g)
    r = lax.broadcasted_iota(jnp.int32, (L, L), 0)
    c = lax.broadcasted_iota(jnp.int32, (L, L), 1)
    causal = c <= r
    tri = causal.astype(BF16)
    lf_hi = logf.astype(BF16)
    lf_lo = (logf - lf_hi.astype(F32)).astype(BF16)
    bc = _dot(tri, lf_hi) + _dot(tri, lf_lo)
    g_t = g.T
    bc_t = bc.T
    e0 = (lax.broadcasted_iota(jnp.int32, (L, LANES), 1) == 0).astype(F32)

    for h in range(H):
        qh = qk[:, h * dk:(h + 1) * dk].astype(BF16)
        kh = (qk[:, W + h * dk:W + (h + 1) * dk] * (dk ** -0.5)).astype(BF16)
        vh = vo_ref[:, h * dk:(h + 1) * dk].astype(F32)
        oh = vo_ref[:, W + h * dk:W + (h + 1) * dk].astype(F32)
        vaug = jnp.concatenate([vh, e0], axis=1)
        ic_col = g[:, h:h + 1]
        ic_row = g_t[h:h + 1, :]
        bc_col = bc[:, H + h:H + h + 1]
        bc_row = bc_t[H + h:H + h + 1, :]
        m = m_ref[h][:, 0:1]
        ct = ct_ref[h]

        log_d = jnp.where(causal, bc_col - bc_row + ic_row, NEG_INF)
        inter = bc_col + m
        m_t = jnp.maximum(inter, jnp.max(log_d, axis=1, keepdims=True))
        w = _dot_nt(qh, kh) * jnp.exp(log_d - m_t)
        s_inter = jnp.exp(inter - m_t)
        tot = s_inter * _dot(qh, ct.astype(BF16)) + _dot(w.astype(BF16), vaug.astype(BF16))
        num = tot[:, :dk]
        den = tot[:, dk:dk + 1]
        hh = num / jnp.maximum(jnp.abs(den), jnp.exp(-m_t))

        b_last = bc_col[L - 1:L, :]
        log_w = b_last - bc_col + ic_col
        m_new = jnp.maximum(b_last + m, jnp.max(log_w, axis=0, keepdims=True))
        wk = jnp.exp(log_w - m_new)
        decay = jnp.exp(b_last + m - m_new)
        ct_ref[h] = decay * ct + _dot_tn(kh, (wk * vaug).astype(BF16))
        m_ref[h] = jnp.broadcast_to(m_new, (1, LANES))

        hn = _ln(hh) * ng_ref[:, h * dk:(h + 1) * dk]
        o_ref[:, h * dk:(h + 1) * dk] = (jax.nn.sigmoid(oh) * hn).astype(o_ref.dtype)


def _mlstm(mqk, mvo, gates, conv_w, conv_b, gate_b, norm_g, L):
    B, S, W2 = mqk.shape
    W = W2 // 2
    kern = functools.partial(_mlstm_kernel, L=L)
    row = lambda n: pl.BlockSpec((None, L, n), lambda b, i: (b, i, 0))
    cst = lambda a: pl.BlockSpec(a.shape, lambda b, i: (0, 0))
    return pl.pallas_call(
        kern,
        grid=(B, S // L),
        in_specs=[row(W2), row(W2), row(LANES), cst(conv_w), cst(conv_b), cst(gate_b), cst(norm_g)],
        out_specs=row(W),
        out_shape=jax.ShapeDtypeStruct((B, S, W), BF16),
        scratch_shapes=[pltpu.VMEM((L + 8, W2), F32),
                        pltpu.VMEM((ML_HEADS, ML_HEAD_DIM, 2 * ML_HEAD_DIM), F32),
                        pltpu.VMEM((ML_HEADS, 1, LANES), F32)],
        compiler_params=_cparams(("arbitrary", "arbitrary")),
        name="mlstm",
    )(mqk, mvo, gates, conv_w, conv_b, gate_b, norm_g)


def _outproj_kernel(sb_ref, ml_ref, x_ref, mod_ref, wo1_ref, wo2_ref, lg_ref, lb_ref, wr_ref,
                    x1_ref, u2_ref, lo_ref):
    mix = _dot(sb_ref[...], wo1_ref[...]) + _dot(ml_ref[...], wo2_ref[...])
    g1 = mod_ref[2:3, :]
    x1 = _ln(DN_ALPHA * x_ref[...] + g1 * mix) * lg_ref[...] + lb_ref[...]
    x1_ref[...] = x1
    u2 = _ln(x1) * (1.0 + mod_ref[4:5, :]) + mod_ref[3:4, :]
    u2_ref[...] = _pack_halves(u2)
    lo_ref[...] = jnp.dot(u2, wr_ref[...], preferred_element_type=F32, precision=HIGHEST)


def _outproj(sb, ml, x, mod, wo1, wo2, ln_g, ln_b, w_router, tm):
    B, S, D = x.shape
    W = sb.shape[2]
    E = w_router.shape[1]
    row = lambda n: pl.BlockSpec((None, tm, n), lambda b, i: (b, i, 0))
    cst = lambda a: pl.BlockSpec(a.shape, lambda b, i: (0, 0))
    return pl.pallas_call(
        _outproj_kernel,
        grid=(B, S // tm),
        in_specs=[row(W), row(W), row(D), pl.BlockSpec((None, 6, D), lambda b, i: (b, 0, 0)),
                  cst(wo1), cst(wo2), cst(ln_g), cst(ln_b), cst(w_router)],
        out_specs=[row(D), row(D // 2), row(E)],
        out_shape=[jax.ShapeDtypeStruct((B, S, D), F32),
                   jax.ShapeDtypeStruct((B, S, D // 2), jnp.uint32),
                   jax.ShapeDtypeStruct((B, S, E), F32)],
        compiler_params=_cparams(("arbitrary", "arbitrary")),
        name="outproj",
    )(sb, ml, x, mod, wo1, wo2, ln_g, ln_b, w_router)


def _group_allreduce(x, lane, op):
    n = x.shape[1]
    sh = 1
    while sh < GROUP_SIZE:
        down = pltpu.roll(x, sh, axis=1)
        up = pltpu.roll(x, n - sh, axis=1)
        x = op(x, jnp.where((lane & sh) != 0, down, up))
        sh *= 2
    return x


def _route_kernel(lo_ref, rb_ref, e_ref, g_ref, cnt_ref, cnt_scr):
    tm, E = lo_ref.shape

    @pl.when(pl.program_id(0) == 0)
    def _():
        cnt_scr[...] = jnp.zeros_like(cnt_scr)

    scores = jax.nn.sigmoid(lo_ref[...])
    sel = scores + rb_ref[...]
    lane = lax.broadcasted_iota(jnp.int32, (tm, E), 1)
    lane_f = lane.astype(F32)
    m1 = _group_allreduce(sel, lane, jnp.maximum)
    is_max = sel == m1
    cnt = _group_allreduce(is_max.astype(F32), lane, jnp.add)
    m2 = _group_allreduce(jnp.where(is_max, NEG_INF, sel), lane, jnp.maximum)
    gscore = m1 + jnp.where(cnt >= 2.0, m1, m2)
    grp = lane // GROUP_SIZE
    rank = jnp.zeros((tm, E), jnp.int32)
    for s in range(1, N_GROUPS):
        other = pltpu.roll(gscore, s * GROUP_SIZE, axis=1)
        beats = (other > gscore) | ((other == gscore) & (grp >= s))
        rank = rank + beats.astype(jnp.int32)
    cur = jnp.where(rank < TOPK_GROUPS, sel, NEG_INF)
    idxs = []
    chosen = jnp.zeros((tm, E), F32)
    for k in range(TOP_K):
        m = jnp.max(cur, axis=1, keepdims=True)
        idx = jnp.min(jnp.where(cur == m, lane_f, float(E)), axis=1, keepdims=True)
        pick = lane_f == idx
        cur = jnp.where(pick, NEG_INF, cur)
        chosen = jnp.where(pick, 1.0, chosen)
        idxs.append(idx)
    r = lax.broadcasted_iota(jnp.int32, (tm, tm), 0)
    c = lax.broadcasted_iota(jnp.int32, (tm, tm), 1)
    before = _dot((c < r).astype(BF16), chosen.astype(BF16)) + cnt_scr[...]
    col = lax.broadcasted_iota(jnp.int32, (tm, LANES), 1)
    e_out = jnp.zeros((tm, LANES), jnp.int32)
    g_out = jnp.zeros((tm, LANES), F32)
    gsum = jnp.zeros((tm, 1), F32)
    for k in range(TOP_K):
        pick = lane_f == idxs[k]
        gk = jnp.sum(jnp.where(pick, scores, 0.0), axis=1, keepdims=True)
        rk = jnp.sum(jnp.where(pick, before, 0.0), axis=1, keepdims=True)
        e_out = jnp.where(col == k, idxs[k].astype(jnp.int32), e_out)
        e_out = jnp.where(col == TOP_K + k, rk.astype(jnp.int32), e_out)
        g_out = jnp.where(col == k, gk, g_out)
        gsum = gsum + gk
    e_ref[...] = e_out
    g_ref[...] = g_out / gsum * ROUTED_SCALE
    cnt_scr[...] = cnt_scr[...] + jnp.sum(chosen, axis=0, keepdims=True)
    cnt_ref[...] = cnt_scr[...]


def _route(logits, router_bias, tm):
    T, E = logits.shape
    return pl.pallas_call(
        _route_kernel,
        grid=(T // tm,),
        in_specs=[pl.BlockSpec((tm, E), lambda i: (i, 0)), pl.BlockSpec((1, E), lambda i: (0, 0))],
        out_specs=[pl.BlockSpec((tm, LANES), lambda i: (i, 0)), pl.BlockSpec((tm, LANES), lambda i: (i, 0)),
                   pl.BlockSpec((1, E), lambda i: (0, 0))],
        out_shape=[jax.ShapeDtypeStruct((T, LANES), jnp.int32), jax.ShapeDtypeStruct((T, LANES), F32),
                   jax.ShapeDtypeStruct((1, E), F32)],
        scratch_shapes=[pltpu.VMEM((1, E), F32)],
        compiler_params=_cparams(("arbitrary",)),
        name="route",
    )(logits, router_bias.reshape(1, E))


def _dest_kernel(e_ref, off_ref, d_ref):
    tm = e_ref.shape[0]
    E = off_ref.shape[1]
    ev = e_ref[...]
    lane = lax.broadcasted_iota(jnp.int32, (tm, E), 1)
    col = lax.broadcasted_iota(jnp.int32, (tm, LANES), 1)
    off = off_ref[...]
    out = jnp.zeros((tm, LANES), jnp.int32)
    for k in range(TOP_K):
        base = jnp.sum(jnp.where(lane == ev[:, k:k + 1], off, 0.0), axis=1, keepdims=True)
        out = jnp.where(col == k, base.astype(jnp.int32) + ev[:, TOP_K + k:TOP_K + k + 1], out)
    d_ref[...] = out


def _dest(e_rank, offsets, tm):
    T = e_rank.shape[0]
    E = offsets.shape[1]
    return pl.pallas_call(
        _dest_kernel,
        grid=(T // tm,),
        in_specs=[pl.BlockSpec((tm, LANES), lambda i: (i, 0)), pl.BlockSpec((1, E), lambda i: (0, 0))],
        out_specs=pl.BlockSpec((tm, LANES), lambda i: (i, 0)),
        out_shape=jax.ShapeDtypeStruct((T, LANES), jnp.int32),
        compiler_params=_cparams(("arbitrary",)),
        name="dest",
    )(e_rank, offsets)


def _dispatch_kernel(pad_off_ref, pad_n_ref, nb_ref, dst_ref, u_ref, xs_hbm, zbuf, sem, zsem, *, tm):
    n = tm * TOP_K
    rb = zbuf.shape[0]

    def row_copy(j):
        t = j // TOP_K
        return pltpu.make_async_copy(u_ref.at[pl.ds(t, 1), :], xs_hbm.at[pl.ds(dst_ref[0, 0, j], 1), :],
                                     sem.at[0])

    def start(j, carry):
        row_copy(j).start()
        return carry

    lax.fori_loop(0, n, start, 0, unroll=8)

    @pl.when(pl.program_id(0) == 0)
    def _():
        zbuf[...] = jnp.zeros_like(zbuf)
        nexp = pad_off_ref.shape[0]

        def zero_rows(off, rows):
            return pltpu.make_async_copy(zbuf.at[pl.ds(0, rows), :], xs_hbm.at[pl.ds(off, rows), :], zsem.at[0])

        def for_each(fn):
            def per_expert(e, carry):
                start = pad_off_ref[e]
                nhead = jnp.minimum((-start) & (SUBLANES - 1), pad_n_ref[e])
                rest = pad_n_ref[e] - nhead

                def head(i, c):
                    fn(zero_rows(start + i, 1))
                    return c

                lax.fori_loop(0, nhead, head, 0)
                for bit in range(SUBLANES.bit_length() - 1, zbuf.shape[0].bit_length() - 1):
                    @pl.when((rest >> bit) & 1 == 1)
                    def _():
                        off = pl.multiple_of(start + nhead + (rest & ((1 << bit) - 1)), SUBLANES)
                        fn(zero_rows(off, 1 << bit))
                return carry
            lax.fori_loop(0, nexp, per_expert, 0)

            def unused_block(b, carry):
                fn(zero_rows(pl.multiple_of(b * rb, rb), rb))
                return carry
            lax.fori_loop(nb_ref[0], xs_hbm.shape[0] // rb, unused_block, 0)

        for_each(lambda cp: cp.start())
        for_each(lambda cp: cp.wait())

    for _ in range(TOP_K):
        pltpu.make_async_copy(u_ref, xs_hbm.at[pl.ds(0, tm), :], sem.at[0]).wait()


def _dispatch(pad_off, pad_n, nb_used, dest, u2p, nrows, rb, tm):
    T, W = u2p.shape
    kern = functools.partial(_dispatch_kernel, tm=tm)
    grid_spec = pltpu.PrefetchScalarGridSpec(
        num_scalar_prefetch=3,
        grid=(T // tm,),
        in_specs=[pl.BlockSpec((1, 1, tm * TOP_K), lambda i, po, pn, nb: (i, 0, 0), memory_space=pltpu.SMEM),
                  pl.BlockSpec((tm, W), lambda i, po, pn, nb: (i, 0))],
        out_specs=pl.BlockSpec(memory_space=pl.ANY),
        scratch_shapes=[pltpu.VMEM((rb, W), jnp.uint32), pltpu.SemaphoreType.DMA((1,)),
                        pltpu.SemaphoreType.DMA((1,))],
    )
    return pl.pallas_call(
        kern,
        grid_spec=grid_spec,
        out_shape=jax.ShapeDtypeStruct((nrows, W), jnp.uint32),
        compiler_params=_cparams(("arbitrary",)),
        name="dispatch",
    )(pad_off, pad_n, nb_used, dest.reshape(T // tm, 1, tm * TOP_K), u2p)


def _expert_kernel(be_ref, nb_ref, xs_ref, w1_ref, w3_ref, w2_ref, y_ref, w1b, w3b, w2b):
    i = pl.program_id(0)
    half = xs_ref.shape[1]
    changed = jnp.logical_or(i == 0, be_ref[i] != be_ref[jnp.maximum(i - 1, 0)])

    @pl.when(jnp.logical_and(i < nb_ref[0], changed))
    def _():
        w1b[...] = w1_ref[0].astype(BF16)
        w3b[...] = w3_ref[0].astype(BF16)
        w2b[...] = w2_ref[0].astype(BF16)

    @pl.when(i < nb_ref[0])
    def _():
        lo, hi = _unpack_halves(xs_ref[...])
        a = _dot(lo, w1b[0:half, :]) + _dot(hi, w1b[half:, :])
        b = _dot(lo, w3b[0:half, :]) + _dot(hi, w3b[half:, :])
        hmid = (a * jax.nn.sigmoid(a) * b).astype(BF16)
        y_ref[...] = _dot(hmid, w2b[...])

    @pl.when(i >= nb_ref[0])
    def _():
        y_ref[...] = jnp.zeros_like(y_ref)


def _experts(block_e, nb_used, xs, w1, w3, w2, rb):
    nrows, W = xs.shape
    nb = nrows // rb
    E, D, FF = w1.shape
    grid_spec = pltpu.PrefetchScalarGridSpec(
        num_scalar_prefetch=2,
        grid=(nb,),
        in_specs=[pl.BlockSpec((rb, W), lambda i, be, n: (i, 0)),
                  pl.BlockSpec((1, D, FF), lambda i, be, n: (be[i], 0, 0)),
                  pl.BlockSpec((1, D, FF), lambda i, be, n: (be[i], 0, 0)),
                  pl.BlockSpec((1, FF, D), lambda i, be, n: (be[i], 0, 0))],
        out_specs=pl.BlockSpec((rb, D), lambda i, be, n: (i, 0)),
        scratch_shapes=[pltpu.VMEM((D, FF), BF16), pltpu.VMEM((D, FF), BF16), pltpu.VMEM((FF, D), BF16)],
    )
    return pl.pallas_call(
        _expert_kernel,
        grid_spec=grid_spec,
        out_shape=jax.ShapeDtypeStruct((nrows, D), F32),
        compiler_params=_cparams(("arbitrary",)),
        name="experts",
    )(block_e, nb_used, xs, w1, w3, w2)


def _combine_kernel(pos_ref, y_hbm, gw_ref, u_ref, x1_ref, mod_ref, s1_ref, s3_ref, s2_ref,
                    lg_ref, lb_ref, o_ref, ybuf, sem, *, tm):
    n = tm * TOP_K
    half = u_ref.shape[1]

    def start(j, carry):
        t = j // TOP_K
        k = j - t * TOP_K
        pltpu.make_async_copy(y_hbm.at[pl.ds(pos_ref[0, 0, j], 1), :], ybuf.at[pl.ds(k * tm + t, 1), :],
                              sem.at[0]).start()
        return carry

    lax.fori_loop(0, n, start, 0, unroll=8)
    lo, hi = _unpack_halves(u_ref[...])
    a = _dot(lo, s1_ref[0:half, :]) + _dot(hi, s1_ref[half:, :])
    b = _dot(lo, s3_ref[0:half, :]) + _dot(hi, s3_ref[half:, :])
    ffn = _dot((a * jax.nn.sigmoid(a) * b).astype(BF16), s2_ref[...])
    pltpu.make_async_copy(y_hbm.at[pl.ds(0, n), :], ybuf, sem.at[0]).wait()
    gw = gw_ref[...]
    for k in range(TOP_K):
        ffn = ffn + gw[:, k:k + 1] * ybuf[k * tm:(k + 1) * tm, :]
    g2 = mod_ref[5:6, :]
    o_ref[...] = _ln(DN_ALPHA * x1_ref[...] + g2 * ffn) * lg_ref[...] + lb_ref[...]


def _combine(pos, ys, gw, u2p, x1, mod, s1, s3, s2, ln_g, ln_b, tm):
    B, S, D = x1.shape
    nt = S // tm
    kern = functools.partial(_combine_kernel, tm=tm)
    row = lambda n: pl.BlockSpec((None, tm, n), lambda b, i: (b, i, 0))
    cst = lambda a: pl.BlockSpec(a.shape, lambda b, i: (0, 0))
    return pl.pallas_call(
        kern,
        grid=(B, nt),
        in_specs=[pl.BlockSpec((1, 1, tm * TOP_K), lambda b, i: (b * nt + i, 0, 0), memory_space=pltpu.SMEM),
                  pl.BlockSpec(memory_space=pl.ANY),
                  row(LANES), row(D // 2), row(D), pl.BlockSpec((None, 6, D), lambda b, i: (b, 0, 0)),
                  cst(s1), cst(s3), cst(s2), cst(ln_g), cst(ln_b)],
        out_specs=row(D),
        out_shape=jax.ShapeDtypeStruct((B, S, D), F32),
        scratch_shapes=[pltpu.VMEM((TOP_K * tm, D), F32), pltpu.SemaphoreType.DMA((1,))],
        compiler_params=_cparams(("arbitrary", "arbitrary")),
        name="combine",
    )(pos.reshape(B * nt, 1, tm * TOP_K), ys, gw, u2p, x1, mod, s1, s3, s2, ln_g, ln_b)


def _block_layout(counts, T, rb):
    counts = counts.reshape(-1).astype(jnp.int32)
    pcounts = (counts + rb - 1) // rb * rb
    pend = jnp.cumsum(pcounts)
    poffsets = pend - pcounts
    nb = (T * TOP_K) // rb + N_EXPERTS
    block_e = jnp.minimum(jnp.searchsorted(pend, jnp.arange(nb, dtype=jnp.int32) * rb, side='right'),
                          N_EXPERTS - 1).astype(jnp.int32)
    nb_used = (pend[-1] // rb).astype(jnp.int32).reshape(1)
    return poffsets, poffsets + counts, pcounts - counts, block_e, nb_used, nb * rb


def kernel(x, c, w_ada, b_ada, w_in, ml_conv_w, ml_conv_b, ml_gate_b, ml_norm_g, w_out, ln1_g, ln1_b,
           w_router, router_bias, moe_w1, moe_w3, moe_w2, sh_w1, sh_w3, sh_w2, ln2_g, ln2_b):
    B, S, D = x.shape
    T = B * S
    SBW = SB_HEADS * SB_HEAD_DIM
    MLW = ML_HEADS * ML_HEAD_DIM
    rb = EXPERT_BLOCK_ROWS
    for l in range(DEPTH):
        mod = _adaln(c, w_ada[l], b_ada[l]).reshape(B, 6, D)

        wi = w_in[l]
        c0 = 3 * SBW
        w_sb = wi[:, :c0].astype(BF16)
        w_mqk = wi[:, c0:c0 + 2 * MLW].astype(BF16)
        w_mvo = wi[:, c0 + 2 * MLW:c0 + 4 * MLW].astype(BF16)
        w_g = jnp.pad(wi[:, c0 + 4 * MLW:], ((0, 0), (0, LANES - 2 * ML_HEADS))).astype(BF16)
        sbp, mqk, mvo, gates = _inproj(x, mod, w_sb, w_mqk, w_mvo, w_g, tm=min(512, S))

        sb = _sb_attention(sbp, tq=min(256, S))

        gate_b = jnp.pad(ml_gate_b[l], (0, LANES - 2 * ML_HEADS)).reshape(1, LANES)
        ml = _mlstm(mqk, mvo, gates, ml_conv_w[l], ml_conv_b[l].reshape(1, -1), gate_b,
                    ml_norm_g[l].reshape(1, -1), L=min(256, S))

        wo = w_out[l].astype(BF16)
        x1, u2p, logits = _outproj(sb, ml, x, mod, wo[:SBW], wo[SBW:], ln1_g[l].reshape(1, D),
                                   ln1_b[l].reshape(1, D), w_router[l], tm=min(512, S))

        e_rank, gw, counts = _route(logits.reshape(T, N_EXPERTS), router_bias[l], tm=min(512, T))
        poffsets, pad_off, pad_n, block_e, nb_used, nrows = _block_layout(counts, T, rb)
        dest = _dest(e_rank, poffsets.astype(F32).reshape(1, N_EXPERTS), tm=min(512, T))[:, :TOP_K]
        xs = _dispatch(pad_off, pad_n, nb_used, dest, u2p.reshape(T, D // 2), nrows, rb, tm=min(256, T))
        ys = _experts(block_e, nb_used, xs, moe_w1[l], moe_w3[l], moe_w2[l], rb)
        x = _combine(dest, ys, gw.reshape(B, S, LANES), u2p, x1, mod, sh_w1[l].astype(BF16),
                     sh_w3[l].astype(BF16), sh_w2[l].astype(BF16), ln2_g[l].reshape(1, D),
                     ln2_b[l].reshape(1, D), tm=min(256, S))
    return x
```

```python
import functools
import math

import jax
import jax.numpy as jnp
from jax import lax
from jax.experimental import pallas as pl
from jax.experimental.pallas import tpu as pltpu
from jax.experimental.pallas import tpu_sc as plsc

F32 = jnp.float32
BF16 = jnp.bfloat16
HIGHEST = lax.Precision.HIGHEST

SB_HEADS = 8
SB_HEAD_DIM = 64
ML_HEADS = 4
ML_HEAD_DIM = 128
CONV_K = 4
N_EXPERTS = 256
TOP_K = 8
N_GROUPS = 8
TOPK_GROUPS = 4
GROUP_SIZE = N_EXPERTS // N_GROUPS
ROUTED_SCALE = 2.5
EXPERT_BLOCK_ROWS = 256
SC_WINDOW = 128
DEPTH = 1
DN_ALPHA = (2 * DEPTH) ** 0.25
LN_EPS = 1e-5
LANES = 128
SUBLANES = 8
NEG_INF = float("-inf")
SB_CUTOFF = 104.0
VMEM_LIMIT = 56 * 1024 * 1024


def _cparams(sem):
    return pltpu.CompilerParams(dimension_semantics=sem, vmem_limit_bytes=VMEM_LIMIT)


def _ln(x):
    mu = jnp.mean(x, axis=-1, keepdims=True)
    xc = x - mu
    var = jnp.mean(xc * xc, axis=-1, keepdims=True)
    return xc * lax.rsqrt(var + LN_EPS)


def _dot(a, b):
    return jnp.dot(a, b, preferred_element_type=F32)


def _dot_nt(a, b):
    return lax.dot_general(a, b, (((1,), (1,)), ((), ())), preferred_element_type=F32)


def _dot_tn(a, b):
    return lax.dot_general(a, b, (((0,), (0,)), ((), ())), preferred_element_type=F32)


def _pack_halves(v):
    w = v.shape[1] // 2
    lo = lax.bitcast_convert_type(v[:, :w].astype(BF16).astype(F32), jnp.uint32) >> 16
    hi = lax.bitcast_convert_type(v[:, w:].astype(BF16).astype(F32), jnp.uint32) & jnp.uint32(0xFFFF0000)
    return hi | lo


def _unpack_halves(p):
    lo = lax.bitcast_convert_type(p << 16, F32).astype(BF16)
    hi = lax.bitcast_convert_type(p & jnp.uint32(0xFFFF0000), F32).astype(BF16)
    return lo, hi


def _store_token_rows(ref, v):
    p = _pack_halves(v)
    q = p.shape[1] // 2
    ref[0] = p[:, :q]
    ref[1] = p[:, q:]


def _load_token_rows(ref, valid=None):
    first, second = ref[0], ref[1]
    if valid is not None:
        row = lax.broadcasted_iota(jnp.int32, first.shape, 0)
        first = jnp.where(row < valid, first, jnp.uint32(0))
        second = jnp.where(row < valid, second, jnp.uint32(0))
    lo_a, hi_a = _unpack_halves(first)
    lo_b, hi_b = _unpack_halves(second)
    return [lo_a, lo_b, hi_a, hi_b]


def _dot_blocks(blocks, w_ref):
    q = blocks[0].shape[1]
    acc = _dot(blocks[0], w_ref[0:q, :])
    for i in range(1, len(blocks)):
        acc = acc + _dot(blocks[i], w_ref[i * q:(i + 1) * q, :])
    return acc


def _adaln_kernel(c_ref, w_ref, b_ref, o_ref):
    c = c_ref[...]
    s = c * jax.nn.sigmoid(c)
    o_ref[...] = jnp.dot(s, w_ref[...], preferred_element_type=F32, precision=HIGHEST) + b_ref[...]


def _adaln(c, w_ada, b_ada):
    B, D = c.shape
    N = w_ada.shape[1]
    tn = 1024
    return pl.pallas_call(
        _adaln_kernel,
        grid=(N // tn,),
        in_specs=[pl.BlockSpec((B, D), lambda j: (0, 0)),
                  pl.BlockSpec((D, tn), lambda j: (0, j)),
                  pl.BlockSpec((1, tn), lambda j: (0, j))],
        out_specs=pl.BlockSpec((B, tn), lambda j: (0, j)),
        out_shape=jax.ShapeDtypeStruct((B, N), F32),
        compiler_params=_cparams(("arbitrary",)),
        name="adaln",
    )(c, w_ada, b_ada.reshape(1, N))


def _inproj_kernel(x_ref, mod_ref, wsb_ref, wqk_ref, wvo_ref, wg_ref,
                   sb_ref, mqk_ref, mvo_ref, g_ref):
    y = _ln(x_ref[...])
    sh = mod_ref[0:1, :]
    sc = mod_ref[1:2, :]
    u = (y * (1.0 + sc) + sh).astype(BF16)
    sb_ref[...] = _dot(u, wsb_ref[...]).astype(BF16)
    mqk_ref[...] = _dot(u, wqk_ref[...])
    mvo_ref[...] = _dot(u, wvo_ref[...]).astype(BF16)
    g_ref[...] = _dot(u, wg_ref[...])


def _inproj(x, mod, w_sb, w_mqk, w_mvo, w_g, tm):
    B, S, D = x.shape
    nsb, nqk, nvo, ng = w_sb.shape[1], w_mqk.shape[1], w_mvo.shape[1], w_g.shape[1]
    row = lambda n: pl.BlockSpec((None, tm, n), lambda b, i: (b, i, 0))
    full = lambda n: pl.BlockSpec((D, n), lambda b, i: (0, 0))
    return pl.pallas_call(
        _inproj_kernel,
        grid=(B, S // tm),
        in_specs=[row(D), pl.BlockSpec((None, 6, D), lambda b, i: (b, 0, 0)),
                  full(nsb), full(nqk), full(nvo), full(ng)],
        out_specs=[row(nsb), row(nqk), row(nvo), row(ng)],
        out_shape=[jax.ShapeDtypeStruct((B, S, nsb), BF16),
                   jax.ShapeDtypeStruct((B, S, nqk), F32),
                   jax.ShapeDtypeStruct((B, S, nvo), BF16),
                   jax.ShapeDtypeStruct((B, S, ng), F32)],
        compiler_params=_cparams(("arbitrary", "arbitrary")),
        name="inproj",
    )(x, mod, w_sb, w_mqk, w_mvo, w_g)


def _sb_block(z):
    sp = jnp.log(1.0 + jnp.exp(-jnp.abs(z)))
    log_beta = jnp.minimum(z, 0.0) - sp
    log_1m = -jnp.maximum(z, 0.0) - sp
    return log_beta, log_1m


def _sb_kernel(q_ref, k_ref, v_ref, o_ref, *, tq, scale):
    qi = pl.program_id(2)
    q2 = q_ref[...]
    lane = lax.broadcasted_iota(jnp.int32, (1, LANES), 1)
    r = lax.broadcasted_iota(jnp.int32, (tq, tq), 0)
    c = lax.broadcasted_iota(jnp.int32, (tq, tq), 1)
    upper = (r > c).astype(BF16)
    strict = c < r
    out = jnp.zeros((tq, LANES), F32)
    for h in range(2):
        hmask = (lane // SB_HEAD_DIM) == h
        qh = jnp.where(hmask, q2, jnp.zeros_like(q2))

        def block(kb, carry, masked):
            off = pl.multiple_of(kb * tq, tq)
            kblk = k_ref[pl.ds(off, tq), :]
            vblk = v_ref[pl.ds(off, tq), :]
            z = _dot_nt(qh, kblk) * scale
            log_beta, log_1m = _sb_block(z)
            if masked:
                log_1m = jnp.where(strict, log_1m, 0.0)
            after = _dot(log_1m.astype(BF16), upper)
            a = jnp.exp(log_beta + after + carry)
            if masked:
                a = jnp.where(strict, a, 0.0)
            pv = _dot(a.astype(BF16), vblk)
            return pv, carry + jnp.sum(log_1m, axis=1, keepdims=True)

        acc, carry = block(qi, jnp.zeros((tq, 1), F32), True)

        def cond(st):
            i, _, _, top = st
            return jnp.logical_and(i < qi, top > -SB_CUTOFF)

        def body(st):
            i, acc, carry, _ = st
            pv, carry = block(qi - 1 - i, carry, False)
            return i + 1, acc + pv, carry, jnp.max(carry)

        _, acc, carry, _ = lax.while_loop(cond, body, (jnp.int32(0), acc, carry, jnp.max(carry)))
        out = jnp.where(hmask, acc, out)
    o_ref[...] = out.astype(o_ref.dtype)


def _sb_attention(sbp, tq):
    B, S, W3 = sbp.shape
    W = W3 // 3
    npair = W // LANES
    kern = functools.partial(_sb_kernel, tq=tq, scale=SB_HEAD_DIM ** -0.5)
    return pl.pallas_call(
        kern,
        grid=(B, npair, S // tq),
        in_specs=[pl.BlockSpec((None, tq, LANES), lambda b, p, i: (b, i, p)),
                  pl.BlockSpec((None, S, LANES), lambda b, p, i: (b, 0, npair + p)),
                  pl.BlockSpec((None, S, LANES), lambda b, p, i: (b, 0, 2 * npair + p))],
        out_specs=pl.BlockSpec((None, tq, LANES), lambda b, p, i: (b, i, p)),
        out_shape=jax.ShapeDtypeStruct((B, S, W), BF16),
        compiler_params=_cparams(("arbitrary", "arbitrary", "arbitrary")),
        name="sb_attention",
    )(sbp, sbp, sbp)


def _mlstm_kernel(qk_ref, vo_ref, g_ref, cw_ref, cb_ref, gb_ref, ng_ref, o_ref,
                  xbuf, ct_ref, m_ref, *, L):
    H, dk = ML_HEADS, ML_HEAD_DIM
    W = H * dk
    ci = pl.program_id(1)

    @pl.when(ci == 0)
    def _():
        xbuf[0:8, :] = jnp.zeros((8, 2 * W), F32)
        ct_ref[...] = jnp.zeros_like(ct_ref)
        m_ref[...] = jnp.zeros_like(m_ref)

    xbuf[8:8 + L, :] = qk_ref[...]
    y = cb_ref[...] + cw_ref[CONV_K - 1:CONV_K, :] * xbuf[8:8 + L, :]
    for j in range(1, CONV_K):
        y = y + cw_ref[CONV_K - 1 - j:CONV_K - j, :] * xbuf[8 - j:8 - j + L, :]
    xbuf[0:8, :] = xbuf[L:L + 8, :]
    qk = y * jax.nn.sigmoid(y)

    g = g_ref[...] + gb_ref[...]
    logf = jax.nn.log_sigmoid(g)
    r = lax.broadcasted_iota(jnp.int32, (L, L), 0)
    c = lax.broadcasted_iota(jnp.int32, (L, L), 1)
    causal = c <= r
    tri = causal.astype(BF16)
    lf_hi = logf.astype(BF16)
    lf_lo = (logf - lf_hi.astype(F32)).astype(BF16)
    bc = _dot(tri, lf_hi) + _dot(tri, lf_lo)
    g_t = g.T
    bc_t = bc.T
    e0 = (lax.broadcasted_iota(jnp.int32, (L, LANES), 1) == 0).astype(F32)

    for h in range(H):
        qh = qk[:, h * dk:(h + 1) * dk].astype(BF16)
        kh = (qk[:, W + h * dk:W + (h + 1) * dk] * (dk ** -0.5)).astype(BF16)
        vh = vo_ref[:, h * dk:(h + 1) * dk].astype(F32)
        oh = vo_ref[:, W + h * dk:W + (h + 1) * dk].astype(F32)
        vaug = jnp.concatenate([vh, e0], axis=1)
        ic_col = g[:, h:h + 1]
        ic_row = g_t[h:h + 1, :]
        bc_col = bc[:, H + h:H + h + 1]
        bc_row = bc_t[H + h:H + h + 1, :]
        m = m_ref[h][:, 0:1]
        ct = ct_ref[h]

        log_d = jnp.where(causal, bc_col - bc_row + ic_row, NEG_INF)
        inter = bc_col + m
        m_t = jnp.maximum(inter, jnp.max(log_d, axis=1, keepdims=True))
        w = _dot_nt(qh, kh) * jnp.exp(log_d - m_t)
        s_inter = jnp.exp(inter - m_t)
        tot = s_inter * _dot(qh, ct.astype(BF16)) + _dot(w.astype(BF16), vaug.astype(BF16))
        num = tot[:, :dk]
        den = tot[:, dk:dk + 1]
        hh = num / jnp.maximum(jnp.abs(den), jnp.exp(-m_t))

        b_last = bc_col[L - 1:L, :]
        log_w = b_last - bc_col + ic_col
        m_new = jnp.maximum(b_last + m, jnp.max(log_w, axis=0, keepdims=True))
        wk = jnp.exp(log_w - m_new)
        decay = jnp.exp(b_last + m - m_new)
        ct_ref[h] = decay * ct + _dot_tn(kh, (wk * vaug).astype(BF16))
        m_ref[h] = jnp.broadcast_to(m_new, (1, LANES))

        hn = _ln(hh) * ng_ref[:, h * dk:(h + 1) * dk]
        o_ref[:, h * dk:(h + 1) * dk] = (jax.nn.sigmoid(oh) * hn).astype(o_ref.dtype)


def _mlstm(mqk, mvo, gates, conv_w, conv_b, gate_b, norm_g, L):
    B, S, W2 = mqk.shape
    W = W2 // 2
    kern = functools.partial(_mlstm_kernel, L=L)
    row = lambda n: pl.BlockSpec((None, L, n), lambda b, i: (b, i, 0))
    cst = lambda a: pl.BlockSpec(a.shape, lambda b, i: (0, 0))
    return pl.pallas_call(
        kern,
        grid=(B, S // L),
        in_specs=[row(W2), row(W2), row(LANES), cst(conv_w), cst(conv_b), cst(gate_b), cst(norm_g)],
        out_specs=row(W),
        out_shape=jax.ShapeDtypeStruct((B, S, W), BF16),
        scratch_shapes=[pltpu.VMEM((L + 8, W2), F32),
                        pltpu.VMEM((ML_HEADS, ML_HEAD_DIM, 2 * ML_HEAD_DIM), F32),
                        pltpu.VMEM((ML_HEADS, 1, LANES), F32)],
        compiler_params=_cparams(("arbitrary", "arbitrary")),
        name="mlstm",
    )(mqk, mvo, gates, conv_w, conv_b, gate_b, norm_g)


def _outproj_kernel(sb_ref, ml_ref, x_ref, mod_ref, wo1_ref, wo2_ref, lg_ref, lb_ref, wr_ref,
                    x1_ref, u2_ref, lo_ref):
    mix = _dot(sb_ref[...], wo1_ref[...]) + _dot(ml_ref[...], wo2_ref[...])
    g1 = mod_ref[2:3, :]
    x1 = _ln(DN_ALPHA * x_ref[...] + g1 * mix) * lg_ref[...] + lb_ref[...]
    x1_ref[...] = x1
    u2 = _ln(x1) * (1.0 + mod_ref[4:5, :]) + mod_ref[3:4, :]
    _store_token_rows(u2_ref, u2)
    lo_ref[...] = jnp.dot(u2, wr_ref[...], preferred_element_type=F32, precision=HIGHEST)


def _outproj(sb, ml, x, mod, wo1, wo2, ln_g, ln_b, w_router, tm):
    B, S, D = x.shape
    W = sb.shape[2]
    E = w_router.shape[1]
    row = lambda n: pl.BlockSpec((None, tm, n), lambda b, i: (b, i, 0))
    cst = lambda a: pl.BlockSpec(a.shape, lambda b, i: (0, 0))
    return pl.pallas_call(
        _outproj_kernel,
        grid=(B, S // tm),
        in_specs=[row(W), row(W), row(D), pl.BlockSpec((None, 6, D), lambda b, i: (b, 0, 0)),
                  cst(wo1), cst(wo2), cst(ln_g), cst(ln_b), cst(w_router)],
        out_specs=[row(D), pl.BlockSpec((2, None, tm, D // 4), lambda b, i: (0, b, i, 0)), row(E)],
        out_shape=[jax.ShapeDtypeStruct((B, S, D), F32),
                   jax.ShapeDtypeStruct((2, B, S, D // 4), jnp.uint32),
                   jax.ShapeDtypeStruct((B, S, E), F32)],
        compiler_params=_cparams(("arbitrary", "arbitrary")),
        name="outproj",
    )(sb, ml, x, mod, wo1, wo2, ln_g, ln_b, w_router)


def _group_allreduce(x, lane, op):
    n = x.shape[1]
    sh = 1
    while sh < GROUP_SIZE:
        down = pltpu.roll(x, sh, axis=1)
        up = pltpu.roll(x, n - sh, axis=1)
        x = op(x, jnp.where((lane & sh) != 0, down, up))
        sh *= 2
    return x


def _route_kernel(lo_ref, rb_ref, e_ref, g_ref, cnt_ref, cnt_scr):
    tm, E = lo_ref.shape

    @pl.when(pl.program_id(0) == 0)
    def _():
        cnt_scr[...] = jnp.zeros_like(cnt_scr)

    scores = jax.nn.sigmoid(lo_ref[...])
    sel = scores + rb_ref[...]
    lane = lax.broadcasted_iota(jnp.int32, (tm, E), 1)
    lane_f = lane.astype(F32)
    m1 = _group_allreduce(sel, lane, jnp.maximum)
    is_max = sel == m1
    cnt = _group_allreduce(is_max.astype(F32), lane, jnp.add)
    m2 = _group_allreduce(jnp.where(is_max, NEG_INF, sel), lane, jnp.maximum)
    gscore = m1 + jnp.where(cnt >= 2.0, m1, m2)
    grp = lane // GROUP_SIZE
    rank = jnp.zeros((tm, E), jnp.int32)
    for s in range(1, N_GROUPS):
        other = pltpu.roll(gscore, s * GROUP_SIZE, axis=1)
        beats = (other > gscore) | ((other == gscore) & (grp >= s))
        rank = rank + beats.astype(jnp.int32)
    cur = jnp.where(rank < TOPK_GROUPS, sel, NEG_INF)
    idxs = []
    chosen = jnp.zeros((tm, E), F32)
    for k in range(TOP_K):
        m = jnp.max(cur, axis=1, keepdims=True)
        idx = jnp.min(jnp.where(cur == m, lane_f, float(E)), axis=1, keepdims=True)
        pick = lane_f == idx
        cur = jnp.where(pick, NEG_INF, cur)
        chosen = jnp.where(pick, 1.0, chosen)
        idxs.append(idx)
    r = lax.broadcasted_iota(jnp.int32, (tm, tm), 0)
    c = lax.broadcasted_iota(jnp.int32, (tm, tm), 1)
    before = _dot((c < r).astype(BF16), chosen.astype(BF16)) + cnt_scr[...]
    col = lax.broadcasted_iota(jnp.int32, (tm, LANES), 1)
    e_out = jnp.zeros((tm, LANES), jnp.int32)
    g_out = jnp.zeros((tm, LANES), F32)
    gsum = jnp.zeros((tm, 1), F32)
    for k in range(TOP_K):
        pick = lane_f == idxs[k]
        gk = jnp.sum(jnp.where(pick, scores, 0.0), axis=1, keepdims=True)
        rk = jnp.sum(jnp.where(pick, before, 0.0), axis=1, keepdims=True)
        e_out = jnp.where(col == k, idxs[k].astype(jnp.int32), e_out)
        e_out = jnp.where(col == TOP_K + k, rk.astype(jnp.int32), e_out)
        g_out = jnp.where(col == k, gk, g_out)
        gsum = gsum + gk
    e_ref[...] = e_out
    g_ref[...] = g_out / gsum * ROUTED_SCALE
    cnt_scr[...] = cnt_scr[...] + jnp.sum(chosen, axis=0, keepdims=True)
    cnt_ref[...] = cnt_scr[...]


def _route(logits, router_bias, tm):
    T, E = logits.shape
    return pl.pallas_call(
        _route_kernel,
        grid=(T // tm,),
        in_specs=[pl.BlockSpec((tm, E), lambda i: (i, 0)), pl.BlockSpec((1, E), lambda i: (0, 0))],
        out_specs=[pl.BlockSpec((tm, LANES), lambda i: (i, 0)), pl.BlockSpec((tm, LANES), lambda i: (i, 0)),
                   pl.BlockSpec((1, E), lambda i: (0, 0))],
        out_shape=[jax.ShapeDtypeStruct((T, LANES), jnp.int32), jax.ShapeDtypeStruct((T, LANES), F32),
                   jax.ShapeDtypeStruct((1, E), F32)],
        scratch_shapes=[pltpu.VMEM((1, E), F32)],
        compiler_params=_cparams(("arbitrary",)),
        name="route",
    )(logits, router_bias.reshape(1, E))


def _dest_kernel(e_ref, off_ref, d_ref):
    tm = e_ref.shape[0]
    E = off_ref.shape[1]
    ev = e_ref[...]
    lane = lax.broadcasted_iota(jnp.int32, (tm, E), 1)
    col = lax.broadcasted_iota(jnp.int32, (tm, LANES), 1)
    off = off_ref[...]
    out = jnp.zeros((tm, LANES), jnp.int32)
    for k in range(TOP_K):
        base = jnp.sum(jnp.where(lane == ev[:, k:k + 1], off, 0.0), axis=1, keepdims=True)
        out = jnp.where(col == k, base.astype(jnp.int32) + ev[:, TOP_K + k:TOP_K + k + 1], out)
    d_ref[...] = out


def _dest(e_rank, offsets, tm):
    T = e_rank.shape[0]
    E = offsets.shape[1]
    return pl.pallas_call(
        _dest_kernel,
        grid=(T // tm,),
        in_specs=[pl.BlockSpec((tm, LANES), lambda i: (i, 0)), pl.BlockSpec((1, E), lambda i: (0, 0))],
        out_specs=pl.BlockSpec((tm, LANES), lambda i: (i, 0)),
        out_shape=jax.ShapeDtypeStruct((T, LANES), jnp.int32),
        compiler_params=_cparams(("arbitrary",)),
        name="dest",
    )(e_rank, offsets)


def _sc_mesh():
    return plsc.VectorSubcoreMesh(core_axis_name="core", subcore_axis_name="subcore")


def _sc_dispatch(u2p, dest_t, nrows):
    T, W = u2p.shape
    K = dest_t.shape[0]
    win = SC_WINDOW

    @pl.kernel(out_type=jax.ShapeDtypeStruct((nrows, W), u2p.dtype), mesh=_sc_mesh(), scratch_types=[])
    def scatter_rows(x_hbm, i_hbm, o_hbm):
        def body(x_vmem, i_vmem):
            for k in range(K):
                pltpu.sync_copy(x_vmem, o_hbm.at[i_vmem.at[k]])

        pltpu.emit_pipeline(
            body,
            grid=(T // win,),
            in_specs=[pl.BlockSpec((win, W), lambda i: (i, 0)),
                      pl.BlockSpec((K, win), lambda i: (0, i))],
            out_specs=[],
            core_axis_name=("core", "subcore"),
            dimension_semantics=(pltpu.PARALLEL,),
        )(x_hbm, i_hbm)

    return scatter_rows(u2p, dest_t)


def _sc_gather(ys, dest_flat):
    W = ys.shape[1]
    n = dest_flat.shape[1]
    win = SC_WINDOW

    @pl.kernel(out_type=jax.ShapeDtypeStruct((n, W), ys.dtype), mesh=_sc_mesh(), scratch_types=[])
    def gather_rows(y_hbm, i_hbm, o_hbm):
        def body(i_vmem, o_vmem):
            pltpu.sync_copy(y_hbm.at[i_vmem.at[0]], o_vmem)

        pltpu.emit_pipeline(
            body,
            grid=(n // win,),
            in_specs=[pl.BlockSpec((1, win), lambda i: (0, i))],
            out_specs=[pl.BlockSpec((win, W), lambda i: (i, 0))],
            core_axis_name=("core", "subcore"),
            dimension_semantics=(pltpu.PARALLEL,),
        )(i_hbm, o_hbm)

    return gather_rows(ys, dest_flat)


def _expert_kernel(be_ref, bv_ref, nb_ref, xs_ref, w1_ref, w3_ref, w2_ref, y_ref, w1b, w3b, w2b):
    i = pl.program_id(0)
    changed = jnp.logical_or(i == 0, be_ref[i] != be_ref[jnp.maximum(i - 1, 0)])

    @pl.when(jnp.logical_and(i < nb_ref[0], changed))
    def _():
        w1b[...] = w1_ref[0].astype(BF16)
        w3b[...] = w3_ref[0].astype(BF16)
        w2b[...] = w2_ref[0].astype(BF16)

    @pl.when(i < nb_ref[0])
    def _():
        xb = _load_token_rows(xs_ref, valid=bv_ref[i])
        a = _dot_blocks(xb, w1b)
        b = _dot_blocks(xb, w3b)
        hmid = (a * jax.nn.sigmoid(a) * b).astype(BF16)
        _store_token_rows(y_ref, _dot(hmid, w2b[...]))

    @pl.when(i >= nb_ref[0])
    def _():
        y_ref[...] = jnp.zeros_like(y_ref)


def _experts(block_e, block_valid, nb_used, xs, w1, w3, w2, rb):
    _, nrows, W = xs.shape
    nb = nrows // rb
    E, D, FF = w1.shape
    grid_spec = pltpu.PrefetchScalarGridSpec(
        num_scalar_prefetch=3,
        grid=(nb,),
        in_specs=[pl.BlockSpec((2, rb, W), lambda i, be, bv, n: (0, i, 0)),
                  pl.BlockSpec((1, D, FF), lambda i, be, bv, n: (be[i], 0, 0)),
                  pl.BlockSpec((1, D, FF), lambda i, be, bv, n: (be[i], 0, 0)),
                  pl.BlockSpec((1, FF, D), lambda i, be, bv, n: (be[i], 0, 0))],
        out_specs=pl.BlockSpec((2, rb, W), lambda i, be, bv, n: (0, i, 0)),
        scratch_shapes=[pltpu.VMEM((D, FF), BF16), pltpu.VMEM((D, FF), BF16), pltpu.VMEM((FF, D), BF16)],
    )
    return pl.pallas_call(
        _expert_kernel,
        grid_spec=grid_spec,
        out_shape=jax.ShapeDtypeStruct((2, nrows, W), jnp.uint32),
        compiler_params=_cparams(("arbitrary",)),
        name="experts",
    )(block_e, block_valid, nb_used, xs, w1, w3, w2)


def _combine_kernel(yg_ref, gw_ref, u_ref, x1_ref, mod_ref, s1_ref, s3_ref, s2_ref, lg_ref, lb_ref, o_ref):
    ub = _load_token_rows(u_ref)
    a = _dot_blocks(ub, s1_ref)
    b = _dot_blocks(ub, s3_ref)
    ffn = _dot((a * jax.nn.sigmoid(a) * b).astype(BF16), s2_ref[...])
    gw = gw_ref[...]
    routed = None
    for k in range(TOP_K):
        yk = [gw[:, k:k + 1] * blk.astype(F32) for blk in _load_token_rows(yg_ref.at[k])]
        routed = yk if routed is None else [r + y for r, y in zip(routed, yk)]
    ffn = ffn + jnp.concatenate(routed, axis=1)
    g2 = mod_ref[5:6, :]
    o_ref[...] = _ln(DN_ALPHA * x1_ref[...] + g2 * ffn) * lg_ref[...] + lb_ref[...]


def _combine(yg, gw, u2p, x1, mod, s1, s3, s2, ln_g, ln_b, tm):
    B, S, D = x1.shape
    K = yg.shape[0]
    row = lambda n: pl.BlockSpec((None, tm, n), lambda b, i: (b, i, 0))
    cst = lambda a: pl.BlockSpec(a.shape, lambda b, i: (0, 0))
    return pl.pallas_call(
        _combine_kernel,
        grid=(B, S // tm),
        in_specs=[pl.BlockSpec((K, 2, None, tm, D // 4), lambda b, i: (0, 0, b, i, 0)),
                  row(LANES), pl.BlockSpec((2, None, tm, D // 4), lambda b, i: (0, b, i, 0)), row(D),
                  pl.BlockSpec((None, 6, D), lambda b, i: (b, 0, 0)),
                  cst(s1), cst(s3), cst(s2), cst(ln_g), cst(ln_b)],
        out_specs=row(D),
        out_shape=jax.ShapeDtypeStruct((B, S, D), F32),
        compiler_params=_cparams(("arbitrary", "arbitrary")),
        name="combine",
    )(yg, gw, u2p, x1, mod, s1, s3, s2, ln_g, ln_b)


def _block_layout(counts, T, rb):
    counts = counts.reshape(-1).astype(jnp.int32)
    pcounts = (counts + rb - 1) // rb * rb
    pend = jnp.cumsum(pcounts)
    poffsets = pend - pcounts
    nb = (T * TOP_K) // rb + N_EXPERTS
    block_e = jnp.minimum(jnp.searchsorted(pend, jnp.arange(nb, dtype=jnp.int32) * rb, side='right'),
                          N_EXPERTS - 1).astype(jnp.int32)
    nb_used = (pend[-1] // rb).astype(jnp.int32).reshape(1)
    first_row = jnp.arange(nb, dtype=jnp.int32) * rb - poffsets[block_e]
    block_valid = jnp.clip(counts[block_e] - first_row, 0, rb).astype(jnp.int32)
    return poffsets, block_e, block_valid, nb_used, nb * rb


def kernel(x, c, w_ada, b_ada, w_in, ml_conv_w, ml_conv_b, ml_gate_b, ml_norm_g, w_out, ln1_g, ln1_b,
           w_router, router_bias, moe_w1, moe_w3, moe_w2, sh_w1, sh_w3, sh_w2, ln2_g, ln2_b):
    B, S, D = x.shape
    T = B * S
    SBW = SB_HEADS * SB_HEAD_DIM
    MLW = ML_HEADS * ML_HEAD_DIM
    rb = EXPERT_BLOCK_ROWS
    for l in range(DEPTH):
        mod = _adaln(c, w_ada[l], b_ada[l]).reshape(B, 6, D)

        wi = w_in[l]
        c0 = 3 * SBW
        w_sb = wi[:, :c0].astype(BF16)
        w_mqk = wi[:, c0:c0 + 2 * MLW].astype(BF16)
        w_mvo = wi[:, c0 + 2 * MLW:c0 + 4 * MLW].astype(BF16)
        w_g = jnp.pad(wi[:, c0 + 4 * MLW:], ((0, 0), (0, LANES - 2 * ML_HEADS))).astype(BF16)
        sbp, mqk, mvo, gates = _inproj(x, mod, w_sb, w_mqk, w_mvo, w_g, tm=min(512, S))

        sb = _sb_attention(sbp, tq=min(256, S))

        gate_b = jnp.pad(ml_gate_b[l], (0, LANES - 2 * ML_HEADS)).reshape(1, LANES)
        ml = _mlstm(mqk, mvo, gates, ml_conv_w[l], ml_conv_b[l].reshape(1, -1), gate_b,
                    ml_norm_g[l].reshape(1, -1), L=min(256, S))

        wo = w_out[l].astype(BF16)
        x1, u2p, logits = _outproj(sb, ml, x, mod, wo[:SBW], wo[SBW:], ln1_g[l].reshape(1, D),
                                   ln1_b[l].reshape(1, D), w_router[l], tm=min(512, S))

        e_rank, gw, counts = _route(logits.reshape(T, N_EXPERTS), router_bias[l], tm=min(512, T))
        poffsets, block_e, block_valid, nb_used, nrows = _block_layout(counts, T, rb)
        dest = _dest(e_rank, poffsets.astype(F32).reshape(1, N_EXPERTS), tm=min(512, T))
        dest_t = dest[:, :TOP_K].T
        idx2 = jnp.concatenate([dest_t, dest_t + nrows], axis=1)
        xs = _sc_dispatch(u2p.reshape(2 * T, D // 4), idx2, 2 * nrows).reshape(2, nrows, D // 4)
        ys = _experts(block_e, block_valid, nb_used, xs, moe_w1[l], moe_w3[l], moe_w2[l], rb)
        yg = _sc_gather(ys.reshape(2 * nrows, D // 4), idx2.reshape(1, TOP_K * 2 * T))
        yg = yg.reshape(TOP_K, 2, B, S, D // 4)
        x = _combine(yg, gw.reshape(B, S, LANES), u2p, x1, mod, sh_w1[l].astype(BF16),
                     sh_w3[l].astype(BF16), sh_w2[l].astype(BF16), ln2_g[l].reshape(1, D),
                     ln2_b[l].reshape(1, D), tm=min(256, S))
    return x
```

```python
import functools
import math

import jax
import jax.numpy as jnp
from jax import lax
from jax.experimental import pallas as pl
from jax.experimental.pallas import tpu as pltpu
from jax.experimental.pallas import tpu_sc as plsc

F32 = jnp.float32
BF16 = jnp.bfloat16
HIGHEST = lax.Precision.HIGHEST

SB_HEADS = 8
SB_HEAD_DIM = 64
ML_HEADS = 4
ML_HEAD_DIM = 128
CONV_K = 4
N_EXPERTS = 256
TOP_K = 8
N_GROUPS = 8
TOPK_GROUPS = 4
GROUP_SIZE = N_EXPERTS // N_GROUPS
ROUTED_SCALE = 2.5
EXPERT_BLOCK_ROWS = 512
SC_WINDOW = 128
DEPTH = 1
DN_ALPHA = (2 * DEPTH) ** 0.25
LN_EPS = 1e-5
LANES = 128
SUBLANES = 8
NEG_INF = float("-inf")
SB_CUTOFF = 104.0
VMEM_LIMIT = 56 * 1024 * 1024


def _cparams(sem):
    return pltpu.CompilerParams(dimension_semantics=sem, vmem_limit_bytes=VMEM_LIMIT)


def _ln(x):
    mu = jnp.mean(x, axis=-1, keepdims=True)
    xc = x - mu
    var = jnp.mean(xc * xc, axis=-1, keepdims=True)
    return xc * lax.rsqrt(var + LN_EPS)


def _dot(a, b):
    return jnp.dot(a, b, preferred_element_type=F32)


def _dot_nt(a, b):
    return lax.dot_general(a, b, (((1,), (1,)), ((), ())), preferred_element_type=F32)


def _dot_tn(a, b):
    return lax.dot_general(a, b, (((0,), (0,)), ((), ())), preferred_element_type=F32)


def _pack_halves(v):
    w = v.shape[1] // 2
    lo = lax.bitcast_convert_type(v[:, :w].astype(BF16).astype(F32), jnp.uint32) >> 16
    hi = lax.bitcast_convert_type(v[:, w:].astype(BF16).astype(F32), jnp.uint32) & jnp.uint32(0xFFFF0000)
    return hi | lo


def _unpack_halves(p):
    lo = lax.bitcast_convert_type(p << 16, F32).astype(BF16)
    hi = lax.bitcast_convert_type(p & jnp.uint32(0xFFFF0000), F32).astype(BF16)
    return lo, hi


def _store_token_rows(ref, v):
    p = _pack_halves(v)
    q = p.shape[1] // 2
    ref[0] = p[:, :q]
    ref[1] = p[:, q:]


def _load_token_rows(ref, valid=None):
    first, second = ref[0], ref[1]
    if valid is not None:
        row = lax.broadcasted_iota(jnp.int32, first.shape, 0)
        first = jnp.where(row < valid, first, jnp.uint32(0))
        second = jnp.where(row < valid, second, jnp.uint32(0))
    lo_a, hi_a = _unpack_halves(first)
    lo_b, hi_b = _unpack_halves(second)
    return [lo_a, lo_b, hi_a, hi_b]


def _dot_blocks(blocks, w_ref):
    q = blocks[0].shape[1]
    acc = _dot(blocks[0], w_ref[0:q, :])
    for i in range(1, len(blocks)):
        acc = acc + _dot(blocks[i], w_ref[i * q:(i + 1) * q, :])
    return acc


def _adaln_kernel(c_ref, w_ref, b_ref, o_ref):
    c = c_ref[...]
    s = c * jax.nn.sigmoid(c)
    o_ref[...] = jnp.dot(s, w_ref[...], preferred_element_type=F32, precision=HIGHEST) + b_ref[...]


def _adaln(c, w_ada, b_ada):
    B, D = c.shape
    N = w_ada.shape[1]
    tn = 1024
    return pl.pallas_call(
        _adaln_kernel,
        grid=(N // tn,),
        in_specs=[pl.BlockSpec((B, D), lambda j: (0, 0)),
                  pl.BlockSpec((D, tn), lambda j: (0, j)),
                  pl.BlockSpec((1, tn), lambda j: (0, j))],
        out_specs=pl.BlockSpec((B, tn), lambda j: (0, j)),
        out_shape=jax.ShapeDtypeStruct((B, N), F32),
        compiler_params=_cparams(("arbitrary",)),
        name="adaln",
    )(c, w_ada, b_ada.reshape(1, N))


def _inproj_kernel(x_ref, mod_ref, wsb_ref, wqk_ref, wvo_ref, wg_ref,
                   sb_ref, mqk_ref, mvo_ref, g_ref):
    y = _ln(x_ref[...])
    sh = mod_ref[0:1, :]
    sc = mod_ref[1:2, :]
    u = (y * (1.0 + sc) + sh).astype(BF16)
    sb_ref[...] = _dot(u, wsb_ref[...]).astype(BF16)
    mqk_ref[...] = _dot(u, wqk_ref[...])
    mvo_ref[...] = _dot(u, wvo_ref[...]).astype(BF16)
    g_ref[...] = _dot(u, wg_ref[...])


def _inproj(x, mod, w_sb, w_mqk, w_mvo, w_g, tm):
    B, S, D = x.shape
    nsb, nqk, nvo, ng = w_sb.shape[1], w_mqk.shape[1], w_mvo.shape[1], w_g.shape[1]
    row = lambda n: pl.BlockSpec((None, tm, n), lambda b, i: (b, i, 0))
    full = lambda n: pl.BlockSpec((D, n), lambda b, i: (0, 0))
    return pl.pallas_call(
        _inproj_kernel,
        grid=(B, S // tm),
        in_specs=[row(D), pl.BlockSpec((None, 6, D), lambda b, i: (b, 0, 0)),
                  full(nsb), full(nqk), full(nvo), full(ng)],
        out_specs=[row(nsb), row(nqk), row(nvo), row(ng)],
        out_shape=[jax.ShapeDtypeStruct((B, S, nsb), BF16),
                   jax.ShapeDtypeStruct((B, S, nqk), F32),
                   jax.ShapeDtypeStruct((B, S, nvo), BF16),
                   jax.ShapeDtypeStruct((B, S, ng), F32)],
        compiler_params=_cparams(("arbitrary", "arbitrary")),
        name="inproj",
    )(x, mod, w_sb, w_mqk, w_mvo, w_g)


def _sb_block(z):
    sp = jnp.log(1.0 + jnp.exp(-jnp.abs(z)))
    log_beta = jnp.minimum(z, 0.0) - sp
    log_1m = -jnp.maximum(z, 0.0) - sp
    return log_beta, log_1m


def _sb_kernel(q_ref, k_ref, v_ref, o_ref, *, tq, scale):
    qi = pl.program_id(2)
    q2 = q_ref[...]
    lane = lax.broadcasted_iota(jnp.int32, (1, LANES), 1)
    r = lax.broadcasted_iota(jnp.int32, (tq, tq), 0)
    c = lax.broadcasted_iota(jnp.int32, (tq, tq), 1)
    upper = (r > c).astype(BF16)
    strict = c < r
    hmasks = [(lane // SB_HEAD_DIM) == h for h in range(2)]
    qs = [jnp.where(m, q2 * scale, jnp.zeros_like(q2)) for m in hmasks]

    def block(qh, kblk, vblk, carry, masked):
        z = _dot_nt(qh, kblk)
        log_beta, log_1m = _sb_block(z)
        if masked:
            log_1m = jnp.where(strict, log_1m, 0.0)
        after = _dot(log_1m.astype(BF16), upper)
        a = jnp.exp(log_beta + after + carry)
        if masked:
            a = jnp.where(strict, a, 0.0)
        pv = _dot(a.astype(BF16), vblk)
        return pv, carry + jnp.sum(log_1m, axis=1, keepdims=True)

    def both_heads(kb, accs, carries, masked):
        off = pl.multiple_of(kb * tq, tq)
        kblk = k_ref[pl.ds(off, tq), :]
        vblk = v_ref[pl.ds(off, tq), :]
        res = [block(qs[h], kblk, vblk, carries[h], masked) for h in range(2)]
        return [accs[h] + res[h][0] for h in range(2)], [res[h][1] for h in range(2)]

    zero = jnp.zeros((tq, LANES), F32)
    accs, carries = both_heads(qi, [zero, zero], [jnp.zeros((tq, 1), F32)] * 2, True)

    def top_of(carries):
        return jnp.max(jnp.maximum(carries[0], carries[1]))

    def cond(st):
        i, _, _, top = st
        return jnp.logical_and(i < qi, top > -SB_CUTOFF)

    def body(st):
        i, accs, carries, _ = st
        accs, carries = both_heads(qi - 1 - i, accs, carries, False)
        return i + 1, accs, carries, top_of(carries)

    _, accs, _, _ = lax.while_loop(cond, body, (jnp.int32(0), accs, carries, top_of(carries)))
    o_ref[...] = jnp.where(hmasks[0], accs[0], accs[1]).astype(o_ref.dtype)


def _sb_attention(sbp, tq):
    B, S, W3 = sbp.shape
    W = W3 // 3
    npair = W // LANES
    kern = functools.partial(_sb_kernel, tq=tq, scale=SB_HEAD_DIM ** -0.5)
    return pl.pallas_call(
        kern,
        grid=(B, npair, S // tq),
        in_specs=[pl.BlockSpec((None, tq, LANES), lambda b, p, i: (b, i, p)),
                  pl.BlockSpec((None, S, LANES), lambda b, p, i: (b, 0, npair + p)),
                  pl.BlockSpec((None, S, LANES), lambda b, p, i: (b, 0, 2 * npair + p))],
        out_specs=pl.BlockSpec((None, tq, LANES), lambda b, p, i: (b, i, p)),
        out_shape=jax.ShapeDtypeStruct((B, S, W), BF16),
        compiler_params=_cparams(("arbitrary", "arbitrary", "arbitrary")),
        name="sb_attention",
    )(sbp, sbp, sbp)


def _mlstm_kernel(qk_ref, vo_ref, g_ref, cw_ref, cb_ref, gb_ref, ng_ref, o_ref,
                  xbuf, ct_ref, m_ref, *, L):
    H, dk = ML_HEADS, ML_HEAD_DIM
    W = H * dk
    ci = pl.program_id(1)

    @pl.when(ci == 0)
    def _():
        xbuf[0:8, :] = jnp.zeros((8, 2 * W), F32)
        ct_ref[...] = jnp.zeros_like(ct_ref)
        m_ref[...] = jnp.zeros_like(m_ref)

    xbuf[8:8 + L, :] = qk_ref[...]
    y = cb_ref[...] + cw_ref[CONV_K - 1:CONV_K, :] * xbuf[8:8 + L, :]
    for j in range(1, CONV_K):
        y = y + cw_ref[CONV_K - 1 - j:CONV_K - j, :] * xbuf[8 - j:8 - j + L, :]
    xbuf[0:8, :] = xbuf[L:L + 8, :]
    qk = y * jax.nn.sigmoid(y)

    g = g_ref[...] + gb_ref[...]
    logf = jax.nn.log_sigmoid(g)
    r = lax.broadcasted_iota(jnp.int32, (L, L), 0)
    c = lax.broadcasted_iota(jnp.int32, (L, L), 1)
    causal = c <= r
    tri = causal.astype(BF16)
    lf_hi = logf.astype(BF16)
    lf_lo = (logf - lf_hi.astype(F32)).astype(BF16)
    bc = _dot(tri, lf_hi) + _dot(tri, lf_lo)
    g_t = g.T
    bc_t = bc.T
    e0 = (lax.broadcasted_iota(jnp.int32, (L, LANES), 1) == 0).astype(F32)

    for h in range(H):
        qh = qk[:, h * dk:(h + 1) * dk].astype(BF16)
        kh = (qk[:, W + h * dk:W + (h + 1) * dk] * (dk ** -0.5)).astype(BF16)
        vh = vo_ref[:, h * dk:(h + 1) * dk].astype(F32)
        oh = vo_ref[:, W + h * dk:W + (h + 1) * dk].astype(F32)
        vaug = jnp.concatenate([vh, e0], axis=1)
        ic_col = g[:, h:h + 1]
        ic_row = g_t[h:h + 1, :]
        bc_col = bc[:, H + h:H + h + 1]
        bc_row = bc_t[H + h:H + h + 1, :]
        m = m_ref[h][:, 0:1]
        ct = ct_ref[h]

        log_d = jnp.where(causal, bc_col - bc_row + ic_row, NEG_INF)
        inter = bc_col + m
        m_t = jnp.maximum(inter, jnp.max(log_d, axis=1, keepdims=True))
        w = _dot_nt(qh, kh) * jnp.exp(log_d - m_t)
        s_inter = jnp.exp(inter - m_t)
        tot = s_inter * _dot(qh, ct.astype(BF16)) + _dot(w.astype(BF16), vaug.astype(BF16))
        num = tot[:, :dk]
        den = tot[:, dk:dk + 1]
        hh = num / jnp.maximum(jnp.abs(den), jnp.exp(-m_t))

        b_last = bc_col[L - 1:L, :]
        log_w = b_last - bc_col + ic_col
        m_new = jnp.maximum(b_last + m, jnp.max(log_w, axis=0, keepdims=True))
        wk = jnp.exp(log_w - m_new)
        decay = jnp.exp(b_last + m - m_new)
        ct_ref[h] = decay * ct + _dot_tn(kh, (wk * vaug).astype(BF16))
        m_ref[h] = jnp.broadcast_to(m_new, (1, LANES))

        hn = _ln(hh) * ng_ref[:, h * dk:(h + 1) * dk]
        o_ref[:, h * dk:(h + 1) * dk] = (jax.nn.sigmoid(oh) * hn).astype(o_ref.dtype)


def _mlstm(mqk, mvo, gates, conv_w, conv_b, gate_b, norm_g, L):
    B, S, W2 = mqk.shape
    W = W2 // 2
    kern = functools.partial(_mlstm_kernel, L=L)
    row = lambda n: pl.BlockSpec((None, L, n), lambda b, i: (b, i, 0))
    cst = lambda a: pl.BlockSpec(a.shape, lambda b, i: (0, 0))
    return pl.pallas_call(
        kern,
        grid=(B, S // L),
        in_specs=[row(W2), row(W2), row(LANES), cst(conv_w), cst(conv_b), cst(gate_b), cst(norm_g)],
        out_specs=row(W),
        out_shape=jax.ShapeDtypeStruct((B, S, W), BF16),
        scratch_shapes=[pltpu.VMEM((L + 8, W2), F32),
                        pltpu.VMEM((ML_HEADS, ML_HEAD_DIM, 2 * ML_HEAD_DIM), F32),
                        pltpu.VMEM((ML_HEADS, 1, LANES), F32)],
        compiler_params=_cparams(("arbitrary", "arbitrary")),
        name="mlstm",
    )(mqk, mvo, gates, conv_w, conv_b, gate_b, norm_g)


def _outproj_kernel(sb_ref, ml_ref, x_ref, mod_ref, wo1_ref, wo2_ref, lg_ref, lb_ref, wr_ref,
                    x1_ref, u2_ref, lo_ref):
    mix = _dot(sb_ref[...], wo1_ref[...]) + _dot(ml_ref[...], wo2_ref[...])
    g1 = mod_ref[2:3, :]
    x1 = _ln(DN_ALPHA * x_ref[...] + g1 * mix) * lg_ref[...] + lb_ref[...]
    x1_ref[...] = x1
    u2 = _ln(x1) * (1.0 + mod_ref[4:5, :]) + mod_ref[3:4, :]
    _store_token_rows(u2_ref, u2)
    lo_ref[...] = jnp.dot(u2, wr_ref[...], preferred_element_type=F32, precision=HIGHEST)


def _outproj(sb, ml, x, mod, wo1, wo2, ln_g, ln_b, w_router, tm):
    B, S, D = x.shape
    W = sb.shape[2]
    E = w_router.shape[1]
    row = lambda n: pl.BlockSpec((None, tm, n), lambda b, i: (b, i, 0))
    cst = lambda a: pl.BlockSpec(a.shape, lambda b, i: (0, 0))
    return pl.pallas_call(
        _outproj_kernel,
        grid=(B, S // tm),
        in_specs=[row(W), row(W), row(D), pl.BlockSpec((None, 6, D), lambda b, i: (b, 0, 0)),
                  cst(wo1), cst(wo2), cst(ln_g), cst(ln_b), cst(w_router)],
        out_specs=[row(D), pl.BlockSpec((2, None, tm, D // 4), lambda b, i: (0, b, i, 0)), row(E)],
        out_shape=[jax.ShapeDtypeStruct((B, S, D), F32),
                   jax.ShapeDtypeStruct((2, B, S, D // 4), jnp.uint32),
                   jax.ShapeDtypeStruct((B, S, E), F32)],
        compiler_params=_cparams(("arbitrary", "arbitrary")),
        name="outproj",
    )(sb, ml, x, mod, wo1, wo2, ln_g, ln_b, w_router)


def _group_allreduce(x, lane, op):
    n = x.shape[1]
    sh = 1
    while sh < GROUP_SIZE:
        down = pltpu.roll(x, sh, axis=1)
        up = pltpu.roll(x, n - sh, axis=1)
        x = op(x, jnp.where((lane & sh) != 0, down, up))
        sh *= 2
    return x


def _route_kernel(lo_ref, rb_ref, e_ref, g_ref, cnt_ref, cnt_scr):
    tm, E = lo_ref.shape

    @pl.when(pl.program_id(0) == 0)
    def _():
        cnt_scr[...] = jnp.zeros_like(cnt_scr)

    scores = jax.nn.sigmoid(lo_ref[...])
    sel = scores + rb_ref[...]
    lane = lax.broadcasted_iota(jnp.int32, (tm, E), 1)
    lane_f = lane.astype(F32)
    m1 = _group_allreduce(sel, lane, jnp.maximum)
    is_max = sel == m1
    cnt = _group_allreduce(is_max.astype(F32), lane, jnp.add)
    m2 = _group_allreduce(jnp.where(is_max, NEG_INF, sel), lane, jnp.maximum)
    gscore = m1 + jnp.where(cnt >= 2.0, m1, m2)
    grp = lane // GROUP_SIZE
    rank = jnp.zeros((tm, E), jnp.int32)
    for s in range(1, N_GROUPS):
        other = pltpu.roll(gscore, s * GROUP_SIZE, axis=1)
        beats = (other > gscore) | ((other == gscore) & (grp >= s))
        rank = rank + beats.astype(jnp.int32)
    cur = jnp.where(rank < TOPK_GROUPS, sel, NEG_INF)
    idxs = []
    chosen = jnp.zeros((tm, E), F32)
    for k in range(TOP_K):
        m = jnp.max(cur, axis=1, keepdims=True)
        idx = jnp.min(jnp.where(cur == m, lane_f, float(E)), axis=1, keepdims=True)
        pick = lane_f == idx
        cur = jnp.where(pick, NEG_INF, cur)
        chosen = jnp.where(pick, 1.0, chosen)
        idxs.append(idx)
    r = lax.broadcasted_iota(jnp.int32, (tm, tm), 0)
    c = lax.broadcasted_iota(jnp.int32, (tm, tm), 1)
    before = _dot((c < r).astype(BF16), chosen.astype(BF16)) + cnt_scr[...]
    col = lax.broadcasted_iota(jnp.int32, (tm, LANES), 1)
    e_out = jnp.zeros((tm, LANES), jnp.int32)
    g_out = jnp.zeros((tm, LANES), F32)
    gsum = jnp.zeros((tm, 1), F32)
    for k in range(TOP_K):
        pick = lane_f == idxs[k]
        gk = jnp.sum(jnp.where(pick, scores, 0.0), axis=1, keepdims=True)
        rk = jnp.sum(jnp.where(pick, before, 0.0), axis=1, keepdims=True)
        e_out = jnp.where(col == k, idxs[k].astype(jnp.int32), e_out)
        e_out = jnp.where(col == TOP_K + k, rk.astype(jnp.int32), e_out)
        g_out = jnp.where(col == k, gk, g_out)
        gsum = gsum + gk
    e_ref[...] = e_out
    g_ref[...] = g_out / gsum * ROUTED_SCALE
    cnt_scr[...] = cnt_scr[...] + jnp.sum(chosen, axis=0, keepdims=True)
    cnt_ref[...] = cnt_scr[...]


def _route(logits, router_bias, tm):
    T, E = logits.shape
    return pl.pallas_call(
        _route_kernel,
        grid=(T // tm,),
        in_specs=[pl.BlockSpec((tm, E), lambda i: (i, 0)), pl.BlockSpec((1, E), lambda i: (0, 0))],
        out_specs=[pl.BlockSpec((tm, LANES), lambda i: (i, 0)), pl.BlockSpec((tm, LANES), lambda i: (i, 0)),
                   pl.BlockSpec((1, E), lambda i: (0, 0))],
        out_shape=[jax.ShapeDtypeStruct((T, LANES), jnp.int32), jax.ShapeDtypeStruct((T, LANES), F32),
                   jax.ShapeDtypeStruct((1, E), F32)],
        scratch_shapes=[pltpu.VMEM((1, E), F32)],
        compiler_params=_cparams(("arbitrary",)),
        name="route",
    )(logits, router_bias.reshape(1, E))


def _dest_kernel(e_ref, off_ref, d_ref):
    tm = e_ref.shape[0]
    E = off_ref.shape[1]
    ev = e_ref[...]
    lane = lax.broadcasted_iota(jnp.int32, (tm, E), 1)
    col = lax.broadcasted_iota(jnp.int32, (tm, LANES), 1)
    off = off_ref[...]
    out = jnp.zeros((tm, LANES), jnp.int32)
    for k in range(TOP_K):
        base = jnp.sum(jnp.where(lane == ev[:, k:k + 1], off, 0.0), axis=1, keepdims=True)
        out = jnp.where(col == k, base.astype(jnp.int32) + ev[:, TOP_K + k:TOP_K + k + 1], out)
    d_ref[...] = out


def _dest(e_rank, offsets, tm):
    T = e_rank.shape[0]
    E = offsets.shape[1]
    return pl.pallas_call(
        _dest_kernel,
        grid=(T // tm,),
        in_specs=[pl.BlockSpec((tm, LANES), lambda i: (i, 0)), pl.BlockSpec((1, E), lambda i: (0, 0))],
        out_specs=pl.BlockSpec((tm, LANES), lambda i: (i, 0)),
        out_shape=jax.ShapeDtypeStruct((T, LANES), jnp.int32),
        compiler_params=_cparams(("arbitrary",)),
        name="dest",
    )(e_rank, offsets)


def _sc_mesh():
    return plsc.VectorSubcoreMesh(core_axis_name="core", subcore_axis_name="subcore")


def _sc_dispatch(u2p, dest_t, nrows):
    T, W = u2p.shape
    K = dest_t.shape[0]
    win = SC_WINDOW

    @pl.kernel(out_type=jax.ShapeDtypeStruct((nrows, W), u2p.dtype), mesh=_sc_mesh(), scratch_types=[])
    def scatter_rows(x_hbm, i_hbm, o_hbm):
        def body(x_vmem, i_vmem):
            for k in range(K):
                pltpu.sync_copy(x_vmem, o_hbm.at[i_vmem.at[k]])

        pltpu.emit_pipeline(
            body,
            grid=(T // win,),
            in_specs=[pl.BlockSpec((win, W), lambda i: (i, 0)),
                      pl.BlockSpec((K, win), lambda i: (0, i))],
            out_specs=[],
            core_axis_name=("core", "subcore"),
            dimension_semantics=(pltpu.PARALLEL,),
        )(x_hbm, i_hbm)

    return scatter_rows(u2p, dest_t)


def _sc_gather(ys, dest_flat):
    W = ys.shape[1]
    n = dest_flat.shape[1]
    win = SC_WINDOW

    @pl.kernel(out_type=jax.ShapeDtypeStruct((n, W), ys.dtype), mesh=_sc_mesh(), scratch_types=[])
    def gather_rows(y_hbm, i_hbm, o_hbm):
        def body(i_vmem, o_vmem):
            pltpu.sync_copy(y_hbm.at[i_vmem.at[0]], o_vmem)

        pltpu.emit_pipeline(
            body,
            grid=(n // win,),
            in_specs=[pl.BlockSpec((1, win), lambda i: (0, i))],
            out_specs=[pl.BlockSpec((win, W), lambda i: (i, 0))],
            core_axis_name=("core", "subcore"),
            dimension_semantics=(pltpu.PARALLEL,),
        )(i_hbm, o_hbm)

    return gather_rows(ys, dest_flat)


def _expert_kernel(be_ref, bv_ref, nb_ref, xs_ref, w1_ref, w3_ref, w2_ref, y_ref, w1b, w3b, w2b):
    i = pl.program_id(0)
    changed = jnp.logical_or(i == 0, be_ref[i] != be_ref[jnp.maximum(i - 1, 0)])

    @pl.when(jnp.logical_and(i < nb_ref[0], changed))
    def _():
        w1b[...] = w1_ref[0].astype(BF16)
        w3b[...] = w3_ref[0].astype(BF16)
        w2b[...] = w2_ref[0].astype(BF16)

    @pl.when(i < nb_ref[0])
    def _():
        xb = _load_token_rows(xs_ref, valid=bv_ref[i])
        a = _dot_blocks(xb, w1b)
        b = _dot_blocks(xb, w3b)
        hmid = (a * jax.nn.sigmoid(a) * b).astype(BF16)
        _store_token_rows(y_ref, _dot(hmid, w2b[...]))

    @pl.when(i >= nb_ref[0])
    def _():
        y_ref[...] = jnp.zeros_like(y_ref)


def _experts(block_e, block_valid, nb_used, xs, w1, w3, w2, rb):
    _, nrows, W = xs.shape
    nb = nrows // rb
    E, D, FF = w1.shape
    grid_spec = pltpu.PrefetchScalarGridSpec(
        num_scalar_prefetch=3,
        grid=(nb,),
        in_specs=[pl.BlockSpec((2, rb, W), lambda i, be, bv, n: (0, i, 0)),
                  pl.BlockSpec((1, D, FF), lambda i, be, bv, n: (be[i], 0, 0)),
                  pl.BlockSpec((1, D, FF), lambda i, be, bv, n: (be[i], 0, 0)),
                  pl.BlockSpec((1, FF, D), lambda i, be, bv, n: (be[i], 0, 0))],
        out_specs=pl.BlockSpec((2, rb, W), lambda i, be, bv, n: (0, i, 0)),
        scratch_shapes=[pltpu.VMEM((D, FF), BF16), pltpu.VMEM((D, FF), BF16), pltpu.VMEM((FF, D), BF16)],
    )
    return pl.pallas_call(
        _expert_kernel,
        grid_spec=grid_spec,
        out_shape=jax.ShapeDtypeStruct((2, nrows, W), jnp.uint32),
        compiler_params=_cparams(("arbitrary",)),
        name="experts",
    )(block_e, block_valid, nb_used, xs, w1, w3, w2)


def _combine_kernel(yg_ref, gw_ref, u_ref, x1_ref, mod_ref, s1_ref, s3_ref, s2_ref, lg_ref, lb_ref, o_ref):
    ub = _load_token_rows(u_ref)
    a = _dot_blocks(ub, s1_ref)
    b = _dot_blocks(ub, s3_ref)
    ffn = _dot((a * jax.nn.sigmoid(a) * b).astype(BF16), s2_ref[...])
    gw = gw_ref[...]
    routed = None
    for k in range(TOP_K):
        yk = [gw[:, k:k + 1] * blk.astype(F32) for blk in _load_token_rows(yg_ref.at[k])]
        routed = yk if routed is None else [r + y for r, y in zip(routed, yk)]
    ffn = ffn + jnp.concatenate(routed, axis=1)
    g2 = mod_ref[5:6, :]
    o_ref[...] = _ln(DN_ALPHA * x1_ref[...] + g2 * ffn) * lg_ref[...] + lb_ref[...]


def _combine(yg, gw, u2p, x1, mod, s1, s3, s2, ln_g, ln_b, tm):
    B, S, D = x1.shape
    K = yg.shape[0]
    row = lambda n: pl.BlockSpec((None, tm, n), lambda b, i: (b, i, 0))
    cst = lambda a: pl.BlockSpec(a.shape, lambda b, i: (0, 0))
    return pl.pallas_call(
        _combine_kernel,
        grid=(B, S // tm),
        in_specs=[pl.BlockSpec((K, 2, None, tm, D // 4), lambda b, i: (0, 0, b, i, 0)),
                  row(LANES), pl.BlockSpec((2, None, tm, D // 4), lambda b, i: (0, b, i, 0)), row(D),
                  pl.BlockSpec((None, 6, D), lambda b, i: (b, 0, 0)),
                  cst(s1), cst(s3), cst(s2), cst(ln_g), cst(ln_b)],
        out_specs=row(D),
        out_shape=jax.ShapeDtypeStruct((B, S, D), F32),
        compiler_params=_cparams(("arbitrary", "arbitrary")),
        name="combine",
    )(yg, gw, u2p, x1, mod, s1, s3, s2, ln_g, ln_b)


def _block_layout(counts, T, rb):
    counts = counts.reshape(-1).astype(jnp.int32)
    pcounts = (counts + rb - 1) // rb * rb
    pend = jnp.cumsum(pcounts)
    poffsets = pend - pcounts
    nb = (T * TOP_K) // rb + N_EXPERTS
    starts = jnp.arange(nb, dtype=jnp.int32) * rb
    block_e = jnp.minimum(jnp.sum(starts[:, None] >= pend[None, :], axis=1), N_EXPERTS - 1).astype(jnp.int32)
    nb_used = (pend[-1] // rb).astype(jnp.int32).reshape(1)
    mine = block_e[:, None] == jnp.arange(N_EXPERTS, dtype=jnp.int32)[None, :]
    seg_start = jnp.sum(jnp.where(mine, poffsets[None, :], 0), axis=1)
    seg_count = jnp.sum(jnp.where(mine, counts[None, :], 0), axis=1)
    block_valid = jnp.clip(seg_count - (starts - seg_start), 0, rb).astype(jnp.int32)
    return poffsets, block_e, block_valid, nb_used, nb * rb


def kernel(x, c, w_ada, b_ada, w_in, ml_conv_w, ml_conv_b, ml_gate_b, ml_norm_g, w_out, ln1_g, ln1_b,
           w_router, router_bias, moe_w1, moe_w3, moe_w2, sh_w1, sh_w3, sh_w2, ln2_g, ln2_b):
    B, S, D = x.shape
    T = B * S
    SBW = SB_HEADS * SB_HEAD_DIM
    MLW = ML_HEADS * ML_HEAD_DIM
    rb = EXPERT_BLOCK_ROWS
    for l in range(DEPTH):
        mod = _adaln(c, w_ada[l], b_ada[l]).reshape(B, 6, D)

        wi = w_in[l]
        c0 = 3 * SBW
        w_sb = wi[:, :c0].astype(BF16)
        w_mqk = wi[:, c0:c0 + 2 * MLW].astype(BF16)
        w_mvo = wi[:, c0 + 2 * MLW:c0 + 4 * MLW].astype(BF16)
        w_g = jnp.pad(wi[:, c0 + 4 * MLW:], ((0, 0), (0, LANES - 2 * ML_HEADS))).astype(BF16)
        sbp, mqk, mvo, gates = _inproj(x, mod, w_sb, w_mqk, w_mvo, w_g, tm=min(512, S))

        sb = _sb_attention(sbp, tq=min(256, S))

        gate_b = jnp.pad(ml_gate_b[l], (0, LANES - 2 * ML_HEADS)).reshape(1, LANES)
        ml = _mlstm(mqk, mvo, gates, ml_conv_w[l], ml_conv_b[l].reshape(1, -1), gate_b,
                    ml_norm_g[l].reshape(1, -1), L=min(256, S))

        wo = w_out[l].astype(BF16)
        x1, u2p, logits = _outproj(sb, ml, x, mod, wo[:SBW], wo[SBW:], ln1_g[l].reshape(1, D),
                                   ln1_b[l].reshape(1, D), w_router[l], tm=min(512, S))

        e_rank, gw, counts = _route(logits.reshape(T, N_EXPERTS), router_bias[l], tm=min(512, T))
        poffsets, block_e, block_valid, nb_used, nrows = _block_layout(counts, T, rb)
        dest = _dest(e_rank, poffsets.astype(F32).reshape(1, N_EXPERTS), tm=min(512, T))
        dest_t = dest[:, :TOP_K].T
        idx2 = jnp.concatenate([dest_t, dest_t + nrows], axis=1)
        xs = _sc_dispatch(u2p.reshape(2 * T, D // 4), idx2, 2 * nrows).reshape(2, nrows, D // 4)
        ys = _experts(block_e, block_valid, nb_used, xs, moe_w1[l], moe_w3[l], moe_w2[l], rb)
        yg = _sc_gather(ys.reshape(2 * nrows, D // 4), idx2.reshape(1, TOP_K * 2 * T))
        yg = yg.reshape(TOP_K, 2, B, S, D // 4)
        x = _combine(yg, gw.reshape(B, S, LANES), u2p, x1, mod, sh_w1[l].astype(BF16),
                     sh_w3[l].astype(BF16), sh_w2[l].astype(BF16), ln2_g[l].reshape(1, D),
                     ln2_b[l].reshape(1, D), tm=min(256, S))
    return x
```

```python
import functools
import math

import jax
import jax.numpy as jnp
from jax import lax
from jax.experimental import pallas as pl
from jax.experimental.pallas import tpu as pltpu
from jax.experimental.pallas import tpu_sc as plsc

F32 = jnp.float32
BF16 = jnp.bfloat16
HIGHEST = lax.Precision.HIGHEST

SB_HEADS = 8
SB_HEAD_DIM = 64
ML_HEADS = 4
ML_HEAD_DIM = 128
CONV_K = 4
N_EXPERTS = 256
TOP_K = 8
N_GROUPS = 8
TOPK_GROUPS = 4
GROUP_SIZE = N_EXPERTS // N_GROUPS
ROUTED_SCALE = 2.5
EXPERT_BLOCK_ROWS = 512
SC_WINDOW = 128
DEPTH = 1
DN_ALPHA = (2 * DEPTH) ** 0.25
LN_EPS = 1e-5
LANES = 128
SUBLANES = 8
NEG_INF = float("-inf")
SB_CUTOFF = 104.0
VMEM_LIMIT = 56 * 1024 * 1024


def _cparams(sem):
    return pltpu.CompilerParams(dimension_semantics=sem, vmem_limit_bytes=VMEM_LIMIT)


def _ln(x):
    mu = jnp.mean(x, axis=-1, keepdims=True)
    xc = x - mu
    var = jnp.mean(xc * xc, axis=-1, keepdims=True)
    return xc * lax.rsqrt(var + LN_EPS)


def _dot(a, b):
    return jnp.dot(a, b, preferred_element_type=F32)


def _dot_nt(a, b):
    return lax.dot_general(a, b, (((1,), (1,)), ((), ())), preferred_element_type=F32)


def _dot_tn(a, b):
    return lax.dot_general(a, b, (((0,), (0,)), ((), ())), preferred_element_type=F32)


def _pack_halves(v):
    w = v.shape[1] // 2
    lo = lax.bitcast_convert_type(v[:, :w].astype(BF16).astype(F32), jnp.uint32) >> 16
    hi = lax.bitcast_convert_type(v[:, w:].astype(BF16).astype(F32), jnp.uint32) & jnp.uint32(0xFFFF0000)
    return hi | lo


def _unpack_halves(p):
    lo = lax.bitcast_convert_type(p << 16, F32).astype(BF16)
    hi = lax.bitcast_convert_type(p & jnp.uint32(0xFFFF0000), F32).astype(BF16)
    return lo, hi


def _store_token_rows(ref, v):
    p = _pack_halves(v)
    q = p.shape[1] // 2
    ref[0] = p[:, :q]
    ref[1] = p[:, q:]


def _load_token_rows(ref, valid=None):
    first, second = ref[0], ref[1]
    if valid is not None:
        row = lax.broadcasted_iota(jnp.int32, first.shape, 0)
        first = jnp.where(row < valid, first, jnp.uint32(0))
        second = jnp.where(row < valid, second, jnp.uint32(0))
    lo_a, hi_a = _unpack_halves(first)
    lo_b, hi_b = _unpack_halves(second)
    return [lo_a, lo_b, hi_a, hi_b]


def _dot_blocks(blocks, w_ref):
    q = blocks[0].shape[1]
    acc = _dot(blocks[0], w_ref[0:q, :])
    for i in range(1, len(blocks)):
        acc = acc + _dot(blocks[i], w_ref[i * q:(i + 1) * q, :])
    return acc


def _adaln_kernel(c_ref, w_ref, b_ref, o_ref):
    c = c_ref[...]
    s = c * jax.nn.sigmoid(c)
    o_ref[...] = jnp.dot(s, w_ref[...], preferred_element_type=F32, precision=HIGHEST) + b_ref[...]


def _adaln(c, w_ada, b_ada):
    B, D = c.shape
    N = w_ada.shape[1]
    tn = 1024
    return pl.pallas_call(
        _adaln_kernel,
        grid=(N // tn,),
        in_specs=[pl.BlockSpec((B, D), lambda j: (0, 0)),
                  pl.BlockSpec((D, tn), lambda j: (0, j)),
                  pl.BlockSpec((1, tn), lambda j: (0, j))],
        out_specs=pl.BlockSpec((B, tn), lambda j: (0, j)),
        out_shape=jax.ShapeDtypeStruct((B, N), F32),
        compiler_params=_cparams(("arbitrary",)),
        name="adaln",
    )(c, w_ada, b_ada.reshape(1, N))


def _inproj_kernel(x_ref, mod_ref, wsb_ref, wqk_ref, wvo_ref, wg_ref,
                   sb_ref, mqk_ref, mvo_ref, g_ref):
    y = _ln(x_ref[...])
    sh = mod_ref[0:1, :]
    sc = mod_ref[1:2, :]
    u = (y * (1.0 + sc) + sh).astype(BF16)
    sb_ref[...] = _dot(u, wsb_ref[...]).astype(BF16)
    mqk_ref[...] = _dot(u, wqk_ref[...])
    mvo_ref[...] = _dot(u, wvo_ref[...]).astype(BF16)
    g_ref[...] = _dot(u, wg_ref[...])


def _inproj(x, mod, w_sb, w_mqk, w_mvo, w_g, tm):
    B, S, D = x.shape
    nsb, nqk, nvo, ng = w_sb.shape[1], w_mqk.shape[1], w_mvo.shape[1], w_g.shape[1]
    row = lambda n: pl.BlockSpec((None, tm, n), lambda b, i: (b, i, 0))
    full = lambda n: pl.BlockSpec((D, n), lambda b, i: (0, 0))
    return pl.pallas_call(
        _inproj_kernel,
        grid=(B, S // tm),
        in_specs=[row(D), pl.BlockSpec((None, 6, D), lambda b, i: (b, 0, 0)),
                  full(nsb), full(nqk), full(nvo), full(ng)],
        out_specs=[row(nsb), row(nqk), row(nvo), row(ng)],
        out_shape=[jax.ShapeDtypeStruct((B, S, nsb), BF16),
                   jax.ShapeDtypeStruct((B, S, nqk), F32),
                   jax.ShapeDtypeStruct((B, S, nvo), BF16),
                   jax.ShapeDtypeStruct((B, S, ng), F32)],
        compiler_params=_cparams(("arbitrary", "arbitrary")),
        name="inproj",
    )(x, mod, w_sb, w_mqk, w_mvo, w_g)


def _sb_block(z):
    sp = jnp.log(1.0 + jnp.exp(-jnp.abs(z)))
    log_beta = jnp.minimum(z, 0.0) - sp
    log_1m = -jnp.maximum(z, 0.0) - sp
    return log_beta, log_1m


def _sb_kernel(q_ref, k_ref, v_ref, o_ref, *, tq, scale):
    qi = pl.program_id(2)
    q2 = q_ref[...]
    lane = lax.broadcasted_iota(jnp.int32, (1, LANES), 1)
    r = lax.broadcasted_iota(jnp.int32, (tq, tq), 0)
    c = lax.broadcasted_iota(jnp.int32, (tq, tq), 1)
    upper = (r > c).astype(BF16)
    strict = c < r
    hmasks = [(lane // SB_HEAD_DIM) == h for h in range(2)]
    qs = [jnp.where(m, q2 * scale, jnp.zeros_like(q2)) for m in hmasks]

    def block(qh, kblk, vblk, carry, masked):
        z = _dot_nt(qh, kblk)
        log_beta, log_1m = _sb_block(z)
        if masked:
            log_1m = jnp.where(strict, log_1m, 0.0)
        after = _dot(log_1m.astype(BF16), upper)
        a = jnp.exp(log_beta + after + carry)
        if masked:
            a = jnp.where(strict, a, 0.0)
        pv = _dot(a.astype(BF16), vblk)
        return pv, carry + jnp.sum(log_1m, axis=1, keepdims=True)

    def both_heads(kb, accs, carries, masked):
        off = pl.multiple_of(kb * tq, tq)
        kblk = k_ref[pl.ds(off, tq), :]
        vblk = v_ref[pl.ds(off, tq), :]
        res = [block(qs[h], kblk, vblk, carries[h], masked) for h in range(2)]
        return [accs[h] + res[h][0] for h in range(2)], [res[h][1] for h in range(2)]

    zero = jnp.zeros((tq, LANES), F32)
    accs, carries = both_heads(qi, [zero, zero], [jnp.zeros((tq, 1), F32)] * 2, True)

    def top_of(carries):
        return jnp.max(jnp.maximum(carries[0], carries[1]))

    def cond(st):
        i, _, _, top = st
        return jnp.logical_and(i < qi, top > -SB_CUTOFF)

    def body(st):
        i, accs, carries, _ = st
        accs, carries = both_heads(qi - 1 - i, accs, carries, False)
        return i + 1, accs, carries, top_of(carries)

    _, accs, _, _ = lax.while_loop(cond, body, (jnp.int32(0), accs, carries, top_of(carries)))
    o_ref[...] = jnp.where(hmasks[0], accs[0], accs[1]).astype(o_ref.dtype)


def _sb_attention(sbp, tq):
    B, S, W3 = sbp.shape
    W = W3 // 3
    npair = W // LANES
    kern = functools.partial(_sb_kernel, tq=tq, scale=SB_HEAD_DIM ** -0.5)
    return pl.pallas_call(
        kern,
        grid=(B, npair, S // tq),
        in_specs=[pl.BlockSpec((None, tq, LANES), lambda b, p, i: (b, i, p)),
                  pl.BlockSpec((None, S, LANES), lambda b, p, i: (b, 0, npair + p)),
                  pl.BlockSpec((None, S, LANES), lambda b, p, i: (b, 0, 2 * npair + p))],
        out_specs=pl.BlockSpec((None, tq, LANES), lambda b, p, i: (b, i, p)),
        out_shape=jax.ShapeDtypeStruct((B, S, W), BF16),
        compiler_params=_cparams(("arbitrary", "arbitrary", "arbitrary")),
        name="sb_attention",
    )(sbp, sbp, sbp)


def _mlstm_kernel(qk_ref, vo_ref, g_ref, cw_ref, cb_ref, gb_ref, ng_ref, o_ref,
                  xbuf, ct_ref, m_ref, *, L):
    H, dk = ML_HEADS, ML_HEAD_DIM
    W = H * dk
    ci = pl.program_id(1)

    @pl.when(ci == 0)
    def _():
        xbuf[0:8, :] = jnp.zeros((8, 2 * W), F32)
        ct_ref[...] = jnp.zeros_like(ct_ref)
        m_ref[...] = jnp.zeros_like(m_ref)

    xbuf[8:8 + L, :] = qk_ref[...]
    y = cb_ref[...] + cw_ref[CONV_K - 1:CONV_K, :] * xbuf[8:8 + L, :]
    for j in range(1, CONV_K):
        y = y + cw_ref[CONV_K - 1 - j:CONV_K - j, :] * xbuf[8 - j:8 - j + L, :]
    xbuf[0:8, :] = xbuf[L:L + 8, :]
    qk = y * jax.nn.sigmoid(y)

    g = g_ref[...] + gb_ref[...]
    logf = jax.nn.log_sigmoid(g)
    r = lax.broadcasted_iota(jnp.int32, (L, L), 0)
    c = lax.broadcasted_iota(jnp.int32, (L, L), 1)
    causal = c <= r
    tri = causal.astype(BF16)
    lf_hi = logf.astype(BF16)
    lf_lo = (logf - lf_hi.astype(F32)).astype(BF16)
    bc = _dot(tri, lf_hi) + _dot(tri, lf_lo)
    g_t = g.T
    bc_t = bc.T
    e0 = (lax.broadcasted_iota(jnp.int32, (L, LANES), 1) == 0).astype(F32)

    for h in range(H):
        qh = qk[:, h * dk:(h + 1) * dk].astype(BF16)
        kh = (qk[:, W + h * dk:W + (h + 1) * dk] * (dk ** -0.5)).astype(BF16)
        vh = vo_ref[:, h * dk:(h + 1) * dk].astype(F32)
        oh = vo_ref[:, W + h * dk:W + (h + 1) * dk].astype(F32)
        vaug = jnp.concatenate([vh, e0], axis=1)
        ic_col = g[:, h:h + 1]
        ic_row = g_t[h:h + 1, :]
        bc_col = bc[:, H + h:H + h + 1]
        bc_row = bc_t[H + h:H + h + 1, :]
        m = m_ref[h][:, 0:1]
        ct = ct_ref[h]

        log_d = jnp.where(causal, bc_col - bc_row + ic_row, NEG_INF)
        inter = bc_col + m
        m_t = jnp.maximum(inter, jnp.max(log_d, axis=1, keepdims=True))
        w = _dot_nt(qh, kh) * jnp.exp(log_d - m_t)
        s_inter = jnp.exp(inter - m_t)
        tot = s_inter * _dot(qh, ct.astype(BF16)) + _dot(w.astype(BF16), vaug.astype(BF16))
        num = tot[:, :dk]
        den = tot[:, dk:dk + 1]
        hh = num / jnp.maximum(jnp.abs(den), jnp.exp(-m_t))

        b_last = bc_col[L - 1:L, :]
        log_w = b_last - bc_col + ic_col
        m_new = jnp.maximum(b_last + m, jnp.max(log_w, axis=0, keepdims=True))
        wk = jnp.exp(log_w - m_new)
        decay = jnp.exp(b_last + m - m_new)
        ct_ref[h] = decay * ct + _dot_tn(kh, (wk * vaug).astype(BF16))
        m_ref[h] = jnp.broadcast_to(m_new, (1, LANES))

        hn = _ln(hh) * ng_ref[:, h * dk:(h + 1) * dk]
        o_ref[:, h * dk:(h + 1) * dk] = (jax.nn.sigmoid(oh) * hn).astype(o_ref.dtype)


def _mlstm(mqk, mvo, gates, conv_w, conv_b, gate_b, norm_g, L):
    B, S, W2 = mqk.shape
    W = W2 // 2
    kern = functools.partial(_mlstm_kernel, L=L)
    row = lambda n: pl.BlockSpec((None, L, n), lambda b, i: (b, i, 0))
    cst = lambda a: pl.BlockSpec(a.shape, lambda b, i: (0, 0))
    return pl.pallas_call(
        kern,
        grid=(B, S // L),
        in_specs=[row(W2), row(W2), row(LANES), cst(conv_w), cst(conv_b), cst(gate_b), cst(norm_g)],
        out_specs=row(W),
        out_shape=jax.ShapeDtypeStruct((B, S, W), BF16),
        scratch_shapes=[pltpu.VMEM((L + 8, W2), F32),
                        pltpu.VMEM((ML_HEADS, ML_HEAD_DIM, 2 * ML_HEAD_DIM), F32),
                        pltpu.VMEM((ML_HEADS, 1, LANES), F32)],
        compiler_params=_cparams(("arbitrary", "arbitrary")),
        name="mlstm",
    )(mqk, mvo, gates, conv_w, conv_b, gate_b, norm_g)


def _outproj_kernel(sb_ref, ml_ref, x_ref, mod_ref, wo1_ref, wo2_ref, lg_ref, lb_ref, wrh_ref, wrl_ref,
                    x1_ref, u2_ref, lo_ref):
    mix = _dot(sb_ref[...], wo1_ref[...]) + _dot(ml_ref[...], wo2_ref[...])
    g1 = mod_ref[2:3, :]
    x1 = _ln(DN_ALPHA * x_ref[...] + g1 * mix) * lg_ref[...] + lb_ref[...]
    x1_ref[...] = x1
    u2 = _ln(x1) * (1.0 + mod_ref[4:5, :]) + mod_ref[3:4, :]
    _store_token_rows(u2_ref, u2)
    u_hi = u2.astype(BF16)
    u_lo = (u2 - u_hi.astype(F32)).astype(BF16)
    lo_ref[...] = _dot_nt(wrh_ref[...], u_hi) + (_dot_nt(wrl_ref[...], u_hi) + _dot_nt(wrh_ref[...], u_lo))


def _outproj(sb, ml, x, mod, wo1, wo2, ln_g, ln_b, wr_hi, wr_lo, tm):
    B, S, D = x.shape
    W = sb.shape[2]
    E = wr_hi.shape[0]
    nt = S // tm
    row = lambda n: pl.BlockSpec((None, tm, n), lambda b, i: (b, i, 0))
    cst = lambda a: pl.BlockSpec(a.shape, lambda b, i: (0, 0))
    return pl.pallas_call(
        _outproj_kernel,
        grid=(B, nt),
        in_specs=[row(W), row(W), row(D), pl.BlockSpec((None, 6, D), lambda b, i: (b, 0, 0)),
                  cst(wo1), cst(wo2), cst(ln_g), cst(ln_b), cst(wr_hi), cst(wr_lo)],
        out_specs=[row(D), pl.BlockSpec((2, None, tm, D // 4), lambda b, i: (0, b, i, 0)),
                   pl.BlockSpec((E, tm), lambda b, i: (0, b * nt + i))],
        out_shape=[jax.ShapeDtypeStruct((B, S, D), F32),
                   jax.ShapeDtypeStruct((2, B, S, D // 4), jnp.uint32),
                   jax.ShapeDtypeStruct((E, B * S), F32)],
        compiler_params=_cparams(("arbitrary", "arbitrary")),
        name="outproj",
    )(sb, ml, x, mod, wo1, wo2, ln_g, ln_b, wr_hi, wr_lo)


def _rows_to_block(rows, dtype):
    n = rows[0].shape[1]
    rid = lax.broadcasted_iota(jnp.int32, (len(rows), n), 0)
    out = jnp.zeros((len(rows), n), dtype)
    for k, v in enumerate(rows):
        out = jnp.where(rid == k, v.astype(dtype), out)
    return out


def _route_kernel(lo_ref, rb_ref, e_ref, r_ref, g_ref, cnt_ref, cnt_scr):
    E, tm = lo_ref.shape

    @pl.when(pl.program_id(0) == 0)
    def _():
        cnt_scr[...] = jnp.zeros_like(cnt_scr)

    scores = jax.nn.sigmoid(lo_ref[...])
    sel = scores + rb_ref[...]
    row_f = lax.broadcasted_iota(jnp.int32, (E, tm), 0).astype(F32)
    groups = [sel[g * GROUP_SIZE:(g + 1) * GROUP_SIZE, :] for g in range(N_GROUPS)]
    gscore = []
    for xg in groups:
        m1 = jnp.max(xg, axis=0, keepdims=True)
        is_max = xg == m1
        cnt = jnp.sum(is_max.astype(F32), axis=0, keepdims=True)
        m2 = jnp.max(jnp.where(is_max, NEG_INF, xg), axis=0, keepdims=True)
        gscore.append(m1 + jnp.where(cnt >= 2.0, m1, m2))
    kept = []
    for g in range(N_GROUPS):
        rank = jnp.zeros((1, tm), jnp.int32)
        for o in range(N_GROUPS):
            if o != g:
                beats = (gscore[o] >= gscore[g]) if o < g else (gscore[o] > gscore[g])
                rank = rank + beats.astype(jnp.int32)
        kept.append(jnp.where(rank < TOPK_GROUPS, groups[g], NEG_INF))
    cur = jnp.concatenate(kept, axis=0)
    idxs = []
    chosen = jnp.zeros((E, tm), F32)
    for k in range(TOP_K):
        m = jnp.max(cur, axis=0, keepdims=True)
        idx = jnp.min(jnp.where(cur == m, row_f, float(E)), axis=0, keepdims=True)
        pick = row_f == idx
        cur = jnp.where(pick, NEG_INF, cur)
        chosen = jnp.where(pick, 1.0, chosen)
        idxs.append(idx)
    r = lax.broadcasted_iota(jnp.int32, (tm, tm), 0)
    c = lax.broadcasted_iota(jnp.int32, (tm, tm), 1)
    before = _dot(chosen.astype(BF16), (r < c).astype(BF16)) + cnt_scr[...]
    gates, ranks = [], []
    for k in range(TOP_K):
        pick = row_f == idxs[k]
        gates.append(jnp.sum(jnp.where(pick, scores, 0.0), axis=0, keepdims=True))
        ranks.append(jnp.sum(jnp.where(pick, before, 0.0), axis=0, keepdims=True))
    gsum = gates[0]
    for gk in gates[1:]:
        gsum = gsum + gk
    e_ref[...] = _rows_to_block(idxs, jnp.int32)
    r_ref[...] = _rows_to_block(ranks, jnp.int32)
    g_ref[...] = _rows_to_block(gates, F32) / gsum * ROUTED_SCALE
    cnt_scr[...] = cnt_scr[...] + jnp.sum(chosen, axis=1, keepdims=True)
    cnt_ref[...] = cnt_scr[...]


def _route(logits_t, router_bias, tm):
    E, T = logits_t.shape
    tok = lambda dt: jax.ShapeDtypeStruct((TOP_K, T), dt)
    return pl.pallas_call(
        _route_kernel,
        grid=(T // tm,),
        in_specs=[pl.BlockSpec((E, tm), lambda i: (0, i)), pl.BlockSpec((E, 1), lambda i: (0, 0))],
        out_specs=[pl.BlockSpec((TOP_K, tm), lambda i: (0, i))] * 3 + [pl.BlockSpec((E, 1), lambda i: (0, 0))],
        out_shape=[tok(jnp.int32), tok(jnp.int32), tok(F32), jax.ShapeDtypeStruct((E, 1), F32)],
        scratch_shapes=[pltpu.VMEM((E, 1), F32)],
        compiler_params=_cparams(("arbitrary",)),
        name="route",
    )(logits_t, router_bias.reshape(E, 1))


def _dest_kernel(e_ref, r_ref, off_ref, d_ref):
    E = off_ref.shape[0]
    tm = e_ref.shape[1]
    row = lax.broadcasted_iota(jnp.int32, (E, tm), 0)
    off = off_ref[...]
    base = [jnp.sum(jnp.where(row == e_ref[k:k + 1, :], off, 0.0), axis=0, keepdims=True) for k in range(TOP_K)]
    d_ref[...] = _rows_to_block(base, jnp.int32) + r_ref[...]


def _dest(e_t, r_t, offsets, tm):
    K, T = e_t.shape
    E = offsets.shape[0]
    tok = pl.BlockSpec((K, tm), lambda i: (0, i))
    return pl.pallas_call(
        _dest_kernel,
        grid=(T // tm,),
        in_specs=[tok, tok, pl.BlockSpec((E, 1), lambda i: (0, 0))],
        out_specs=tok,
        out_shape=jax.ShapeDtypeStruct((K, T), jnp.int32),
        compiler_params=_cparams(("arbitrary",)),
        name="dest",
    )(e_t, r_t, offsets)


def _sc_mesh():
    return plsc.VectorSubcoreMesh(core_axis_name="core", subcore_axis_name="subcore")


def _sc_dispatch(u2p, dest_t, nrows):
    T, W = u2p.shape
    K = dest_t.shape[0]
    win = SC_WINDOW

    @pl.kernel(out_type=jax.ShapeDtypeStruct((nrows, W), u2p.dtype), mesh=_sc_mesh(), scratch_types=[])
    def scatter_rows(x_hbm, i_hbm, o_hbm):
        def body(x_vmem, i_vmem):
            for k in range(K):
                pltpu.sync_copy(x_vmem, o_hbm.at[i_vmem.at[k]])

        pltpu.emit_pipeline(
            body,
            grid=(T // win,),
            in_specs=[pl.BlockSpec((win, W), lambda i: (i, 0)),
                      pl.BlockSpec((K, win), lambda i: (0, i))],
            out_specs=[],
            core_axis_name=("core", "subcore"),
            dimension_semantics=(pltpu.PARALLEL,),
        )(x_hbm, i_hbm)

    return scatter_rows(u2p, dest_t)


def _sc_gather(ys, dest_flat):
    W = ys.shape[1]
    n = dest_flat.shape[1]
    win = SC_WINDOW

    @pl.kernel(out_type=jax.ShapeDtypeStruct((n, W), ys.dtype), mesh=_sc_mesh(), scratch_types=[])
    def gather_rows(y_hbm, i_hbm, o_hbm):
        def body(i_vmem, o_vmem):
            pltpu.sync_copy(y_hbm.at[i_vmem.at[0]], o_vmem)

        pltpu.emit_pipeline(
            body,
            grid=(n // win,),
            in_specs=[pl.BlockSpec((1, win), lambda i: (0, i))],
            out_specs=[pl.BlockSpec((win, W), lambda i: (i, 0))],
            core_axis_name=("core", "subcore"),
            dimension_semantics=(pltpu.PARALLEL,),
        )(i_hbm, o_hbm)

    return gather_rows(ys, dest_flat)


def _expert_kernel(be_ref, bv_ref, nb_ref, xs_ref, w1_ref, w3_ref, w2_ref, y_ref, w1b, w3b, w2b):
    i = pl.program_id(0)
    changed = jnp.logical_or(i == 0, be_ref[i] != be_ref[jnp.maximum(i - 1, 0)])

    @pl.when(jnp.logical_and(i < nb_ref[0], changed))
    def _():
        w1b[...] = w1_ref[0].astype(BF16)
        w3b[...] = w3_ref[0].astype(BF16)
        w2b[...] = w2_ref[0].astype(BF16)

    @pl.when(i < nb_ref[0])
    def _():
        xb = _load_token_rows(xs_ref, valid=bv_ref[i])
        a = _dot_blocks(xb, w1b)
        b = _dot_blocks(xb, w3b)
        hmid = (a * jax.nn.sigmoid(a) * b).astype(BF16)
        _store_token_rows(y_ref, _dot(hmid, w2b[...]))


def _experts(block_e, block_valid, nb_used, xs, w1, w3, w2, rb):
    _, nrows, W = xs.shape
    nb = nrows // rb
    E, D, FF = w1.shape
    last = lambda i, n: jnp.minimum(i, n[0] - 1)
    grid_spec = pltpu.PrefetchScalarGridSpec(
        num_scalar_prefetch=3,
        grid=(nb,),
        in_specs=[pl.BlockSpec((2, rb, W), lambda i, be, bv, n: (0, last(i, n), 0)),
                  pl.BlockSpec((1, D, FF), lambda i, be, bv, n: (be[last(i, n)], 0, 0)),
                  pl.BlockSpec((1, D, FF), lambda i, be, bv, n: (be[last(i, n)], 0, 0)),
                  pl.BlockSpec((1, FF, D), lambda i, be, bv, n: (be[last(i, n)], 0, 0))],
        out_specs=pl.BlockSpec((2, rb, W), lambda i, be, bv, n: (0, last(i, n), 0)),
        scratch_shapes=[pltpu.VMEM((D, FF), BF16), pltpu.VMEM((D, FF), BF16), pltpu.VMEM((FF, D), BF16)],
    )
    return pl.pallas_call(
        _expert_kernel,
        grid_spec=grid_spec,
        out_shape=jax.ShapeDtypeStruct((2, nrows, W), jnp.uint32),
        compiler_params=_cparams(("arbitrary",)),
        name="experts",
    )(block_e, block_valid, nb_used, xs, w1, w3, w2)


def _combine_kernel(yg_ref, gw_ref, u_ref, x1_ref, mod_ref, s1_ref, s3_ref, s2_ref, lg_ref, lb_ref, o_ref):
    ub = _load_token_rows(u_ref)
    a = _dot_blocks(ub, s1_ref)
    b = _dot_blocks(ub, s3_ref)
    ffn = _dot((a * jax.nn.sigmoid(a) * b).astype(BF16), s2_ref[...])
    gw = gw_ref[...]
    routed = None
    for k in range(TOP_K):
        yk = [gw[:, k:k + 1] * blk.astype(F32) for blk in _load_token_rows(yg_ref.at[k])]
        routed = yk if routed is None else [r + y for r, y in zip(routed, yk)]
    ffn = ffn + jnp.concatenate(routed, axis=1)
    g2 = mod_ref[5:6, :]
    o_ref[...] = _ln(DN_ALPHA * x1_ref[...] + g2 * ffn) * lg_ref[...] + lb_ref[...]


def _combine(yg, gw, u2p, x1, mod, s1, s3, s2, ln_g, ln_b, tm):
    B, S, D = x1.shape
    K = yg.shape[0]
    row = lambda n: pl.BlockSpec((None, tm, n), lambda b, i: (b, i, 0))
    cst = lambda a: pl.BlockSpec(a.shape, lambda b, i: (0, 0))
    return pl.pallas_call(
        _combine_kernel,
        grid=(B, S // tm),
        in_specs=[pl.BlockSpec((K, 2, None, tm, D // 4), lambda b, i: (0, 0, b, i, 0)),
                  row(K), pl.BlockSpec((2, None, tm, D // 4), lambda b, i: (0, b, i, 0)), row(D),
                  pl.BlockSpec((None, 6, D), lambda b, i: (b, 0, 0)),
                  cst(s1), cst(s3), cst(s2), cst(ln_g), cst(ln_b)],
        out_specs=row(D),
        out_shape=jax.ShapeDtypeStruct((B, S, D), F32),
        compiler_params=_cparams(("arbitrary", "arbitrary")),
        name="combine",
    )(yg, gw, u2p, x1, mod, s1, s3, s2, ln_g, ln_b)


def _block_layout(counts, T, rb):
    counts = counts.reshape(-1).astype(jnp.int32)
    pcounts = (counts + rb - 1) // rb * rb
    pend = jnp.cumsum(pcounts)
    poffsets = pend - pcounts
    nb = (T * TOP_K) // rb + N_EXPERTS
    starts = jnp.arange(nb, dtype=jnp.int32) * rb
    block_e = jnp.minimum(jnp.sum(starts[:, None] >= pend[None, :], axis=1), N_EXPERTS - 1).astype(jnp.int32)
    nb_used = (pend[-1] // rb).astype(jnp.int32).reshape(1)
    mine = block_e[:, None] == jnp.arange(N_EXPERTS, dtype=jnp.int32)[None, :]
    seg_start = jnp.sum(jnp.where(mine, poffsets[None, :], 0), axis=1)
    seg_count = jnp.sum(jnp.where(mine, counts[None, :], 0), axis=1)
    block_valid = jnp.clip(seg_count - (starts - seg_start), 0, rb).astype(jnp.int32)
    return poffsets, block_e, block_valid, nb_used, nb * rb


def kernel(x, c, w_ada, b_ada, w_in, ml_conv_w, ml_conv_b, ml_gate_b, ml_norm_g, w_out, ln1_g, ln1_b,
           w_router, router_bias, moe_w1, moe_w3, moe_w2, sh_w1, sh_w3, sh_w2, ln2_g, ln2_b):
    B, S, D = x.shape
    T = B * S
    SBW = SB_HEADS * SB_HEAD_DIM
    MLW = ML_HEADS * ML_HEAD_DIM
    rb = EXPERT_BLOCK_ROWS
    for l in range(DEPTH):
        mod = _adaln(c, w_ada[l], b_ada[l]).reshape(B, 6, D)

        wi = w_in[l]
        c0 = 3 * SBW
        w_sb = wi[:, :c0].astype(BF16)
        w_mqk = wi[:, c0:c0 + 2 * MLW].astype(BF16)
        w_mvo = wi[:, c0 + 2 * MLW:c0 + 4 * MLW].astype(BF16)
        w_g = jnp.pad(wi[:, c0 + 4 * MLW:], ((0, 0), (0, LANES - 2 * ML_HEADS))).astype(BF16)
        sbp, mqk, mvo, gates = _inproj(x, mod, w_sb, w_mqk, w_mvo, w_g, tm=min(512, S))

        sb = _sb_attention(sbp, tq=min(256, S))

        gate_b = jnp.pad(ml_gate_b[l], (0, LANES - 2 * ML_HEADS)).reshape(1, LANES)
        ml = _mlstm(mqk, mvo, gates, ml_conv_w[l], ml_conv_b[l].reshape(1, -1), gate_b,
                    ml_norm_g[l].reshape(1, -1), L=min(256, S))

        wo = w_out[l].astype(BF16)
        wr = w_router[l].T
        wr_hi = wr.astype(BF16)
        wr_lo = (wr - wr_hi.astype(F32)).astype(BF16)
        x1, u2p, logits_t = _outproj(sb, ml, x, mod, wo[:SBW], wo[SBW:], ln1_g[l].reshape(1, D),
                                     ln1_b[l].reshape(1, D), wr_hi, wr_lo, tm=min(512, S))

        e_t, r_t, g_t, counts = _route(logits_t, router_bias[l], tm=min(512, T))
        poffsets, block_e, block_valid, nb_used, nrows = _block_layout(counts, T, rb)
        dest_t = _dest(e_t, r_t, poffsets.astype(F32).reshape(N_EXPERTS, 1), tm=min(512, T))
        gw = g_t.T
        idx2 = jnp.concatenate([dest_t, dest_t + nrows], axis=1)
        xs = _sc_dispatch(u2p.reshape(2 * T, D // 4), idx2, 2 * nrows).reshape(2, nrows, D // 4)
        ys = _experts(block_e, block_valid, nb_used, xs, moe_w1[l], moe_w3[l], moe_w2[l], rb)
        yg = _sc_gather(ys.reshape(2 * nrows, D // 4), idx2.reshape(1, TOP_K * 2 * T))
        yg = yg.reshape(TOP_K, 2, B, S, D // 4)
        x = _combine(yg, gw.reshape(B, S, TOP_K), u2p, x1, mod, sh_w1[l].astype(BF16),
                     sh_w3[l].astype(BF16), sh_w2[l].astype(BF16), ln2_g[l].reshape(1, D),
                     ln2_b[l].reshape(1, D), tm=min(256, S))
    return x
```

```python
import functools
import math

import jax
import jax.numpy as jnp
from jax import lax
from jax.experimental import pallas as pl
from jax.experimental.pallas import tpu as pltpu
from jax.experimental.pallas import tpu_sc as plsc

F32 = jnp.float32
BF16 = jnp.bfloat16
HIGHEST = lax.Precision.HIGHEST

SB_HEADS = 8
SB_HEAD_DIM = 64
ML_HEADS = 4
ML_HEAD_DIM = 128
CONV_K = 4
N_EXPERTS = 256
TOP_K = 8
N_GROUPS = 8
TOPK_GROUPS = 4
GROUP_SIZE = N_EXPERTS // N_GROUPS
ROUTED_SCALE = 2.5
EXPERT_BLOCK_ROWS = 512
SC_WINDOW = 128
DEPTH = 1
DN_ALPHA = (2 * DEPTH) ** 0.25
LN_EPS = 1e-5
LANES = 128
SUBLANES = 8
NEG_INF = float("-inf")
SB_CUTOFF = 104.0
VMEM_LIMIT = 56 * 1024 * 1024


def _cparams(sem):
    return pltpu.CompilerParams(dimension_semantics=sem, vmem_limit_bytes=VMEM_LIMIT)


def _ln(x):
    mu = jnp.mean(x, axis=-1, keepdims=True)
    xc = x - mu
    var = jnp.mean(xc * xc, axis=-1, keepdims=True)
    return xc * lax.rsqrt(var + LN_EPS)


def _dot(a, b):
    return jnp.dot(a, b, preferred_element_type=F32)


def _dot_nt(a, b):
    return lax.dot_general(a, b, (((1,), (1,)), ((), ())), preferred_element_type=F32)


def _dot_tn(a, b):
    return lax.dot_general(a, b, (((0,), (0,)), ((), ())), preferred_element_type=F32)


def _pack_halves(v):
    w = v.shape[1] // 2
    lo = lax.bitcast_convert_type(v[:, :w].astype(BF16).astype(F32), jnp.uint32) >> 16
    hi = lax.bitcast_convert_type(v[:, w:].astype(BF16).astype(F32), jnp.uint32) & jnp.uint32(0xFFFF0000)
    return hi | lo


def _unpack_halves(p):
    lo = lax.bitcast_convert_type(p << 16, F32).astype(BF16)
    hi = lax.bitcast_convert_type(p & jnp.uint32(0xFFFF0000), F32).astype(BF16)
    return lo, hi


def _store_token_rows(ref, v):
    p = _pack_halves(v)
    q = p.shape[1] // 2
    ref[0] = p[:, :q]
    ref[1] = p[:, q:]


def _load_token_rows(ref, valid=None):
    first, second = ref[0], ref[1]
    if valid is not None:
        row = lax.broadcasted_iota(jnp.int32, first.shape, 0)
        first = jnp.where(row < valid, first, jnp.uint32(0))
        second = jnp.where(row < valid, second, jnp.uint32(0))
    lo_a, hi_a = _unpack_halves(first)
    lo_b, hi_b = _unpack_halves(second)
    return [lo_a, lo_b, hi_a, hi_b]


def _dot_blocks(blocks, w_ref):
    q = blocks[0].shape[1]
    acc = _dot(blocks[0], w_ref[0:q, :])
    for i in range(1, len(blocks)):
        acc = acc + _dot(blocks[i], w_ref[i * q:(i + 1) * q, :])
    return acc


def _adaln_kernel(c_ref, w_ref, b_ref, o_ref):
    c = c_ref[...]
    s = c * jax.nn.sigmoid(c)
    o_ref[...] = jnp.dot(s, w_ref[...], preferred_element_type=F32, precision=HIGHEST) + b_ref[...]


def _adaln(c, w_ada, b_ada):
    B, D = c.shape
    N = w_ada.shape[1]
    tn = 1024
    return pl.pallas_call(
        _adaln_kernel,
        grid=(N // tn,),
        in_specs=[pl.BlockSpec((B, D), lambda j: (0, 0)),
                  pl.BlockSpec((D, tn), lambda j: (0, j)),
                  pl.BlockSpec((1, tn), lambda j: (0, j))],
        out_specs=pl.BlockSpec((B, tn), lambda j: (0, j)),
        out_shape=jax.ShapeDtypeStruct((B, N), F32),
        compiler_params=_cparams(("arbitrary",)),
        name="adaln",
    )(c, w_ada, b_ada.reshape(1, N))


def _inproj_kernel(x_ref, mod_ref, wsb_ref, wqk_ref, wvo_ref, wg_ref,
                   sb_ref, mqk_ref, mvo_ref, g_ref):
    y = _ln(x_ref[...])
    sh = mod_ref[0:1, :]
    sc = mod_ref[1:2, :]
    u = (y * (1.0 + sc) + sh).astype(BF16)
    sb_ref[...] = _dot(u, wsb_ref[...]).astype(BF16)
    mqk_ref[...] = _dot(u, wqk_ref[...])
    mvo_ref[...] = _dot(u, wvo_ref[...]).astype(BF16)
    g_ref[...] = _dot(u, wg_ref[...])


def _inproj(x, mod, w_sb, w_mqk, w_mvo, w_g, tm):
    B, S, D = x.shape
    nsb, nqk, nvo, ng = w_sb.shape[1], w_mqk.shape[1], w_mvo.shape[1], w_g.shape[1]
    row = lambda n: pl.BlockSpec((None, tm, n), lambda b, i: (b, i, 0))
    full = lambda n: pl.BlockSpec((D, n), lambda b, i: (0, 0))
    return pl.pallas_call(
        _inproj_kernel,
        grid=(B, S // tm),
        in_specs=[row(D), pl.BlockSpec((None, 6, D), lambda b, i: (b, 0, 0)),
                  full(nsb), full(nqk), full(nvo), full(ng)],
        out_specs=[row(nsb), row(nqk), row(nvo), row(ng)],
        out_shape=[jax.ShapeDtypeStruct((B, S, nsb), BF16),
                   jax.ShapeDtypeStruct((B, S, nqk), F32),
                   jax.ShapeDtypeStruct((B, S, nvo), BF16),
                   jax.ShapeDtypeStruct((B, S, ng), F32)],
        compiler_params=_cparams(("arbitrary", "arbitrary")),
        name="inproj",
    )(x, mod, w_sb, w_mqk, w_mvo, w_g)


def _sb_block(z):
    sp = jnp.log(1.0 + jnp.exp(-jnp.abs(z)))
    log_beta = jnp.minimum(z, 0.0) - sp
    return log_beta, log_beta - z


def _sb_kernel(q_ref, k_ref, v_ref, o_ref, *, tq, scale):
    qi = pl.program_id(2)
    q2 = q_ref[...]
    lane = lax.broadcasted_iota(jnp.int32, (1, LANES), 1)
    r = lax.broadcasted_iota(jnp.int32, (tq, tq), 0)
    c = lax.broadcasted_iota(jnp.int32, (tq, tq), 1)
    upper = (r > c).astype(BF16)
    strict = c < r
    hmasks = [(lane // SB_HEAD_DIM) == h for h in range(2)]
    qs = [jnp.where(m, q2 * scale, jnp.zeros_like(q2)) for m in hmasks]

    def block(qh, kblk, vblk, carry, masked):
        z = _dot_nt(qh, kblk)
        log_beta, log_1m = _sb_block(z)
        if masked:
            log_1m = jnp.where(strict, log_1m, 0.0)
        after = _dot(log_1m.astype(BF16), upper)
        a = jnp.exp(log_beta + after + carry)
        if masked:
            a = jnp.where(strict, a, 0.0)
        pv = _dot(a.astype(BF16), vblk)
        return pv, carry + jnp.sum(log_1m, axis=1, keepdims=True)

    def both_heads(kb, accs, carries, masked):
        off = pl.multiple_of(kb * tq, tq)
        kblk = k_ref[pl.ds(off, tq), :]
        vblk = v_ref[pl.ds(off, tq), :]
        res = [block(qs[h], kblk, vblk, carries[h], masked) for h in range(2)]
        return [accs[h] + res[h][0] for h in range(2)], [res[h][1] for h in range(2)]

    zero = jnp.zeros((tq, LANES), F32)
    accs, carries = both_heads(qi, [zero, zero], [jnp.zeros((tq, 1), F32)] * 2, True)

    def top_of(carries):
        return jnp.max(jnp.maximum(carries[0], carries[1]))

    def cond(st):
        i, _, _, top = st
        return jnp.logical_and(i < qi, top > -SB_CUTOFF)

    def body(st):
        i, accs, carries, _ = st
        accs, carries = both_heads(qi - 1 - i, accs, carries, False)
        return i + 1, accs, carries, top_of(carries)

    _, accs, _, _ = lax.while_loop(cond, body, (jnp.int32(0), accs, carries, top_of(carries)))
    o_ref[...] = jnp.where(hmasks[0], accs[0], accs[1]).astype(o_ref.dtype)


def _sb_attention(sbp, tq):
    B, S, W3 = sbp.shape
    W = W3 // 3
    npair = W // LANES
    kern = functools.partial(_sb_kernel, tq=tq, scale=SB_HEAD_DIM ** -0.5)
    return pl.pallas_call(
        kern,
        grid=(B, npair, S // tq),
        in_specs=[pl.BlockSpec((None, tq, LANES), lambda b, p, i: (b, i, p)),
                  pl.BlockSpec((None, S, LANES), lambda b, p, i: (b, 0, npair + p)),
                  pl.BlockSpec((None, S, LANES), lambda b, p, i: (b, 0, 2 * npair + p))],
        out_specs=pl.BlockSpec((None, tq, LANES), lambda b, p, i: (b, i, p)),
        out_shape=jax.ShapeDtypeStruct((B, S, W), BF16),
        compiler_params=_cparams(("arbitrary", "arbitrary", "arbitrary")),
        name="sb_attention",
    )(sbp, sbp, sbp)


def _mlstm_kernel(qk_ref, vo_ref, g_ref, cw_ref, cb_ref, gb_ref, ng_ref, o_ref,
                  xbuf, ct_ref, m_ref, *, L):
    H, dk = ML_HEADS, ML_HEAD_DIM
    W = H * dk
    ci = pl.program_id(1)

    @pl.when(ci == 0)
    def _():
        xbuf[0:8, :] = jnp.zeros((8, 2 * W), F32)
        ct_ref[...] = jnp.zeros_like(ct_ref)
        m_ref[...] = jnp.zeros_like(m_ref)

    xbuf[8:8 + L, :] = qk_ref[...]
    y = cb_ref[...] + cw_ref[CONV_K - 1:CONV_K, :] * xbuf[8:8 + L, :]
    for j in range(1, CONV_K):
        y = y + cw_ref[CONV_K - 1 - j:CONV_K - j, :] * xbuf[8 - j:8 - j + L, :]
    xbuf[0:8, :] = xbuf[L:L + 8, :]
    qk = y * jax.nn.sigmoid(y)

    g = g_ref[...] + gb_ref[...]
    logf = jax.nn.log_sigmoid(g)
    r = lax.broadcasted_iota(jnp.int32, (L, L), 0)
    c = lax.broadcasted_iota(jnp.int32, (L, L), 1)
    causal = c <= r
    tri = causal.astype(BF16)
    lf_hi = logf.astype(BF16)
    lf_lo = (logf - lf_hi.astype(F32)).astype(BF16)
    bc = _dot(tri, lf_hi) + _dot(tri, lf_lo)
    g_t = g.T
    bc_t = bc.T
    e0 = (lax.broadcasted_iota(jnp.int32, (L, LANES), 1) == 0).astype(F32)

    for h in range(H):
        qh = qk[:, h * dk:(h + 1) * dk].astype(BF16)
        kh = (qk[:, W + h * dk:W + (h + 1) * dk] * (dk ** -0.5)).astype(BF16)
        vh = vo_ref[:, h * dk:(h + 1) * dk].astype(F32)
        oh = vo_ref[:, W + h * dk:W + (h + 1) * dk].astype(F32)
        vaug = jnp.concatenate([vh, e0], axis=1)
        ic_col = g[:, h:h + 1]
        ic_row = g_t[h:h + 1, :]
        bc_col = bc[:, H + h:H + h + 1]
        bc_row = bc_t[H + h:H + h + 1, :]
        m = m_ref[h][:, 0:1]
        ct = ct_ref[h]

        log_d = jnp.where(causal, bc_col - bc_row + ic_row, NEG_INF)
        inter = bc_col + m
        m_t = jnp.maximum(inter, jnp.max(log_d, axis=1, keepdims=True))
        w = _dot_nt(qh, kh) * jnp.exp(log_d - m_t)
        s_inter = jnp.exp(inter - m_t)
        tot = s_inter * _dot(qh, ct.astype(BF16)) + _dot(w.astype(BF16), vaug.astype(BF16))
        num = tot[:, :dk]
        den = tot[:, dk:dk + 1]
        hh = num / jnp.maximum(jnp.abs(den), jnp.exp(-m_t))

        b_last = bc_col[L - 1:L, :]
        log_w = b_last - bc_col + ic_col
        m_new = jnp.maximum(b_last + m, jnp.max(log_w, axis=0, keepdims=True))
        wk = jnp.exp(log_w - m_new)
        decay = jnp.exp(b_last + m - m_new)
        ct_ref[h] = decay * ct + _dot_tn(kh, (wk * vaug).astype(BF16))
        m_ref[h] = jnp.broadcast_to(m_new, (1, LANES))

        hn = _ln(hh) * ng_ref[:, h * dk:(h + 1) * dk]
        o_ref[:, h * dk:(h + 1) * dk] = (jax.nn.sigmoid(oh) * hn).astype(o_ref.dtype)


def _mlstm(mqk, mvo, gates, conv_w, conv_b, gate_b, norm_g, L):
    B, S, W2 = mqk.shape
    W = W2 // 2
    kern = functools.partial(_mlstm_kernel, L=L)
    row = lambda n: pl.BlockSpec((None, L, n), lambda b, i: (b, i, 0))
    cst = lambda a: pl.BlockSpec(a.shape, lambda b, i: (0, 0))
    return pl.pallas_call(
        kern,
        grid=(B, S // L),
        in_specs=[row(W2), row(W2), row(LANES), cst(conv_w), cst(conv_b), cst(gate_b), cst(norm_g)],
        out_specs=row(W),
        out_shape=jax.ShapeDtypeStruct((B, S, W), BF16),
        scratch_shapes=[pltpu.VMEM((L + 8, W2), F32),
                        pltpu.VMEM((ML_HEADS, ML_HEAD_DIM, 2 * ML_HEAD_DIM), F32),
                        pltpu.VMEM((ML_HEADS, 1, LANES), F32)],
        compiler_params=_cparams(("arbitrary", "arbitrary")),
        name="mlstm",
    )(mqk, mvo, gates, conv_w, conv_b, gate_b, norm_g)


def _split3(v):
    hi = v.astype(BF16)
    r = v - hi.astype(F32)
    mid = r.astype(BF16)
    lo = (r - mid.astype(F32)).astype(BF16)
    return hi, mid, lo


def _outproj_kernel(sb_ref, ml_ref, x_ref, mod_ref, wo1_ref, wo2_ref, lg_ref, lb_ref, wr_ref,
                    x1_ref, u2_ref, lo_ref):
    mix = _dot(sb_ref[...], wo1_ref[...]) + _dot(ml_ref[...], wo2_ref[...])
    g1 = mod_ref[2:3, :]
    x1 = _ln(DN_ALPHA * x_ref[...] + g1 * mix) * lg_ref[...] + lb_ref[...]
    x1_ref[...] = x1
    u2 = _ln(x1) * (1.0 + mod_ref[4:5, :]) + mod_ref[3:4, :]
    _store_token_rows(u2_ref, u2)
    u_hi, u_mid, u_lo = _split3(u2)
    w_hi, w_mid, w_lo = wr_ref[0], wr_ref[1], wr_ref[2]
    small = (_dot_nt(w_lo, u_hi) + _dot_nt(w_hi, u_lo)) + _dot_nt(w_mid, u_mid)
    lo_ref[...] = _dot_nt(w_hi, u_hi) + ((_dot_nt(w_mid, u_hi) + _dot_nt(w_hi, u_mid)) + small)


def _outproj(sb, ml, x, mod, wo1, wo2, ln_g, ln_b, wr_parts, tm):
    B, S, D = x.shape
    W = sb.shape[2]
    E = wr_parts.shape[1]
    nt = S // tm
    row = lambda n: pl.BlockSpec((None, tm, n), lambda b, i: (b, i, 0))
    cst = lambda a: pl.BlockSpec(a.shape, lambda b, i: (0, 0))
    return pl.pallas_call(
        _outproj_kernel,
        grid=(B, nt),
        in_specs=[row(W), row(W), row(D), pl.BlockSpec((None, 6, D), lambda b, i: (b, 0, 0)),
                  cst(wo1), cst(wo2), cst(ln_g), cst(ln_b),
                  pl.BlockSpec(wr_parts.shape, lambda b, i: (0, 0, 0))],
        out_specs=[row(D), pl.BlockSpec((2, None, tm, D // 4), lambda b, i: (0, b, i, 0)),
                   pl.BlockSpec((E, tm), lambda b, i: (0, b * nt + i))],
        out_shape=[jax.ShapeDtypeStruct((B, S, D), F32),
                   jax.ShapeDtypeStruct((2, B, S, D // 4), jnp.uint32),
                   jax.ShapeDtypeStruct((E, B * S), F32)],
        compiler_params=_cparams(("arbitrary", "arbitrary")),
        name="outproj",
    )(sb, ml, x, mod, wo1, wo2, ln_g, ln_b, wr_parts)


def _rows_to_block(rows, dtype):
    n = rows[0].shape[1]
    rid = lax.broadcasted_iota(jnp.int32, (len(rows), n), 0)
    out = jnp.zeros((len(rows), n), dtype)
    for k, v in enumerate(rows):
        out = jnp.where(rid == k, v.astype(dtype), out)
    return out


def _route_kernel(lo_ref, rb_ref, e_ref, r_ref, g_ref, cnt_ref, cnt_scr):
    E, tm = lo_ref.shape

    @pl.when(pl.program_id(0) == 0)
    def _():
        cnt_scr[...] = jnp.zeros_like(cnt_scr)

    scores = jax.nn.sigmoid(lo_ref[...])
    sel = scores + rb_ref[...]
    row_f = lax.broadcasted_iota(jnp.int32, (E, tm), 0).astype(F32)
    groups = [sel[g * GROUP_SIZE:(g + 1) * GROUP_SIZE, :] for g in range(N_GROUPS)]
    gscore = []
    for xg in groups:
        m1 = jnp.max(xg, axis=0, keepdims=True)
        is_max = xg == m1
        cnt = jnp.sum(is_max.astype(F32), axis=0, keepdims=True)
        m2 = jnp.max(jnp.where(is_max, NEG_INF, xg), axis=0, keepdims=True)
        gscore.append(m1 + jnp.where(cnt >= 2.0, m1, m2))
    kept = []
    for g in range(N_GROUPS):
        rank = jnp.zeros((1, tm), jnp.int32)
        for o in range(N_GROUPS):
            if o != g:
                beats = (gscore[o] >= gscore[g]) if o < g else (gscore[o] > gscore[g])
                rank = rank + beats.astype(jnp.int32)
        kept.append(jnp.where(rank < TOPK_GROUPS, groups[g], NEG_INF))
    cur = jnp.concatenate(kept, axis=0)
    idxs = []
    chosen = jnp.zeros((E, tm), F32)
    for k in range(TOP_K):
        m = jnp.max(cur, axis=0, keepdims=True)
        idx = jnp.min(jnp.where(cur == m, row_f, float(E)), axis=0, keepdims=True)
        pick = row_f == idx
        cur = jnp.where(pick, NEG_INF, cur)
        chosen = jnp.where(pick, 1.0, chosen)
        idxs.append(idx)
    r = lax.broadcasted_iota(jnp.int32, (tm, tm), 0)
    c = lax.broadcasted_iota(jnp.int32, (tm, tm), 1)
    before = _dot(chosen.astype(BF16), (r < c).astype(BF16)) + cnt_scr[...]
    gates, ranks = [], []
    for k in range(TOP_K):
        pick = row_f == idxs[k]
        gates.append(jnp.sum(jnp.where(pick, scores, 0.0), axis=0, keepdims=True))
        ranks.append(jnp.sum(jnp.where(pick, before, 0.0), axis=0, keepdims=True))
    gsum = gates[0]
    for gk in gates[1:]:
        gsum = gsum + gk
    e_ref[...] = _rows_to_block(idxs, jnp.int32)
    r_ref[...] = _rows_to_block(ranks, jnp.int32)
    g_ref[...] = _rows_to_block(gates, F32) / gsum * ROUTED_SCALE
    cnt_scr[...] = cnt_scr[...] + jnp.sum(chosen, axis=1, keepdims=True)
    cnt_ref[...] = cnt_scr[...]


def _route(logits_t, router_bias, tm):
    E, T = logits_t.shape
    tok = lambda dt: jax.ShapeDtypeStruct((TOP_K, T), dt)
    return pl.pallas_call(
        _route_kernel,
        grid=(T // tm,),
        in_specs=[pl.BlockSpec((E, tm), lambda i: (0, i)), pl.BlockSpec((E, 1), lambda i: (0, 0))],
        out_specs=[pl.BlockSpec((TOP_K, tm), lambda i: (0, i))] * 3 + [pl.BlockSpec((E, 1), lambda i: (0, 0))],
        out_shape=[tok(jnp.int32), tok(jnp.int32), tok(F32), jax.ShapeDtypeStruct((E, 1), F32)],
        scratch_shapes=[pltpu.VMEM((E, 1), F32)],
        compiler_params=_cparams(("arbitrary",)),
        name="route",
    )(logits_t, router_bias.reshape(E, 1))


def _dest_kernel(e_ref, r_ref, off_ref, d_ref):
    E = off_ref.shape[0]
    tm = e_ref.shape[1]
    row = lax.broadcasted_iota(jnp.int32, (E, tm), 0)
    off = off_ref[...]
    base = [jnp.sum(jnp.where(row == e_ref[k:k + 1, :], off, 0.0), axis=0, keepdims=True) for k in range(TOP_K)]
    d_ref[...] = _rows_to_block(base, jnp.int32) + r_ref[...]


def _dest(e_t, r_t, offsets, tm):
    K, T = e_t.shape
    E = offsets.shape[0]
    tok = pl.BlockSpec((K, tm), lambda i: (0, i))
    return pl.pallas_call(
        _dest_kernel,
        grid=(T // tm,),
        in_specs=[tok, tok, pl.BlockSpec((E, 1), lambda i: (0, 0))],
        out_specs=tok,
        out_shape=jax.ShapeDtypeStruct((K, T), jnp.int32),
        compiler_params=_cparams(("arbitrary",)),
        name="dest",
    )(e_t, r_t, offsets)


def _sc_mesh():
    return plsc.VectorSubcoreMesh(core_axis_name="core", subcore_axis_name="subcore")


def _sc_dispatch(u2p, dest_t, nrows):
    T, W = u2p.shape
    K = dest_t.shape[0]
    win = SC_WINDOW

    @pl.kernel(out_type=jax.ShapeDtypeStruct((nrows, W), u2p.dtype), mesh=_sc_mesh(), scratch_types=[])
    def scatter_rows(x_hbm, i_hbm, o_hbm):
        def body(x_vmem, i_vmem):
            for k in range(K):
                pltpu.sync_copy(x_vmem, o_hbm.at[i_vmem.at[k]])

        pltpu.emit_pipeline(
            body,
            grid=(T // win,),
            in_specs=[pl.BlockSpec((win, W), lambda i: (i, 0)),
                      pl.BlockSpec((K, win), lambda i: (0, i))],
            out_specs=[],
            core_axis_name=("core", "subcore"),
            dimension_semantics=(pltpu.PARALLEL,),
        )(x_hbm, i_hbm)

    return scatter_rows(u2p, dest_t)


def _sc_gather(ys, dest_flat):
    W = ys.shape[1]
    n = dest_flat.shape[1]
    win = SC_WINDOW

    @pl.kernel(out_type=jax.ShapeDtypeStruct((n, W), ys.dtype), mesh=_sc_mesh(), scratch_types=[])
    def gather_rows(y_hbm, i_hbm, o_hbm):
        def body(i_vmem, o_vmem):
            pltpu.sync_copy(y_hbm.at[i_vmem.at[0]], o_vmem)

        pltpu.emit_pipeline(
            body,
            grid=(n // win,),
            in_specs=[pl.BlockSpec((1, win), lambda i: (0, i))],
            out_specs=[pl.BlockSpec((win, W), lambda i: (i, 0))],
            core_axis_name=("core", "subcore"),
            dimension_semantics=(pltpu.PARALLEL,),
        )(i_hbm, o_hbm)

    return gather_rows(ys, dest_flat)


def _expert_kernel(be_ref, bv_ref, nb_ref, first_ref, slot_ref, next_ref, xs_ref, w1_hbm, w3_hbm, w2_hbm,
                   y_ref, w1f, w3f, w2f, w1b, w3b, w2b, sem):
    i = pl.program_id(0)

    def weight_copies(e, s):
        return [pltpu.make_async_copy(w1_hbm.at[e], w1f.at[s], sem.at[s]),
                pltpu.make_async_copy(w3_hbm.at[e], w3f.at[s], sem.at[s]),
                pltpu.make_async_copy(w2_hbm.at[e], w2f.at[s], sem.at[s])]

    @pl.when(i == 0)
    def _():
        for cp in weight_copies(be_ref[0], 0):
            cp.start()

    @pl.when(jnp.logical_and(i < nb_ref[0], first_ref[i] == 1))
    def _():
        s = slot_ref[i]
        for cp in weight_copies(be_ref[i], s):
            cp.wait()

        @pl.when(next_ref[i] >= 0)
        def _():
            for cp in weight_copies(next_ref[i], 1 - s):
                cp.start()

        w1b[...] = w1f[s].astype(BF16)
        w3b[...] = w3f[s].astype(BF16)
        w2b[...] = w2f[s].astype(BF16)

    @pl.when(i < nb_ref[0])
    def _():
        xb = _load_token_rows(xs_ref, valid=bv_ref[i])
        a = _dot_blocks(xb, w1b)
        b = _dot_blocks(xb, w3b)
        hmid = (a * jax.nn.sigmoid(a) * b).astype(BF16)
        _store_token_rows(y_ref, _dot(hmid, w2b[...]))


def _experts(block_e, block_valid, nb_used, xs, w1, w3, w2, rb):
    _, nrows, W = xs.shape
    nb = nrows // rb
    E, D, FF = w1.shape
    idx = jnp.arange(nb, dtype=jnp.int32)
    first = (idx < nb_used[0]) & (block_e != jnp.concatenate([jnp.full((1,), -1, jnp.int32), block_e[:-1]]))
    slot = (jnp.cumsum(first.astype(jnp.int32)) - 1) % 2
    first_pos = jnp.where(first, idx, nb)
    next_pos = jnp.concatenate([lax.cummin(first_pos, reverse=True)[1:], jnp.full((1,), nb, jnp.int32)])
    next_e = jnp.where(next_pos < nb, block_e[jnp.minimum(next_pos, nb - 1)], -1).astype(jnp.int32)
    rows = pl.BlockSpec((2, rb, W), lambda i, be, bv, n, fi, sl, nx: (0, jnp.minimum(i, n[0] - 1), 0))
    grid_spec = pltpu.PrefetchScalarGridSpec(
        num_scalar_prefetch=6,
        grid=(nb,),
        in_specs=[rows, pl.BlockSpec(memory_space=pl.ANY), pl.BlockSpec(memory_space=pl.ANY),
                  pl.BlockSpec(memory_space=pl.ANY)],
        out_specs=rows,
        scratch_shapes=[pltpu.VMEM((2, D, FF), F32), pltpu.VMEM((2, D, FF), F32), pltpu.VMEM((2, FF, D), F32),
                        pltpu.VMEM((D, FF), BF16), pltpu.VMEM((D, FF), BF16), pltpu.VMEM((FF, D), BF16),
                        pltpu.SemaphoreType.DMA((2,))],
    )
    return pl.pallas_call(
        _expert_kernel,
        grid_spec=grid_spec,
        out_shape=jax.ShapeDtypeStruct((2, nrows, W), jnp.uint32),
        compiler_params=_cparams(("arbitrary",)),
        name="experts",
    )(block_e, block_valid, nb_used, first.astype(jnp.int32), slot.astype(jnp.int32), next_e, xs, w1, w3, w2)


def _combine_kernel(yg_ref, gw_ref, u_ref, x1_ref, mod_ref, s1_ref, s3_ref, s2_ref, lg_ref, lb_ref, o_ref):
    ub = _load_token_rows(u_ref)
    a = _dot_blocks(ub, s1_ref)
    b = _dot_blocks(ub, s3_ref)
    ffn = _dot((a * jax.nn.sigmoid(a) * b).astype(BF16), s2_ref[...])
    gw = gw_ref[...]
    routed = None
    for k in range(TOP_K):
        yk = [gw[:, k:k + 1] * blk.astype(F32) for blk in _load_token_rows(yg_ref.at[k])]
        routed = yk if routed is None else [r + y for r, y in zip(routed, yk)]
    ffn = ffn + jnp.concatenate(routed, axis=1)
    g2 = mod_ref[5:6, :]
    o_ref[...] = _ln(DN_ALPHA * x1_ref[...] + g2 * ffn) * lg_ref[...] + lb_ref[...]


def _combine(yg, gw, u2p, x1, mod, s1, s3, s2, ln_g, ln_b, tm):
    B, S, D = x1.shape
    K = yg.shape[0]
    row = lambda n: pl.BlockSpec((None, tm, n), lambda b, i: (b, i, 0))
    cst = lambda a: pl.BlockSpec(a.shape, lambda b, i: (0, 0))
    return pl.pallas_call(
        _combine_kernel,
        grid=(B, S // tm),
        in_specs=[pl.BlockSpec((K, 2, None, tm, D // 4), lambda b, i: (0, 0, b, i, 0)),
                  row(K), pl.BlockSpec((2, None, tm, D // 4), lambda b, i: (0, b, i, 0)), row(D),
                  pl.BlockSpec((None, 6, D), lambda b, i: (b, 0, 0)),
                  cst(s1), cst(s3), cst(s2), cst(ln_g), cst(ln_b)],
        out_specs=row(D),
        out_shape=jax.ShapeDtypeStruct((B, S, D), F32),
        compiler_params=_cparams(("arbitrary", "arbitrary")),
        name="combine",
    )(yg, gw, u2p, x1, mod, s1, s3, s2, ln_g, ln_b)


def _block_layout(counts, T, rb):
    counts = counts.reshape(-1).astype(jnp.int32)
    pcounts = (counts + rb - 1) // rb * rb
    pend = jnp.cumsum(pcounts)
    poffsets = pend - pcounts
    nb = (T * TOP_K) // rb + N_EXPERTS
    starts = jnp.arange(nb, dtype=jnp.int32) * rb
    block_e = jnp.minimum(jnp.sum(starts[:, None] >= pend[None, :], axis=1), N_EXPERTS - 1).astype(jnp.int32)
    nb_used = (pend[-1] // rb).astype(jnp.int32).reshape(1)
    mine = block_e[:, None] == jnp.arange(N_EXPERTS, dtype=jnp.int32)[None, :]
    seg_start = jnp.sum(jnp.where(mine, poffsets[None, :], 0), axis=1)
    seg_count = jnp.sum(jnp.where(mine, counts[None, :], 0), axis=1)
    block_valid = jnp.clip(seg_count - (starts - seg_start), 0, rb).astype(jnp.int32)
    return poffsets, block_e, block_valid, nb_used, nb * rb


def kernel(x, c, w_ada, b_ada, w_in, ml_conv_w, ml_conv_b, ml_gate_b, ml_norm_g, w_out, ln1_g, ln1_b,
           w_router, router_bias, moe_w1, moe_w3, moe_w2, sh_w1, sh_w3, sh_w2, ln2_g, ln2_b):
    B, S, D = x.shape
    T = B * S
    SBW = SB_HEADS * SB_HEAD_DIM
    MLW = ML_HEADS * ML_HEAD_DIM
    rb = EXPERT_BLOCK_ROWS
    for l in range(DEPTH):
        mod = _adaln(c, w_ada[l], b_ada[l]).reshape(B, 6, D)

        wi = w_in[l]
        c0 = 3 * SBW
        w_sb = wi[:, :c0].astype(BF16)
        w_mqk = wi[:, c0:c0 + 2 * MLW].astype(BF16)
        w_mvo = wi[:, c0 + 2 * MLW:c0 + 4 * MLW].astype(BF16)
        w_g = jnp.pad(wi[:, c0 + 4 * MLW:], ((0, 0), (0, LANES - 2 * ML_HEADS))).astype(BF16)
        sbp, mqk, mvo, gates = _inproj(x, mod, w_sb, w_mqk, w_mvo, w_g, tm=min(512, S))

        sb = _sb_attention(sbp, tq=min(256, S))

        gate_b = jnp.pad(ml_gate_b[l], (0, LANES - 2 * ML_HEADS)).reshape(1, LANES)
        ml = _mlstm(mqk, mvo, gates, ml_conv_w[l], ml_conv_b[l].reshape(1, -1), gate_b,
                    ml_norm_g[l].reshape(1, -1), L=min(256, S))

        wo = w_out[l].astype(BF16)
        wr_parts = jnp.stack(_split3(w_router[l].T))
        x1, u2p, logits_t = _outproj(sb, ml, x, mod, wo[:SBW], wo[SBW:], ln1_g[l].reshape(1, D),
                                     ln1_b[l].reshape(1, D), wr_parts, tm=min(512, S))

        e_t, r_t, g_t, counts = _route(logits_t, router_bias[l], tm=min(512, T))
        poffsets, block_e, block_valid, nb_used, nrows = _block_layout(counts, T, rb)
        dest_t = _dest(e_t, r_t, poffsets.astype(F32).reshape(N_EXPERTS, 1), tm=min(512, T))
        gw = g_t.T
        idx2 = jnp.concatenate([dest_t, dest_t + nrows], axis=1)
        xs = _sc_dispatch(u2p.reshape(2 * T, D // 4), idx2, 2 * nrows).reshape(2, nrows, D // 4)
        ys = _experts(block_e, block_valid, nb_used, xs, moe_w1[l], moe_w3[l], moe_w2[l], rb)
        yg = _sc_gather(ys.reshape(2 * nrows, D // 4), idx2.reshape(1, TOP_K * 2 * T))
        yg = yg.reshape(TOP_K, 2, B, S, D // 4)
        x = _combine(yg, gw.reshape(B, S, TOP_K), u2p, x1, mod, sh_w1[l].astype(BF16),
                     sh_w3[l].astype(BF16), sh_w2[l].astype(BF16), ln2_g[l].reshape(1, D),
                     ln2_b[l].reshape(1, D), tm=min(256, S))
    return x
```

```python
import functools
import math

import jax
import jax.numpy as jnp
from jax import lax
from jax.experimental import pallas as pl
from jax.experimental.pallas import tpu as pltpu
from jax.experimental.pallas import tpu_sc as plsc

F32 = jnp.float32
BF16 = jnp.bfloat16
HIGHEST = lax.Precision.HIGHEST

SB_HEADS = 8
SB_HEAD_DIM = 64
ML_HEADS = 4
ML_HEAD_DIM = 128
CONV_K = 4
N_EXPERTS = 256
TOP_K = 8
N_GROUPS = 8
TOPK_GROUPS = 4
GROUP_SIZE = N_EXPERTS // N_GROUPS
ROUTED_SCALE = 2.5
EXPERT_BLOCK_ROWS = 512
SC_WINDOW = 128
DEPTH = 1
DN_ALPHA = (2 * DEPTH) ** 0.25
LN_EPS = 1e-5
LANES = 128
SUBLANES = 8
NEG_INF = float("-inf")
SB_CUTOFF = 104.0
VMEM_LIMIT = 56 * 1024 * 1024


def _cparams(sem):
    return pltpu.CompilerParams(dimension_semantics=sem, vmem_limit_bytes=VMEM_LIMIT)


def _ln(x):
    mu = jnp.mean(x, axis=-1, keepdims=True)
    xc = x - mu
    var = jnp.mean(xc * xc, axis=-1, keepdims=True)
    return xc * lax.rsqrt(var + LN_EPS)


def _dot(a, b):
    return jnp.dot(a, b, preferred_element_type=F32)


def _dot_nt(a, b):
    return lax.dot_general(a, b, (((1,), (1,)), ((), ())), preferred_element_type=F32)


def _dot_tn(a, b):
    return lax.dot_general(a, b, (((0,), (0,)), ((), ())), preferred_element_type=F32)


def _pack_halves(v):
    w = v.shape[1] // 2
    lo = lax.bitcast_convert_type(v[:, :w].astype(BF16).astype(F32), jnp.uint32) >> 16
    hi = lax.bitcast_convert_type(v[:, w:].astype(BF16).astype(F32), jnp.uint32) & jnp.uint32(0xFFFF0000)
    return hi | lo


def _unpack_halves(p):
    lo = lax.bitcast_convert_type(p << 16, F32).astype(BF16)
    hi = lax.bitcast_convert_type(p & jnp.uint32(0xFFFF0000), F32).astype(BF16)
    return lo, hi


def _store_token_rows(ref, v):
    p = _pack_halves(v)
    q = p.shape[1] // 2
    ref[0] = p[:, :q]
    ref[1] = p[:, q:]


def _load_token_rows(ref, valid=None):
    first, second = ref[0], ref[1]
    if valid is not None:
        row = lax.broadcasted_iota(jnp.int32, first.shape, 0)
        first = jnp.where(row < valid, first, jnp.uint32(0))
        second = jnp.where(row < valid, second, jnp.uint32(0))
    lo_a, hi_a = _unpack_halves(first)
    lo_b, hi_b = _unpack_halves(second)
    return [lo_a, lo_b, hi_a, hi_b]


def _dot_blocks(blocks, w_ref):
    q = blocks[0].shape[1]
    acc = _dot(blocks[0], w_ref[0:q, :])
    for i in range(1, len(blocks)):
        acc = acc + _dot(blocks[i], w_ref[i * q:(i + 1) * q, :])
    return acc


def _adaln_kernel(c_ref, w_ref, b_ref, o_ref):
    c = c_ref[...]
    s = c * jax.nn.sigmoid(c)
    o_ref[...] = jnp.dot(s, w_ref[...], preferred_element_type=F32, precision=HIGHEST) + b_ref[...]


def _adaln(c, w_ada, b_ada):
    B, D = c.shape
    N = w_ada.shape[1]
    tn = 1024
    return pl.pallas_call(
        _adaln_kernel,
        grid=(N // tn,),
        in_specs=[pl.BlockSpec((B, D), lambda j: (0, 0)),
                  pl.BlockSpec((D, tn), lambda j: (0, j)),
                  pl.BlockSpec((1, tn), lambda j: (0, j))],
        out_specs=pl.BlockSpec((B, tn), lambda j: (0, j)),
        out_shape=jax.ShapeDtypeStruct((B, N), F32),
        compiler_params=_cparams(("arbitrary",)),
        name="adaln",
    )(c, w_ada, b_ada.reshape(1, N))


def _inproj_kernel(x_ref, mod_ref, wsb_ref, wqk_ref, wvo_ref, wg_ref,
                   sb_ref, mqk_ref, mvo_ref, g_ref):
    y = _ln(x_ref[...])
    sh = mod_ref[0:1, :]
    sc = mod_ref[1:2, :]
    u = (y * (1.0 + sc) + sh).astype(BF16)
    sb_ref[...] = _dot(u, wsb_ref[...]).astype(BF16)
    mqk_ref[...] = _dot(u, wqk_ref[...])
    mvo_ref[...] = _dot(u, wvo_ref[...]).astype(BF16)
    g_ref[...] = _dot(u, wg_ref[...])


def _inproj(x, mod, w_sb, w_mqk, w_mvo, w_g, tm):
    B, S, D = x.shape
    nsb, nqk, nvo, ng = w_sb.shape[1], w_mqk.shape[1], w_mvo.shape[1], w_g.shape[1]
    row = lambda n: pl.BlockSpec((None, tm, n), lambda b, i: (b, i, 0))
    full = lambda n: pl.BlockSpec((D, n), lambda b, i: (0, 0))
    return pl.pallas_call(
        _inproj_kernel,
        grid=(B, S // tm),
        in_specs=[row(D), pl.BlockSpec((None, 6, D), lambda b, i: (b, 0, 0)),
                  full(nsb), full(nqk), full(nvo), full(ng)],
        out_specs=[row(nsb), row(nqk), row(nvo), row(ng)],
        out_shape=[jax.ShapeDtypeStruct((B, S, nsb), BF16),
                   jax.ShapeDtypeStruct((B, S, nqk), F32),
                   jax.ShapeDtypeStruct((B, S, nvo), BF16),
                   jax.ShapeDtypeStruct((B, S, ng), F32)],
        compiler_params=_cparams(("arbitrary", "arbitrary")),
        name="inproj",
    )(x, mod, w_sb, w_mqk, w_mvo, w_g)


def _sb_block(z):
    sp = jnp.log(1.0 + jnp.exp(-jnp.abs(z)))
    log_beta = jnp.minimum(z, 0.0) - sp
    return log_beta, log_beta - z


def _sb_kernel(q_ref, k_ref, v_ref, o_ref, *, tq, scale):
    qi = pl.program_id(2)
    q2 = q_ref[...]
    lane = lax.broadcasted_iota(jnp.int32, (1, LANES), 1)
    r = lax.broadcasted_iota(jnp.int32, (tq, tq), 0)
    c = lax.broadcasted_iota(jnp.int32, (tq, tq), 1)
    upper = (r > c).astype(BF16)
    strict = c < r
    hmasks = [(lane // SB_HEAD_DIM) == h for h in range(2)]
    qs = [jnp.where(m, q2 * scale, jnp.zeros_like(q2)) for m in hmasks]

    def block(qh, kblk, vblk, carry, masked):
        z = _dot_nt(qh, kblk)
        log_beta, log_1m = _sb_block(z)
        if masked:
            log_1m = jnp.where(strict, log_1m, 0.0)
        after = _dot(log_1m.astype(BF16), upper)
        a = jnp.exp(log_beta + after + carry)
        if masked:
            a = jnp.where(strict, a, 0.0)
        pv = _dot(a.astype(BF16), vblk)
        return pv, carry + jnp.sum(log_1m, axis=1, keepdims=True)

    def both_heads(kb, accs, carries, masked):
        off = pl.multiple_of(kb * tq, tq)
        kblk = k_ref[pl.ds(off, tq), :]
        vblk = v_ref[pl.ds(off, tq), :]
        res = [block(qs[h], kblk, vblk, carries[h], masked) for h in range(2)]
        return [accs[h] + res[h][0] for h in range(2)], [res[h][1] for h in range(2)]

    zero = jnp.zeros((tq, LANES), F32)
    accs, carries = both_heads(qi, [zero, zero], [jnp.zeros((tq, 1), F32)] * 2, True)

    def top_of(carries):
        return jnp.max(jnp.maximum(carries[0], carries[1]))

    def cond(st):
        i, _, _, top = st
        return jnp.logical_and(i < qi, top > -SB_CUTOFF)

    def body(st):
        i, accs, carries, _ = st
        accs, carries = both_heads(qi - 1 - i, accs, carries, False)
        return i + 1, accs, carries, top_of(carries)

    _, accs, _, _ = lax.while_loop(cond, body, (jnp.int32(0), accs, carries, top_of(carries)))
    o_ref[...] = jnp.where(hmasks[0], accs[0], accs[1]).astype(o_ref.dtype)


def _sb_attention(sbp, tq):
    B, S, W3 = sbp.shape
    W = W3 // 3
    npair = W // LANES
    kern = functools.partial(_sb_kernel, tq=tq, scale=SB_HEAD_DIM ** -0.5)
    return pl.pallas_call(
        kern,
        grid=(B, npair, S // tq),
        in_specs=[pl.BlockSpec((None, tq, LANES), lambda b, p, i: (b, i, p)),
                  pl.BlockSpec((None, S, LANES), lambda b, p, i: (b, 0, npair + p)),
                  pl.BlockSpec((None, S, LANES), lambda b, p, i: (b, 0, 2 * npair + p))],
        out_specs=pl.BlockSpec((None, tq, LANES), lambda b, p, i: (b, i, p)),
        out_shape=jax.ShapeDtypeStruct((B, S, W), BF16),
        compiler_params=_cparams(("arbitrary", "arbitrary", "arbitrary")),
        name="sb_attention",
    )(sbp, sbp, sbp)


def _mlstm_kernel(qk_ref, vo_ref, g_ref, cw_ref, cb_ref, gb_ref, ng_ref, o_ref,
                  xbuf, ct_ref, m_ref, *, L):
    H, dk = ML_HEADS, ML_HEAD_DIM
    W = H * dk
    ci = pl.program_id(1)

    @pl.when(ci == 0)
    def _():
        xbuf[0:8, :] = jnp.zeros((8, 2 * W), F32)
        ct_ref[...] = jnp.zeros_like(ct_ref)
        m_ref[...] = jnp.zeros_like(m_ref)

    xbuf[8:8 + L, :] = qk_ref[...]
    y = cb_ref[...] + cw_ref[CONV_K - 1:CONV_K, :] * xbuf[8:8 + L, :]
    for j in range(1, CONV_K):
        y = y + cw_ref[CONV_K - 1 - j:CONV_K - j, :] * xbuf[8 - j:8 - j + L, :]
    xbuf[0:8, :] = xbuf[L:L + 8, :]
    qk = y * jax.nn.sigmoid(y)

    g = g_ref[...] + gb_ref[...]
    logf = jax.nn.log_sigmoid(g)
    r = lax.broadcasted_iota(jnp.int32, (L, L), 0)
    c = lax.broadcasted_iota(jnp.int32, (L, L), 1)
    causal = c <= r
    tri = causal.astype(BF16)
    lf_hi = logf.astype(BF16)
    lf_lo = (logf - lf_hi.astype(F32)).astype(BF16)
    bc = _dot(tri, lf_hi) + _dot(tri, lf_lo)
    g_t = g.T
    bc_t = bc.T
    e0 = (lax.broadcasted_iota(jnp.int32, (L, LANES), 1) == 0).astype(F32)

    for h in range(H):
        qh = qk[:, h * dk:(h + 1) * dk].astype(BF16)
        kh = (qk[:, W + h * dk:W + (h + 1) * dk] * (dk ** -0.5)).astype(BF16)
        vh = vo_ref[:, h * dk:(h + 1) * dk].astype(F32)
        oh = vo_ref[:, W + h * dk:W + (h + 1) * dk].astype(F32)
        vaug = jnp.concatenate([vh, e0], axis=1)
        ic_col = g[:, h:h + 1]
        ic_row = g_t[h:h + 1, :]
        bc_col = bc[:, H + h:H + h + 1]
        bc_row = bc_t[H + h:H + h + 1, :]
        m = m_ref[h][:, 0:1]
        ct = ct_ref[h]

        log_d = jnp.where(causal, bc_col - bc_row + ic_row, NEG_INF)
        inter = bc_col + m
        m_t = jnp.maximum(inter, jnp.max(log_d, axis=1, keepdims=True))
        w = _dot_nt(qh, kh) * jnp.exp(log_d - m_t)
        s_inter = jnp.exp(inter - m_t)
        tot = s_inter * _dot(qh, ct.astype(BF16)) + _dot(w.astype(BF16), vaug.astype(BF16))
        num = tot[:, :dk]
        den = tot[:, dk:dk + 1]
        hh = num / jnp.maximum(jnp.abs(den), jnp.exp(-m_t))

        b_last = bc_col[L - 1:L, :]
        log_w = b_last - bc_col + ic_col
        m_new = jnp.maximum(b_last + m, jnp.max(log_w, axis=0, keepdims=True))
        wk = jnp.exp(log_w - m_new)
        decay = jnp.exp(b_last + m - m_new)
        ct_ref[h] = decay * ct + _dot_tn(kh, (wk * vaug).astype(BF16))
        m_ref[h] = jnp.broadcast_to(m_new, (1, LANES))

        hn = _ln(hh) * ng_ref[:, h * dk:(h + 1) * dk]
        o_ref[:, h * dk:(h + 1) * dk] = (jax.nn.sigmoid(oh) * hn).astype(o_ref.dtype)


def _mlstm(mqk, mvo, gates, conv_w, conv_b, gate_b, norm_g, L):
    B, S, W2 = mqk.shape
    W = W2 // 2
    kern = functools.partial(_mlstm_kernel, L=L)
    row = lambda n: pl.BlockSpec((None, L, n), lambda b, i: (b, i, 0))
    cst = lambda a: pl.BlockSpec(a.shape, lambda b, i: (0, 0))
    return pl.pallas_call(
        kern,
        grid=(B, S // L),
        in_specs=[row(W2), row(W2), row(LANES), cst(conv_w), cst(conv_b), cst(gate_b), cst(norm_g)],
        out_specs=row(W),
        out_shape=jax.ShapeDtypeStruct((B, S, W), BF16),
        scratch_shapes=[pltpu.VMEM((L + 8, W2), F32),
                        pltpu.VMEM((ML_HEADS, ML_HEAD_DIM, 2 * ML_HEAD_DIM), F32),
                        pltpu.VMEM((ML_HEADS, 1, LANES), F32)],
        compiler_params=_cparams(("arbitrary", "arbitrary")),
        name="mlstm",
    )(mqk, mvo, gates, conv_w, conv_b, gate_b, norm_g)


def _keep_bf16_bits(v):
    bits = lax.bitcast_convert_type(v, jnp.uint32) & jnp.uint32(0xFFFF0000)
    return lax.bitcast_convert_type(bits, F32)


def _split3(v):
    hi = _keep_bf16_bits(v)
    r = v - hi
    mid = _keep_bf16_bits(r)
    lo = r - mid
    return hi.astype(BF16), mid.astype(BF16), lo.astype(BF16)


def _outproj_kernel(sb_ref, ml_ref, x_ref, mod_ref, wo1_ref, wo2_ref, lg_ref, lb_ref, wr_ref,
                    x1_ref, u2_ref, lo_ref):
    mix = _dot(sb_ref[...], wo1_ref[...]) + _dot(ml_ref[...], wo2_ref[...])
    g1 = mod_ref[2:3, :]
    x1 = _ln(DN_ALPHA * x_ref[...] + g1 * mix) * lg_ref[...] + lb_ref[...]
    x1_ref[...] = x1
    u2 = _ln(x1) * (1.0 + mod_ref[4:5, :]) + mod_ref[3:4, :]
    _store_token_rows(u2_ref, u2)
    u_hi, u_mid, _ = _split3(u2)
    w_hi, w_mid = wr_ref[0], wr_ref[1]
    lo_ref[...] = _dot_nt(w_hi, u_hi) + (_dot_nt(w_mid, u_hi) + _dot_nt(w_hi, u_mid))


def _outproj(sb, ml, x, mod, wo1, wo2, ln_g, ln_b, wr_parts, tm):
    B, S, D = x.shape
    W = sb.shape[2]
    E = wr_parts.shape[1]
    nt = S // tm
    row = lambda n: pl.BlockSpec((None, tm, n), lambda b, i: (b, i, 0))
    cst = lambda a: pl.BlockSpec(a.shape, lambda b, i: (0, 0))
    return pl.pallas_call(
        _outproj_kernel,
        grid=(B, nt),
        in_specs=[row(W), row(W), row(D), pl.BlockSpec((None, 6, D), lambda b, i: (b, 0, 0)),
                  cst(wo1), cst(wo2), cst(ln_g), cst(ln_b),
                  pl.BlockSpec(wr_parts.shape, lambda b, i: (0, 0, 0))],
        out_specs=[row(D), pl.BlockSpec((2, None, tm, D // 4), lambda b, i: (0, b, i, 0)),
                   pl.BlockSpec((E, tm), lambda b, i: (0, b * nt + i))],
        out_shape=[jax.ShapeDtypeStruct((B, S, D), F32),
                   jax.ShapeDtypeStruct((2, B, S, D // 4), jnp.uint32),
                   jax.ShapeDtypeStruct((E, B * S), F32)],
        compiler_params=_cparams(("arbitrary", "arbitrary")),
        name="outproj",
    )(sb, ml, x, mod, wo1, wo2, ln_g, ln_b, wr_parts)


def _rows_to_block(rows, dtype):
    n = rows[0].shape[1]
    rid = lax.broadcasted_iota(jnp.int32, (len(rows), n), 0)
    out = jnp.zeros((len(rows), n), dtype)
    for k, v in enumerate(rows):
        out = jnp.where(rid == k, v.astype(dtype), out)
    return out


def _route_kernel(lo_ref, rb_ref, e_ref, r_ref, g_ref, cnt_ref, cnt_scr):
    E, tm = lo_ref.shape

    @pl.when(pl.program_id(0) == 0)
    def _():
        cnt_scr[...] = jnp.zeros_like(cnt_scr)

    scores = jax.nn.sigmoid(lo_ref[...])
    sel = scores + rb_ref[...]
    row_f = lax.broadcasted_iota(jnp.int32, (E, tm), 0).astype(F32)
    groups = [sel[g * GROUP_SIZE:(g + 1) * GROUP_SIZE, :] for g in range(N_GROUPS)]
    gscore = []
    for xg in groups:
        m1 = jnp.max(xg, axis=0, keepdims=True)
        is_max = xg == m1
        cnt = jnp.sum(is_max.astype(F32), axis=0, keepdims=True)
        m2 = jnp.max(jnp.where(is_max, NEG_INF, xg), axis=0, keepdims=True)
        gscore.append(m1 + jnp.where(cnt >= 2.0, m1, m2))
    kept = []
    for g in range(N_GROUPS):
        rank = jnp.zeros((1, tm), jnp.int32)
        for o in range(N_GROUPS):
            if o != g:
                beats = (gscore[o] >= gscore[g]) if o < g else (gscore[o] > gscore[g])
                rank = rank + beats.astype(jnp.int32)
        kept.append(jnp.where(rank < TOPK_GROUPS, groups[g], NEG_INF))
    cur = jnp.concatenate(kept, axis=0)
    idxs = []
    chosen = jnp.zeros((E, tm), F32)
    for k in range(TOP_K):
        m = jnp.max(cur, axis=0, keepdims=True)
        idx = jnp.min(jnp.where(cur == m, row_f, float(E)), axis=0, keepdims=True)
        pick = row_f == idx
        cur = jnp.where(pick, NEG_INF, cur)
        chosen = jnp.where(pick, 1.0, chosen)
        idxs.append(idx)
    r = lax.broadcasted_iota(jnp.int32, (tm, tm), 0)
    c = lax.broadcasted_iota(jnp.int32, (tm, tm), 1)
    before = _dot(chosen.astype(BF16), (r < c).astype(BF16)) + cnt_scr[...]
    gates, ranks = [], []
    for k in range(TOP_K):
        pick = row_f == idxs[k]
        gates.append(jnp.sum(jnp.where(pick, scores, 0.0), axis=0, keepdims=True))
        ranks.append(jnp.sum(jnp.where(pick, before, 0.0), axis=0, keepdims=True))
    gsum = gates[0]
    for gk in gates[1:]:
        gsum = gsum + gk
    e_ref[...] = _rows_to_block(idxs, jnp.int32)
    r_ref[...] = _rows_to_block(ranks, jnp.int32)
    g_ref[...] = _rows_to_block(gates, F32) / gsum * ROUTED_SCALE
    cnt_scr[...] = cnt_scr[...] + jnp.sum(chosen, axis=1, keepdims=True)
    cnt_ref[...] = cnt_scr[...]


def _route(logits_t, router_bias, tm):
    E, T = logits_t.shape
    tok = lambda dt: jax.ShapeDtypeStruct((TOP_K, T), dt)
    return pl.pallas_call(
        _route_kernel,
        grid=(T // tm,),
        in_specs=[pl.BlockSpec((E, tm), lambda i: (0, i)), pl.BlockSpec((E, 1), lambda i: (0, 0))],
        out_specs=[pl.BlockSpec((TOP_K, tm), lambda i: (0, i))] * 3 + [pl.BlockSpec((E, 1), lambda i: (0, 0))],
        out_shape=[tok(jnp.int32), tok(jnp.int32), tok(F32), jax.ShapeDtypeStruct((E, 1), F32)],
        scratch_shapes=[pltpu.VMEM((E, 1), F32)],
        compiler_params=_cparams(("arbitrary",)),
        name="route",
    )(logits_t, router_bias.reshape(E, 1))


def _dest_kernel(e_ref, r_ref, off_ref, d_ref):
    E = off_ref.shape[0]
    tm = e_ref.shape[1]
    row = lax.broadcasted_iota(jnp.int32, (E, tm), 0)
    off = off_ref[...]
    base = [jnp.sum(jnp.where(row == e_ref[k:k + 1, :], off, 0.0), axis=0, keepdims=True) for k in range(TOP_K)]
    d_ref[...] = _rows_to_block(base, jnp.int32) + r_ref[...]


def _dest(e_t, r_t, offsets, tm):
    K, T = e_t.shape
    E = offsets.shape[0]
    tok = pl.BlockSpec((K, tm), lambda i: (0, i))
    return pl.pallas_call(
        _dest_kernel,
        grid=(T // tm,),
        in_specs=[tok, tok, pl.BlockSpec((E, 1), lambda i: (0, 0))],
        out_specs=tok,
        out_shape=jax.ShapeDtypeStruct((K, T), jnp.int32),
        compiler_params=_cparams(("arbitrary",)),
        name="dest",
    )(e_t, r_t, offsets)


def _sc_mesh():
    return plsc.VectorSubcoreMesh(core_axis_name="core", subcore_axis_name="subcore")


def _sc_dispatch(u2p, dest_t, nrows):
    T, W = u2p.shape
    K = dest_t.shape[0]
    win = SC_WINDOW

    @pl.kernel(out_type=jax.ShapeDtypeStruct((nrows, W), u2p.dtype), mesh=_sc_mesh(), scratch_types=[])
    def scatter_rows(x_hbm, i_hbm, o_hbm):
        def body(x_vmem, i_vmem):
            for k in range(K):
                pltpu.sync_copy(x_vmem, o_hbm.at[i_vmem.at[k]])

        pltpu.emit_pipeline(
            body,
            grid=(T // win,),
            in_specs=[pl.BlockSpec((win, W), lambda i: (i, 0)),
                      pl.BlockSpec((K, win), lambda i: (0, i))],
            out_specs=[],
            core_axis_name=("core", "subcore"),
            dimension_semantics=(pltpu.PARALLEL,),
        )(x_hbm, i_hbm)

    return scatter_rows(u2p, dest_t)


def _sc_gather(ys, dest_flat):
    W = ys.shape[1]
    n = dest_flat.shape[1]
    win = SC_WINDOW

    @pl.kernel(out_type=jax.ShapeDtypeStruct((n, W), ys.dtype), mesh=_sc_mesh(), scratch_types=[])
    def gather_rows(y_hbm, i_hbm, o_hbm):
        def body(i_vmem, o_vmem):
            pltpu.sync_copy(y_hbm.at[i_vmem.at[0]], o_vmem)

        pltpu.emit_pipeline(
            body,
            grid=(n // win,),
            in_specs=[pl.BlockSpec((1, win), lambda i: (0, i))],
            out_specs=[pl.BlockSpec((win, W), lambda i: (i, 0))],
            core_axis_name=("core", "subcore"),
            dimension_semantics=(pltpu.PARALLEL,),
        )(i_hbm, o_hbm)

    return gather_rows(ys, dest_flat)


def _expert_kernel(be_ref, bv_ref, nb_ref, first_ref, slot_ref, next_ref, xs_ref, w1_hbm, w3_hbm, w2_hbm,
                   y_ref, w1f, w3f, w2f, w1b, w3b, w2b, sem):
    i = pl.program_id(0)

    def weight_copies(e, s):
        return [pltpu.make_async_copy(w1_hbm.at[e], w1f.at[s], sem.at[s]),
                pltpu.make_async_copy(w3_hbm.at[e], w3f.at[s], sem.at[s]),
                pltpu.make_async_copy(w2_hbm.at[e], w2f.at[s], sem.at[s])]

    @pl.when(i == 0)
    def _():
        for cp in weight_copies(be_ref[0], 0):
            cp.start(priority=1)

    @pl.when(jnp.logical_and(i < nb_ref[0], first_ref[i] == 1))
    def _():
        s = slot_ref[i]
        for cp in weight_copies(be_ref[i], s):
            cp.wait()

        @pl.when(next_ref[i] >= 0)
        def _():
            for cp in weight_copies(next_ref[i], 1 - s):
                cp.start(priority=1)

        w1b[...] = w1f[s].astype(BF16)
        w3b[...] = w3f[s].astype(BF16)
        w2b[...] = w2f[s].astype(BF16)

    @pl.when(i < nb_ref[0])
    def _():
        xb = _load_token_rows(xs_ref, valid=bv_ref[i])
        a = _dot_blocks(xb, w1b)
        b = _dot_blocks(xb, w3b)
        hmid = (a * jax.nn.sigmoid(a) * b).astype(BF16)
        _store_token_rows(y_ref, _dot(hmid, w2b[...]))


def _experts(block_e, block_valid, nb_used, xs, w1, w3, w2, rb):
    _, nrows, W = xs.shape
    nb = nrows // rb
    E, D, FF = w1.shape
    idx = jnp.arange(nb, dtype=jnp.int32)
    first = (idx < nb_used[0]) & (block_e != jnp.concatenate([jnp.full((1,), -1, jnp.int32), block_e[:-1]]))
    slot = (jnp.cumsum(first.astype(jnp.int32)) - 1) % 2
    first_pos = jnp.where(first, idx, nb)
    next_pos = jnp.concatenate([lax.cummin(first_pos, reverse=True)[1:], jnp.full((1,), nb, jnp.int32)])
    next_e = jnp.where(next_pos < nb, block_e[jnp.minimum(next_pos, nb - 1)], -1).astype(jnp.int32)
    rows = pl.BlockSpec((2, rb, W), lambda i, be, bv, n, fi, sl, nx: (0, jnp.minimum(i, n[0] - 1), 0))
    grid_spec = pltpu.PrefetchScalarGridSpec(
        num_scalar_prefetch=6,
        grid=(nb,),
        in_specs=[rows, pl.BlockSpec(memory_space=pl.ANY), pl.BlockSpec(memory_space=pl.ANY),
                  pl.BlockSpec(memory_space=pl.ANY)],
        out_specs=rows,
        scratch_shapes=[pltpu.VMEM((2, D, FF), F32), pltpu.VMEM((2, D, FF), F32), pltpu.VMEM((2, FF, D), F32),
                        pltpu.VMEM((D, FF), BF16), pltpu.VMEM((D, FF), BF16), pltpu.VMEM((FF, D), BF16),
                        pltpu.SemaphoreType.DMA((2,))],
    )
    return pl.pallas_call(
        _expert_kernel,
        grid_spec=grid_spec,
        out_shape=jax.ShapeDtypeStruct((2, nrows, W), jnp.uint32),
        compiler_params=_cparams(("arbitrary",)),
        name="experts",
    )(block_e, block_valid, nb_used, first.astype(jnp.int32), slot.astype(jnp.int32), next_e, xs, w1, w3, w2)


def _combine_kernel(yg_ref, gw_ref, u_ref, x1_ref, mod_ref, s1_ref, s3_ref, s2_ref, lg_ref, lb_ref, o_ref):
    ub = _load_token_rows(u_ref)
    a = _dot_blocks(ub, s1_ref)
    b = _dot_blocks(ub, s3_ref)
    ffn = _dot((a * jax.nn.sigmoid(a) * b).astype(BF16), s2_ref[...])
    gw = gw_ref[...]
    routed = None
    for k in range(TOP_K):
        yk = [gw[:, k:k + 1] * blk.astype(F32) for blk in _load_token_rows(yg_ref.at[k])]
        routed = yk if routed is None else [r + y for r, y in zip(routed, yk)]
    ffn = ffn + jnp.concatenate(routed, axis=1)
    g2 = mod_ref[5:6, :]
    o_ref[...] = _ln(DN_ALPHA * x1_ref[...] + g2 * ffn) * lg_ref[...] + lb_ref[...]


def _combine(yg, gw, u2p, x1, mod, s1, s3, s2, ln_g, ln_b, tm):
    B, S, D = x1.shape
    K = yg.shape[0]
    row = lambda n: pl.BlockSpec((None, tm, n), lambda b, i: (b, i, 0))
    cst = lambda a: pl.BlockSpec(a.shape, lambda b, i: (0, 0))
    return pl.pallas_call(
        _combine_kernel,
        grid=(B, S // tm),
        in_specs=[pl.BlockSpec((K, 2, None, tm, D // 4), lambda b, i: (0, 0, b, i, 0)),
                  row(K), pl.BlockSpec((2, None, tm, D // 4), lambda b, i: (0, b, i, 0)), row(D),
                  pl.BlockSpec((None, 6, D), lambda b, i: (b, 0, 0)),
                  cst(s1), cst(s3), cst(s2), cst(ln_g), cst(ln_b)],
        out_specs=row(D),
        out_shape=jax.ShapeDtypeStruct((B, S, D), F32),
        compiler_params=_cparams(("arbitrary", "arbitrary")),
        name="combine",
    )(yg, gw, u2p, x1, mod, s1, s3, s2, ln_g, ln_b)


def _block_layout(counts, T, rb):
    counts = counts.reshape(-1).astype(jnp.int32)
    pcounts = (counts + rb - 1) // rb * rb
    pend = jnp.cumsum(pcounts)
    poffsets = pend - pcounts
    nb = (T * TOP_K) // rb + N_EXPERTS
    starts = jnp.arange(nb, dtype=jnp.int32) * rb
    block_e = jnp.minimum(jnp.sum(starts[:, None] >= pend[None, :], axis=1), N_EXPERTS - 1).astype(jnp.int32)
    nb_used = (pend[-1] // rb).astype(jnp.int32).reshape(1)
    mine = block_e[:, None] == jnp.arange(N_EXPERTS, dtype=jnp.int32)[None, :]
    seg_start = jnp.sum(jnp.where(mine, poffsets[None, :], 0), axis=1)
    seg_count = jnp.sum(jnp.where(mine, counts[None, :], 0), axis=1)
    block_valid = jnp.clip(seg_count - (starts - seg_start), 0, rb).astype(jnp.int32)
    return poffsets, block_e, block_valid, nb_used, nb * rb


def kernel(x, c, w_ada, b_ada, w_in, ml_conv_w, ml_conv_b, ml_gate_b, ml_norm_g, w_out, ln1_g, ln1_b,
           w_router, router_bias, moe_w1, moe_w3, moe_w2, sh_w1, sh_w3, sh_w2, ln2_g, ln2_b):
    B, S, D = x.shape
    T = B * S
    SBW = SB_HEADS * SB_HEAD_DIM
    MLW = ML_HEADS * ML_HEAD_DIM
    rb = EXPERT_BLOCK_ROWS
    for l in range(DEPTH):
        mod = _adaln(c, w_ada[l], b_ada[l]).reshape(B, 6, D)

        wi = w_in[l]
        c0 = 3 * SBW
        w_sb = wi[:, :c0].astype(BF16)
        w_mqk = wi[:, c0:c0 + 2 * MLW].astype(BF16)
        w_mvo = wi[:, c0 + 2 * MLW:c0 + 4 * MLW].astype(BF16)
        w_g = jnp.pad(wi[:, c0 + 4 * MLW:], ((0, 0), (0, LANES - 2 * ML_HEADS))).astype(BF16)
        sbp, mqk, mvo, gates = _inproj(x, mod, w_sb, w_mqk, w_mvo, w_g, tm=min(512, S))

        sb = _sb_attention(sbp, tq=min(256, S))

        gate_b = jnp.pad(ml_gate_b[l], (0, LANES - 2 * ML_HEADS)).reshape(1, LANES)
        ml = _mlstm(mqk, mvo, gates, ml_conv_w[l], ml_conv_b[l].reshape(1, -1), gate_b,
                    ml_norm_g[l].reshape(1, -1), L=min(256, S))

        wo = w_out[l].astype(BF16)
        wr_parts = jnp.stack(_split3(w_router[l].T)[:2])
        x1, u2p, logits_t = _outproj(sb, ml, x, mod, wo[:SBW], wo[SBW:], ln1_g[l].reshape(1, D),
                                     ln1_b[l].reshape(1, D), wr_parts, tm=min(512, S))

        e_t, r_t, g_t, counts = _route(logits_t, router_bias[l], tm=min(512, T))
        poffsets, block_e, block_valid, nb_used, nrows = _block_layout(counts, T, rb)
        dest_t = _dest(e_t, r_t, poffsets.astype(F32).reshape(N_EXPERTS, 1), tm=min(512, T))
        gw = g_t.T
        idx2 = jnp.concatenate([dest_t, dest_t + nrows], axis=1)
        xs = _sc_dispatch(u2p.reshape(2 * T, D // 4), idx2, 2 * nrows).reshape(2, nrows, D // 4)
        ys = _experts(block_e, block_valid, nb_used, xs, moe_w1[l], moe_w3[l], moe_w2[l], rb)
        yg = _sc_gather(ys.reshape(2 * nrows, D // 4), idx2.reshape(1, TOP_K * 2 * T))
        yg = yg.reshape(TOP_K, 2, B, S, D // 4)
        x = _combine(yg, gw.reshape(B, S, TOP_K), u2p, x1, mod, sh_w1[l].astype(BF16),
                     sh_w3[l].astype(BF16), sh_w2[l].astype(BF16), ln2_g[l].reshape(1, D),
                     ln2_b[l].reshape(1, D), tm=min(256, S))
    return x
```

```python
import functools
import math

import jax
import jax.numpy as jnp
from jax import lax
from jax.experimental import pallas as pl
from jax.experimental.pallas import tpu as pltpu
from jax.experimental.pallas import tpu_sc as plsc

F32 = jnp.float32
BF16 = jnp.bfloat16
HIGHEST = lax.Precision.HIGHEST

SB_HEADS = 8
SB_HEAD_DIM = 64
ML_HEADS = 4
ML_HEAD_DIM = 128
CONV_K = 4
N_EXPERTS = 256
TOP_K = 8
N_GROUPS = 8
TOPK_GROUPS = 4
GROUP_SIZE = N_EXPERTS // N_GROUPS
ROUTED_SCALE = 2.5
EXPERT_BLOCK_ROWS = 512
SC_WINDOW = 128
DEPTH = 1
DN_ALPHA = (2 * DEPTH) ** 0.25
LN_EPS = 1e-5
LANES = 128
SUBLANES = 8
NEG_INF = float("-inf")
SB_CUTOFF = 104.0
SB_QBLOCKS = 8
ML_SEQS_PER_STEP = 4
VMEM_LIMIT = 56 * 1024 * 1024


def _cparams(sem):
    return pltpu.CompilerParams(dimension_semantics=sem, vmem_limit_bytes=VMEM_LIMIT)


def _ln(x):
    mu = jnp.mean(x, axis=-1, keepdims=True)
    xc = x - mu
    var = jnp.mean(xc * xc, axis=-1, keepdims=True)
    return xc * lax.rsqrt(var + LN_EPS)


def _dot(a, b):
    return jnp.dot(a, b, preferred_element_type=F32)


def _dot_nt(a, b):
    return lax.dot_general(a, b, (((1,), (1,)), ((), ())), preferred_element_type=F32)


def _dot_tn(a, b):
    return lax.dot_general(a, b, (((0,), (0,)), ((), ())), preferred_element_type=F32)


def _pack_halves(v):
    w = v.shape[1] // 2
    lo = lax.bitcast_convert_type(v[:, :w].astype(BF16).astype(F32), jnp.uint32) >> 16
    hi = lax.bitcast_convert_type(v[:, w:].astype(BF16).astype(F32), jnp.uint32) & jnp.uint32(0xFFFF0000)
    return hi | lo


def _unpack_halves(p):
    lo = lax.bitcast_convert_type(p << 16, F32).astype(BF16)
    hi = lax.bitcast_convert_type(p & jnp.uint32(0xFFFF0000), F32).astype(BF16)
    return lo, hi


def _store_token_rows(ref, v):
    p = _pack_halves(v)
    q = p.shape[1] // 2
    ref[0] = p[:, :q]
    ref[1] = p[:, q:]


def _load_token_rows(ref, valid=None):
    first, second = ref[0], ref[1]
    if valid is not None:
        row = lax.broadcasted_iota(jnp.int32, first.shape, 0)
        first = jnp.where(row < valid, first, jnp.uint32(0))
        second = jnp.where(row < valid, second, jnp.uint32(0))
    lo_a, hi_a = _unpack_halves(first)
    lo_b, hi_b = _unpack_halves(second)
    return [lo_a, lo_b, hi_a, hi_b]


def _dot_blocks(blocks, w_ref):
    q = blocks[0].shape[1]
    acc = _dot(blocks[0], w_ref[0:q, :])
    for i in range(1, len(blocks)):
        acc = acc + _dot(blocks[i], w_ref[i * q:(i + 1) * q, :])
    return acc


def _adaln_kernel(c_ref, w_ref, b_ref, o_ref):
    c = c_ref[...]
    s = c * jax.nn.sigmoid(c)
    o_ref[...] = jnp.dot(s, w_ref[...], preferred_element_type=F32, precision=HIGHEST) + b_ref[...]


def _adaln(c, w_ada, b_ada):
    B, D = c.shape
    N = w_ada.shape[1]
    tn = 1024
    return pl.pallas_call(
        _adaln_kernel,
        grid=(N // tn,),
        in_specs=[pl.BlockSpec((B, D), lambda j: (0, 0)),
                  pl.BlockSpec((D, tn), lambda j: (0, j)),
                  pl.BlockSpec((1, tn), lambda j: (0, j))],
        out_specs=pl.BlockSpec((B, tn), lambda j: (0, j)),
        out_shape=jax.ShapeDtypeStruct((B, N), F32),
        compiler_params=_cparams(("arbitrary",)),
        name="adaln",
    )(c, w_ada, b_ada.reshape(1, N))


def _inproj_kernel(x_ref, mod_ref, wsb_ref, wqk_ref, wvo_ref, wg_ref,
                   sb_ref, mqk_ref, mvo_ref, g_ref):
    y = _ln(x_ref[...])
    sh = mod_ref[0:1, :]
    sc = mod_ref[1:2, :]
    u = (y * (1.0 + sc) + sh).astype(BF16)
    sb_ref[...] = _dot(u, wsb_ref[...]).astype(BF16)
    mqk_ref[...] = _dot(u, wqk_ref[...])
    mvo_ref[...] = _dot(u, wvo_ref[...]).astype(BF16)
    g_ref[...] = _dot(u, wg_ref[...])


def _inproj(x, mod, w_sb, w_mqk, w_mvo, w_g, tm):
    B, S, D = x.shape
    nsb, nqk, nvo, ng = w_sb.shape[1], w_mqk.shape[1], w_mvo.shape[1], w_g.shape[1]
    row = lambda n: pl.BlockSpec((None, tm, n), lambda b, i: (b, i, 0))
    full = lambda n: pl.BlockSpec((D, n), lambda b, i: (0, 0))
    return pl.pallas_call(
        _inproj_kernel,
        grid=(B, S // tm),
        in_specs=[row(D), pl.BlockSpec((None, 6, D), lambda b, i: (b, 0, 0)),
                  full(nsb), full(nqk), full(nvo), full(ng)],
        out_specs=[row(nsb), row(nqk), row(nvo), row(ng)],
        out_shape=[jax.ShapeDtypeStruct((B, S, nsb), BF16),
                   jax.ShapeDtypeStruct((B, S, nqk), F32),
                   jax.ShapeDtypeStruct((B, S, nvo), BF16),
                   jax.ShapeDtypeStruct((B, S, ng), F32)],
        compiler_params=_cparams(("arbitrary", "arbitrary")),
        name="inproj",
    )(x, mod, w_sb, w_mqk, w_mvo, w_g)


def _sb_block(z):
    sp = jnp.log(1.0 + jnp.exp(-jnp.abs(z)))
    log_beta = jnp.minimum(z, 0.0) - sp
    return log_beta, log_beta - z


def _sb_kernel(q_ref, k_ref, v_ref, o_ref, *, tq, scale):
    first = pl.program_id(2) * SB_QBLOCKS
    lane = lax.broadcasted_iota(jnp.int32, (1, LANES), 1)
    r = lax.broadcasted_iota(jnp.int32, (tq, tq), 0)
    c = lax.broadcasted_iota(jnp.int32, (tq, tq), 1)
    upper = (r > c).astype(BF16)
    strict = c < r
    hmasks = [(lane // SB_HEAD_DIM) == h for h in range(2)]
    chains = [(u, h) for u in range(SB_QBLOCKS) for h in range(2)]
    qs = {}
    for u in range(SB_QBLOCKS):
        q2 = q_ref[u * tq:(u + 1) * tq, :]
        for h in range(2):
            qs[u, h] = jnp.where(hmasks[h], q2 * scale, jnp.zeros_like(q2))

    def block(qh, kblk, vblk, carry, masked):
        z = _dot_nt(qh, kblk)
        log_beta, log_1m = _sb_block(z)
        if masked:
            log_1m = jnp.where(strict, log_1m, 0.0)
        after = _dot(log_1m.astype(BF16), upper)
        a = jnp.exp(log_beta + after + carry)
        if masked:
            a = jnp.where(strict, a, 0.0)
        pv = _dot(a.astype(BF16), vblk)
        return pv, carry + jnp.sum(log_1m, axis=1, keepdims=True)

    def sweep(back, accs, carries, masked):
        new_accs, new_carries = dict(accs), dict(carries)
        for u in range(SB_QBLOCKS):
            kb = first + u - back
            off = pl.multiple_of(jnp.maximum(kb, 0) * tq, tq)
            kblk = k_ref[pl.ds(off, tq), :]
            vblk = v_ref[pl.ds(off, tq), :]
            for h in range(2):
                pv, carry = block(qs[u, h], kblk, vblk, carries[u, h], masked)
                new_accs[u, h] = jnp.where(kb >= 0, accs[u, h] + pv, accs[u, h])
                new_carries[u, h] = jnp.where(kb >= 0, carry, carries[u, h])
        return new_accs, new_carries

    accs = {ch: jnp.zeros((tq, LANES), F32) for ch in chains}
    carries = {ch: jnp.zeros((tq, 1), F32) for ch in chains}
    accs, carries = sweep(0, accs, carries, True)

    def top_of(carries, back):
        top = jnp.float32(NEG_INF)
        for u in range(SB_QBLOCKS):
            t = jnp.max(jnp.maximum(carries[u, 0], carries[u, 1]))
            top = jnp.maximum(top, jnp.where(first + u - back >= 0, t, NEG_INF))
        return top

    def cond(st):
        _, _, _, top = st
        return top > -SB_CUTOFF

    def body(st):
        back, accs, carries, _ = st
        accs, carries = sweep(back, accs, carries, False)
        return back + 1, accs, carries, top_of(carries, back + 1)

    _, accs, _, _ = lax.while_loop(cond, body, (jnp.int32(1), accs, carries, top_of(carries, 1)))
    for u in range(SB_QBLOCKS):
        o_ref[u * tq:(u + 1) * tq, :] = jnp.where(hmasks[0], accs[u, 0], accs[u, 1]).astype(o_ref.dtype)


def _sb_attention(sbp, tq):
    B, S, W3 = sbp.shape
    W = W3 // 3
    npair = W // LANES
    ts = SB_QBLOCKS * tq
    kern = functools.partial(_sb_kernel, tq=tq, scale=SB_HEAD_DIM ** -0.5)
    return pl.pallas_call(
        kern,
        grid=(B, npair, S // ts),
        in_specs=[pl.BlockSpec((None, ts, LANES), lambda b, p, i: (b, i, p)),
                  pl.BlockSpec((None, S, LANES), lambda b, p, i: (b, 0, npair + p)),
                  pl.BlockSpec((None, S, LANES), lambda b, p, i: (b, 0, 2 * npair + p))],
        out_specs=pl.BlockSpec((None, ts, LANES), lambda b, p, i: (b, i, p)),
        out_shape=jax.ShapeDtypeStruct((B, S, W), BF16),
        compiler_params=_cparams(("arbitrary", "arbitrary", "arbitrary")),
        name="sb_attention",
    )(sbp, sbp, sbp)


def _mlstm_kernel(qk_ref, vo_ref, g_ref, cw_ref, cb_ref, gb_ref, ng_ref, o_ref,
                  xbuf, ct_ref, m_ref, *, L):
    @pl.when(pl.program_id(1) == 0)
    def _():
        xbuf[:, 0:8, :] = jnp.zeros((xbuf.shape[0], 8, xbuf.shape[2]), F32)
        ct_ref[...] = jnp.zeros_like(ct_ref)
        m_ref[...] = jnp.zeros_like(m_ref)

    for b in range(qk_ref.shape[0]):
        _mlstm_chunk(qk_ref.at[b], vo_ref.at[b], g_ref.at[b], cw_ref, cb_ref, gb_ref, ng_ref, o_ref.at[b],
                     xbuf.at[b], ct_ref.at[b], m_ref.at[b], L)


def _mlstm_chunk(qk_ref, vo_ref, g_ref, cw_ref, cb_ref, gb_ref, ng_ref, o_ref, xbuf, ct_ref, m_ref, L):
    H, dk = ML_HEADS, ML_HEAD_DIM
    W = H * dk

    xbuf[8:8 + L, :] = qk_ref[...]
    y = cb_ref[...] + cw_ref[CONV_K - 1:CONV_K, :] * xbuf[8:8 + L, :]
    for j in range(1, CONV_K):
        y = y + cw_ref[CONV_K - 1 - j:CONV_K - j, :] * xbuf[8 - j:8 - j + L, :]
    xbuf[0:8, :] = xbuf[L:L + 8, :]
    qk = y * jax.nn.sigmoid(y)

    g = g_ref[...] + gb_ref[...]
    logf = jax.nn.log_sigmoid(g)
    r = lax.broadcasted_iota(jnp.int32, (L, L), 0)
    c = lax.broadcasted_iota(jnp.int32, (L, L), 1)
    causal = c <= r
    tri = causal.astype(BF16)
    lf_hi = logf.astype(BF16)
    lf_lo = (logf - lf_hi.astype(F32)).astype(BF16)
    bc = _dot(tri, lf_hi) + _dot(tri, lf_lo)
    g_t = g.T
    bc_t = bc.T
    e0 = (lax.broadcasted_iota(jnp.int32, (L, LANES), 1) == 0).astype(F32)

    for h in range(H):
        qh = qk[:, h * dk:(h + 1) * dk].astype(BF16)
        kh = (qk[:, W + h * dk:W + (h + 1) * dk] * (dk ** -0.5)).astype(BF16)
        vh = vo_ref[:, h * dk:(h + 1) * dk].astype(F32)
        oh = vo_ref[:, W + h * dk:W + (h + 1) * dk].astype(F32)
        vaug = jnp.concatenate([vh, e0], axis=1)
        ic_col = g[:, h:h + 1]
        ic_row = g_t[h:h + 1, :]
        bc_col = bc[:, H + h:H + h + 1]
        bc_row = bc_t[H + h:H + h + 1, :]
        m = m_ref[h][:, 0:1]
        ct = ct_ref[h]

        log_d = jnp.where(causal, bc_col - bc_row + ic_row, NEG_INF)
        inter = bc_col + m
        m_t = jnp.maximum(inter, jnp.max(log_d, axis=1, keepdims=True))
        w = _dot_nt(qh, kh) * jnp.exp(log_d - m_t)
        s_inter = jnp.exp(inter - m_t)
        tot = s_inter * _dot(qh, ct.astype(BF16)) + _dot(w.astype(BF16), vaug.astype(BF16))
        num = tot[:, :dk]
        den = tot[:, dk:dk + 1]
        hh = num / jnp.maximum(jnp.abs(den), jnp.exp(-m_t))

        b_last = bc_col[L - 1:L, :]
        log_w = b_last - bc_col + ic_col
        m_new = jnp.maximum(b_last + m, jnp.max(log_w, axis=0, keepdims=True))
        wk = jnp.exp(log_w - m_new)
        decay = jnp.exp(b_last + m - m_new)
        ct_ref[h] = decay * ct + _dot_tn(kh, (wk * vaug).astype(BF16))
        m_ref[h] = jnp.broadcast_to(m_new, (1, LANES))

        hn = _ln(hh) * ng_ref[:, h * dk:(h + 1) * dk]
        o_ref[:, h * dk:(h + 1) * dk] = (jax.nn.sigmoid(oh) * hn).astype(o_ref.dtype)


def _mlstm(mqk, mvo, gates, conv_w, conv_b, gate_b, norm_g, L):
    B, S, W2 = mqk.shape
    W = W2 // 2
    kern = functools.partial(_mlstm_kernel, L=L)
    nb = math.gcd(B, ML_SEQS_PER_STEP)
    row = lambda n: pl.BlockSpec((nb, L, n), lambda b, i: (b, i, 0))
    cst = lambda a: pl.BlockSpec(a.shape, lambda b, i: (0, 0))
    return pl.pallas_call(
        kern,
        grid=(B // nb, S // L),
        in_specs=[row(W2), row(W2), row(LANES), cst(conv_w), cst(conv_b), cst(gate_b), cst(norm_g)],
        out_specs=row(W),
        out_shape=jax.ShapeDtypeStruct((B, S, W), BF16),
        scratch_shapes=[pltpu.VMEM((nb, L + 8, W2), F32),
                        pltpu.VMEM((nb, ML_HEADS, ML_HEAD_DIM, 2 * ML_HEAD_DIM), F32),
                        pltpu.VMEM((nb, ML_HEADS, 1, LANES), F32)],
        compiler_params=_cparams(("arbitrary", "arbitrary")),
        name="mlstm",
    )(mqk, mvo, gates, conv_w, conv_b, gate_b, norm_g)


def _keep_bf16_bits(v):
    bits = lax.bitcast_convert_type(v, jnp.uint32) & jnp.uint32(0xFFFF0000)
    return lax.bitcast_convert_type(bits, F32)


def _split3(v):
    hi = _keep_bf16_bits(v)
    r = v - hi
    mid = _keep_bf16_bits(r)
    lo = r - mid
    return hi.astype(BF16), mid.astype(BF16), lo.astype(BF16)


def _outproj_kernel(sb_ref, ml_ref, x_ref, mod_ref, wo1_ref, wo2_ref, lg_ref, lb_ref, wr_ref,
                    x1_ref, u2_ref, lo_ref):
    mix = _dot(sb_ref[...], wo1_ref[...]) + _dot(ml_ref[...], wo2_ref[...])
    g1 = mod_ref[2:3, :]
    x1 = _ln(DN_ALPHA * x_ref[...] + g1 * mix) * lg_ref[...] + lb_ref[...]
    x1_ref[...] = x1
    u2 = _ln(x1) * (1.0 + mod_ref[4:5, :]) + mod_ref[3:4, :]
    _store_token_rows(u2_ref, u2)
    u_hi, u_mid, _ = _split3(u2)
    w_hi, w_mid = wr_ref[0], wr_ref[1]
    lo_ref[...] = _dot_nt(w_hi, u_hi) + (_dot_nt(w_mid, u_hi) + _dot_nt(w_hi, u_mid))


def _outproj(sb, ml, x, mod, wo1, wo2, ln_g, ln_b, wr_parts, tm):
    B, S, D = x.shape
    W = sb.shape[2]
    E = wr_parts.shape[1]
    nt = S // tm
    row = lambda n: pl.BlockSpec((None, tm, n), lambda b, i: (b, i, 0))
    cst = lambda a: pl.BlockSpec(a.shape, lambda b, i: (0, 0))
    return pl.pallas_call(
        _outproj_kernel,
        grid=(B, nt),
        in_specs=[row(W), row(W), row(D), pl.BlockSpec((None, 6, D), lambda b, i: (b, 0, 0)),
                  cst(wo1), cst(wo2), cst(ln_g), cst(ln_b),
                  pl.BlockSpec(wr_parts.shape, lambda b, i: (0, 0, 0))],
        out_specs=[row(D), pl.BlockSpec((2, None, tm, D // 4), lambda b, i: (0, b, i, 0)),
                   pl.BlockSpec((E, tm), lambda b, i: (0, b * nt + i))],
        out_shape=[jax.ShapeDtypeStruct((B, S, D), F32),
                   jax.ShapeDtypeStruct((2, B, S, D // 4), jnp.uint32),
                   jax.ShapeDtypeStruct((E, B * S), F32)],
        compiler_params=_cparams(("arbitrary", "arbitrary")),
        name="outproj",
    )(sb, ml, x, mod, wo1, wo2, ln_g, ln_b, wr_parts)


def _rows_to_block(rows, dtype):
    n = rows[0].shape[1]
    rid = lax.broadcasted_iota(jnp.int32, (len(rows), n), 0)
    out = jnp.zeros((len(rows), n), dtype)
    for k, v in enumerate(rows):
        out = jnp.where(rid == k, v.astype(dtype), out)
    return out


def _route_kernel(lo_ref, rb_ref, e_ref, r_ref, g_ref, cnt_ref, cnt_scr):
    E, tm = lo_ref.shape

    @pl.when(pl.program_id(0) == 0)
    def _():
        cnt_scr[...] = jnp.zeros_like(cnt_scr)

    scores = jax.nn.sigmoid(lo_ref[...])
    sel = scores + rb_ref[...]
    row_f = lax.broadcasted_iota(jnp.int32, (E, tm), 0).astype(F32)
    groups = [sel[g * GROUP_SIZE:(g + 1) * GROUP_SIZE, :] for g in range(N_GROUPS)]
    gscore = []
    for xg in groups:
        m1 = jnp.max(xg, axis=0, keepdims=True)
        is_max = xg == m1
        cnt = jnp.sum(is_max.astype(F32), axis=0, keepdims=True)
        m2 = jnp.max(jnp.where(is_max, NEG_INF, xg), axis=0, keepdims=True)
        gscore.append(m1 + jnp.where(cnt >= 2.0, m1, m2))
    kept = []
    for g in range(N_GROUPS):
        rank = jnp.zeros((1, tm), jnp.int32)
        for o in range(N_GROUPS):
            if o != g:
                beats = (gscore[o] >= gscore[g]) if o < g else (gscore[o] > gscore[g])
                rank = rank + beats.astype(jnp.int32)
        kept.append(jnp.where(rank < TOPK_GROUPS, groups[g], NEG_INF))
    cur = jnp.concatenate(kept, axis=0)
    idxs = []
    chosen = jnp.zeros((E, tm), F32)
    for k in range(TOP_K):
        m = jnp.max(cur, axis=0, keepdims=True)
        idx = jnp.min(jnp.where(cur == m, row_f, float(E)), axis=0, keepdims=True)
        pick = row_f == idx
        cur = jnp.where(pick, NEG_INF, cur)
        chosen = jnp.where(pick, 1.0, chosen)
        idxs.append(idx)
    r = lax.broadcasted_iota(jnp.int32, (tm, tm), 0)
    c = lax.broadcasted_iota(jnp.int32, (tm, tm), 1)
    before = _dot(chosen.astype(BF16), (r < c).astype(BF16)) + cnt_scr[...]
    gates, ranks = [], []
    for k in range(TOP_K):
        pick = row_f == idxs[k]
        gates.append(jnp.sum(jnp.where(pick, scores, 0.0), axis=0, keepdims=True))
        ranks.append(jnp.sum(jnp.where(pick, before, 0.0), axis=0, keepdims=True))
    gsum = gates[0]
    for gk in gates[1:]:
        gsum = gsum + gk
    e_ref[...] = _rows_to_block(idxs, jnp.int32)
    r_ref[...] = _rows_to_block(ranks, jnp.int32)
    g_ref[...] = _rows_to_block(gates, F32) / gsum * ROUTED_SCALE
    cnt_scr[...] = cnt_scr[...] + jnp.sum(chosen, axis=1, keepdims=True)
    cnt_ref[...] = cnt_scr[...]


def _route(logits_t, router_bias, tm):
    E, T = logits_t.shape
    tok = lambda dt: jax.ShapeDtypeStruct((TOP_K, T), dt)
    return pl.pallas_call(
        _route_kernel,
        grid=(T // tm,),
        in_specs=[pl.BlockSpec((E, tm), lambda i: (0, i)), pl.BlockSpec((E, 1), lambda i: (0, 0))],
        out_specs=[pl.BlockSpec((TOP_K, tm), lambda i: (0, i))] * 3 + [pl.BlockSpec((E, 1), lambda i: (0, 0))],
        out_shape=[tok(jnp.int32), tok(jnp.int32), tok(F32), jax.ShapeDtypeStruct((E, 1), F32)],
        scratch_shapes=[pltpu.VMEM((E, 1), F32)],
        compiler_params=_cparams(("arbitrary",)),
        name="route",
    )(logits_t, router_bias.reshape(E, 1))


def _dest_kernel(e_ref, r_ref, off_ref, d_ref):
    E = off_ref.shape[0]
    tm = e_ref.shape[1]
    row = lax.broadcasted_iota(jnp.int32, (E, tm), 0)
    off = off_ref[...]
    base = [jnp.sum(jnp.where(row == e_ref[k:k + 1, :], off, 0.0), axis=0, keepdims=True) for k in range(TOP_K)]
    d_ref[...] = _rows_to_block(base, jnp.int32) + r_ref[...]


def _dest(e_t, r_t, offsets, tm):
    K, T = e_t.shape
    E = offsets.shape[0]
    tok = pl.BlockSpec((K, tm), lambda i: (0, i))
    return pl.pallas_call(
        _dest_kernel,
        grid=(T // tm,),
        in_specs=[tok, tok, pl.BlockSpec((E, 1), lambda i: (0, 0))],
        out_specs=tok,
        out_shape=jax.ShapeDtypeStruct((K, T), jnp.int32),
        compiler_params=_cparams(("arbitrary",)),
        name="dest",
    )(e_t, r_t, offsets)


def _sc_mesh():
    return plsc.VectorSubcoreMesh(core_axis_name="core", subcore_axis_name="subcore")


def _sc_dispatch(u2p, dest_t, nrows):
    T, W = u2p.shape
    K = dest_t.shape[0]
    win = SC_WINDOW

    @pl.kernel(out_type=jax.ShapeDtypeStruct((nrows, W), u2p.dtype), mesh=_sc_mesh(), scratch_types=[])
    def scatter_rows(x_hbm, i_hbm, o_hbm):
        def body(x_vmem, i_vmem):
            for k in range(K):
                pltpu.sync_copy(x_vmem, o_hbm.at[i_vmem.at[k]])

        pltpu.emit_pipeline(
            body,
            grid=(T // win,),
            in_specs=[pl.BlockSpec((win, W), lambda i: (i, 0)),
                      pl.BlockSpec((K, win), lambda i: (0, i))],
            out_specs=[],
            core_axis_name=("core", "subcore"),
            dimension_semantics=(pltpu.PARALLEL,),
        )(x_hbm, i_hbm)

    return scatter_rows(u2p, dest_t)


def _sc_gather(ys, dest_flat):
    W = ys.shape[1]
    n = dest_flat.shape[1]
    win = SC_WINDOW

    @pl.kernel(out_type=jax.ShapeDtypeStruct((n, W), ys.dtype), mesh=_sc_mesh(), scratch_types=[])
    def gather_rows(y_hbm, i_hbm, o_hbm):
        def body(i_vmem, o_vmem):
            pltpu.sync_copy(y_hbm.at[i_vmem.at[0]], o_vmem)

        pltpu.emit_pipeline(
            body,
            grid=(n // win,),
            in_specs=[pl.BlockSpec((1, win), lambda i: (0, i))],
            out_specs=[pl.BlockSpec((win, W), lambda i: (i, 0))],
            core_axis_name=("core", "subcore"),
            dimension_semantics=(pltpu.PARALLEL,),
        )(i_hbm, o_hbm)

    return gather_rows(ys, dest_flat)


def _expert_kernel(be_ref, bv_ref, nb_ref, first_ref, slot_ref, next_ref, xs_ref, w1_hbm, w3_hbm, w2_hbm,
                   y_ref, w1f, w3f, w2f, w1b, w3b, w2b, sem):
    i = pl.program_id(0)

    def weight_copies(e, s):
        return [pltpu.make_async_copy(w1_hbm.at[e], w1f.at[s], sem.at[s]),
                pltpu.make_async_copy(w3_hbm.at[e], w3f.at[s], sem.at[s]),
                pltpu.make_async_copy(w2_hbm.at[e], w2f.at[s], sem.at[s])]

    @pl.when(i == 0)
    def _():
        for cp in weight_copies(be_ref[0], 0):
            cp.start(priority=1)

    @pl.when(jnp.logical_and(i < nb_ref[0], first_ref[i] == 1))
    def _():
        s = slot_ref[i]
        for cp in weight_copies(be_ref[i], s):
            cp.wait()

        @pl.when(next_ref[i] >= 0)
        def _():
            for cp in weight_copies(next_ref[i], 1 - s):
                cp.start(priority=1)

        w1b[...] = w1f[s].astype(BF16)
        w3b[...] = w3f[s].astype(BF16)
        w2b[...] = w2f[s].astype(BF16)

    @pl.when(i < nb_ref[0])
    def _():
        xb = _load_token_rows(xs_ref, valid=bv_ref[i])
        a = _dot_blocks(xb, w1b)
        b = _dot_blocks(xb, w3b)
        hmid = (a * jax.nn.sigmoid(a) * b).astype(BF16)
        _store_token_rows(y_ref, _dot(hmid, w2b[...]))


def _experts(block_e, block_valid, nb_used, xs, w1, w3, w2, rb):
    _, nrows, W = xs.shape
    nb = nrows // rb
    E, D, FF = w1.shape
    idx = jnp.arange(nb, dtype=jnp.int32)
    first = (idx < nb_used[0]) & (block_e != jnp.concatenate([jnp.full((1,), -1, jnp.int32), block_e[:-1]]))
    slot = (jnp.cumsum(first.astype(jnp.int32)) - 1) % 2
    first_pos = jnp.where(first, idx, nb)
    next_pos = jnp.concatenate([lax.cummin(first_pos, reverse=True)[1:], jnp.full((1,), nb, jnp.int32)])
    next_e = jnp.where(next_pos < nb, block_e[jnp.minimum(next_pos, nb - 1)], -1).astype(jnp.int32)
    rows = pl.BlockSpec((2, rb, W), lambda i, be, bv, n, fi, sl, nx: (0, jnp.minimum(i, n[0] - 1), 0))
    grid_spec = pltpu.PrefetchScalarGridSpec(
        num_scalar_prefetch=6,
        grid=(nb,),
        in_specs=[rows, pl.BlockSpec(memory_space=pl.ANY), pl.BlockSpec(memory_space=pl.ANY),
                  pl.BlockSpec(memory_space=pl.ANY)],
        out_specs=rows,
        scratch_shapes=[pltpu.VMEM((2, D, FF), F32), pltpu.VMEM((2, D, FF), F32), pltpu.VMEM((2, FF, D), F32),
                        pltpu.VMEM((D, FF), BF16), pltpu.VMEM((D, FF), BF16), pltpu.VMEM((FF, D), BF16),
                        pltpu.SemaphoreType.DMA((2,))],
    )
    return pl.pallas_call(
        _expert_kernel,
        grid_spec=grid_spec,
        out_shape=jax.ShapeDtypeStruct((2, nrows, W), jnp.uint32),
        compiler_params=_cparams(("arbitrary",)),
        name="experts",
    )(block_e, block_valid, nb_used, first.astype(jnp.int32), slot.astype(jnp.int32), next_e, xs, w1, w3, w2)


def _combine_kernel(yg_ref, gw_ref, u_ref, x1_ref, mod_ref, s1_ref, s3_ref, s2_ref, lg_ref, lb_ref, o_ref):
    ub = _load_token_rows(u_ref)
    a = _dot_blocks(ub, s1_ref)
    b = _dot_blocks(ub, s3_ref)
    ffn = _dot((a * jax.nn.sigmoid(a) * b).astype(BF16), s2_ref[...])
    gw = gw_ref[...]
    routed = None
    for k in range(TOP_K):
        yk = [gw[:, k:k + 1] * blk.astype(F32) for blk in _load_token_rows(yg_ref.at[k])]
        routed = yk if routed is None else [r + y for r, y in zip(routed, yk)]
    ffn = ffn + jnp.concatenate(routed, axis=1)
    g2 = mod_ref[5:6, :]
    o_ref[...] = _ln(DN_ALPHA * x1_ref[...] + g2 * ffn) * lg_ref[...] + lb_ref[...]


def _combine(yg, gw, u2p, x1, mod, s1, s3, s2, ln_g, ln_b, tm):
    B, S, D = x1.shape
    K = yg.shape[0]
    row = lambda n: pl.BlockSpec((None, tm, n), lambda b, i: (b, i, 0))
    cst = lambda a: pl.BlockSpec(a.shape, lambda b, i: (0, 0))
    return pl.pallas_call(
        _combine_kernel,
        grid=(B, S // tm),
        in_specs=[pl.BlockSpec((K, 2, None, tm, D // 4), lambda b, i: (0, 0, b, i, 0)),
                  row(K), pl.BlockSpec((2, None, tm, D // 4), lambda b, i: (0, b, i, 0)), row(D),
                  pl.BlockSpec((None, 6, D), lambda b, i: (b, 0, 0)),
                  cst(s1), cst(s3), cst(s2), cst(ln_g), cst(ln_b)],
        out_specs=row(D),
        out_shape=jax.ShapeDtypeStruct((B, S, D), F32),
        compiler_params=_cparams(("arbitrary", "arbitrary")),
        name="combine",
    )(yg, gw, u2p, x1, mod, s1, s3, s2, ln_g, ln_b)


def _block_layout(counts, T, rb):
    counts = counts.reshape(-1).astype(jnp.int32)
    pcounts = (counts + rb - 1) // rb * rb
    pend = jnp.cumsum(pcounts)
    poffsets = pend - pcounts
    nb = (T * TOP_K) // rb + N_EXPERTS
    starts = jnp.arange(nb, dtype=jnp.int32) * rb
    block_e = jnp.minimum(jnp.sum(starts[:, None] >= pend[None, :], axis=1), N_EXPERTS - 1).astype(jnp.int32)
    nb_used = (pend[-1] // rb).astype(jnp.int32).reshape(1)
    mine = block_e[:, None] == jnp.arange(N_EXPERTS, dtype=jnp.int32)[None, :]
    seg_start = jnp.sum(jnp.where(mine, poffsets[None, :], 0), axis=1)
    seg_count = jnp.sum(jnp.where(mine, counts[None, :], 0), axis=1)
    block_valid = jnp.clip(seg_count - (starts - seg_start), 0, rb).astype(jnp.int32)
    return poffsets, block_e, block_valid, nb_used, nb * rb


def kernel(x, c, w_ada, b_ada, w_in, ml_conv_w, ml_conv_b, ml_gate_b, ml_norm_g, w_out, ln1_g, ln1_b,
           w_router, router_bias, moe_w1, moe_w3, moe_w2, sh_w1, sh_w3, sh_w2, ln2_g, ln2_b):
    B, S, D = x.shape
    T = B * S
    SBW = SB_HEADS * SB_HEAD_DIM
    MLW = ML_HEADS * ML_HEAD_DIM
    rb = EXPERT_BLOCK_ROWS
    for l in range(DEPTH):
        mod = _adaln(c, w_ada[l], b_ada[l]).reshape(B, 6, D)

        wi = w_in[l]
        c0 = 3 * SBW
        w_sb = wi[:, :c0].astype(BF16)
        w_mqk = wi[:, c0:c0 + 2 * MLW].astype(BF16)
        w_mvo = wi[:, c0 + 2 * MLW:c0 + 4 * MLW].astype(BF16)
        w_g = jnp.pad(wi[:, c0 + 4 * MLW:], ((0, 0), (0, LANES - 2 * ML_HEADS))).astype(BF16)
        sbp, mqk, mvo, gates = _inproj(x, mod, w_sb, w_mqk, w_mvo, w_g, tm=min(512, S))

        sb = _sb_attention(sbp, tq=min(256, S // SB_QBLOCKS))

        gate_b = jnp.pad(ml_gate_b[l], (0, LANES - 2 * ML_HEADS)).reshape(1, LANES)
        ml = _mlstm(mqk, mvo, gates, ml_conv_w[l], ml_conv_b[l].reshape(1, -1), gate_b,
                    ml_norm_g[l].reshape(1, -1), L=min(256, S))

        wo = w_out[l].astype(BF16)
        wr_parts = jnp.stack(_split3(w_router[l].T)[:2])
        x1, u2p, logits_t = _outproj(sb, ml, x, mod, wo[:SBW], wo[SBW:], ln1_g[l].reshape(1, D),
                                     ln1_b[l].reshape(1, D), wr_parts, tm=min(512, S))

        e_t, r_t, g_t, counts = _route(logits_t, router_bias[l], tm=min(512, T))
        poffsets, block_e, block_valid, nb_used, nrows = _block_layout(counts, T, rb)
        dest_t = _dest(e_t, r_t, poffsets.astype(F32).reshape(N_EXPERTS, 1), tm=min(512, T))
        gw = g_t.T
        idx2 = jnp.concatenate([dest_t, dest_t + nrows], axis=1)
        xs = _sc_dispatch(u2p.reshape(2 * T, D // 4), idx2, 2 * nrows).reshape(2, nrows, D // 4)
        ys = _experts(block_e, block_valid, nb_used, xs, moe_w1[l], moe_w3[l], moe_w2[l], rb)
        yg = _sc_gather(ys.reshape(2 * nrows, D // 4), idx2.reshape(1, TOP_K * 2 * T))
        yg = yg.reshape(TOP_K, 2, B, S, D // 4)
        x = _combine(yg, gw.reshape(B, S, TOP_K), u2p, x1, mod, sh_w1[l].astype(BF16),
                     sh_w3[l].astype(BF16), sh_w2[l].astype(BF16), ln2_g[l].reshape(1, D),
                     ln2_b[l].reshape(1, D), tm=min(256, S))
    return x
```

```python
import functools
import math

import jax
import jax.numpy as jnp
from jax import lax
from jax.experimental import pallas as pl
from jax.experimental.pallas import tpu as pltpu
from jax.experimental.pallas import tpu_sc as plsc

F32 = jnp.float32
BF16 = jnp.bfloat16
HIGHEST = lax.Precision.HIGHEST

SB_HEADS = 8
SB_HEAD_DIM = 64
ML_HEADS = 4
ML_HEAD_DIM = 128
CONV_K = 4
N_EXPERTS = 256
TOP_K = 8
N_GROUPS = 8
TOPK_GROUPS = 4
GROUP_SIZE = N_EXPERTS // N_GROUPS
ROUTED_SCALE = 2.5
EXPERT_BLOCK_ROWS = 512
SC_WINDOW = 128
DEPTH = 1
DN_ALPHA = (2 * DEPTH) ** 0.25
LN_EPS = 1e-5
LANES = 128
NEG_INF = float("-inf")
SB_CUTOFF = 104.0
SB_QBLOCKS = 8
ML_SEQS_PER_STEP = 4
VMEM_LIMIT = 56 * 1024 * 1024

PROJ_ROWS = 512
SB_ROWS = 256
ML_CHUNK_ROWS = 256
COMBINE_ROWS = 256


def _tiles(S):
    return dict(proj=min(PROJ_ROWS, S), sb=min(SB_ROWS, S // SB_QBLOCKS), ml=min(ML_CHUNK_ROWS, S),
                combine=min(COMBINE_ROWS, S))


def _cparams(sem):
    return pltpu.CompilerParams(dimension_semantics=sem, vmem_limit_bytes=VMEM_LIMIT)


def _ln(x):
    mu = jnp.mean(x, axis=-1, keepdims=True)
    xc = x - mu
    var = jnp.mean(xc * xc, axis=-1, keepdims=True)
    return xc * lax.rsqrt(var + LN_EPS)


def _dot(a, b):
    return jnp.dot(a, b, preferred_element_type=F32)


def _dot_nt(a, b):
    return lax.dot_general(a, b, (((1,), (1,)), ((), ())), preferred_element_type=F32)


def _dot_tn(a, b):
    return lax.dot_general(a, b, (((0,), (0,)), ((), ())), preferred_element_type=F32)


def _pack_halves(v):
    w = v.shape[1] // 2
    lo = lax.bitcast_convert_type(v[:, :w].astype(BF16).astype(F32), jnp.uint32) >> 16
    hi = lax.bitcast_convert_type(v[:, w:].astype(BF16).astype(F32), jnp.uint32) & jnp.uint32(0xFFFF0000)
    return hi | lo


def _unpack_halves(p):
    lo = lax.bitcast_convert_type(p << 16, F32).astype(BF16)
    hi = lax.bitcast_convert_type(p & jnp.uint32(0xFFFF0000), F32).astype(BF16)
    return lo, hi


def _store_token_rows(ref, v):
    p = _pack_halves(v)
    q = p.shape[1] // 2
    ref[0] = p[:, :q]
    ref[1] = p[:, q:]


def _load_token_rows(ref, valid=None):
    first, second = ref[0], ref[1]
    if valid is not None:
        row = lax.broadcasted_iota(jnp.int32, first.shape, 0)
        first = jnp.where(row < valid, first, jnp.uint32(0))
        second = jnp.where(row < valid, second, jnp.uint32(0))
    lo_a, hi_a = _unpack_halves(first)
    lo_b, hi_b = _unpack_halves(second)
    return [lo_a, lo_b, hi_a, hi_b]


def _dot_blocks(blocks, w_ref):
    q = blocks[0].shape[1]
    acc = _dot(blocks[0], w_ref[0:q, :])
    for i in range(1, len(blocks)):
        acc = acc + _dot(blocks[i], w_ref[i * q:(i + 1) * q, :])
    return acc


def _adaln_kernel(c_ref, w_ref, b_ref, o_ref):
    c = c_ref[...]
    s = c * jax.nn.sigmoid(c)
    o_ref[...] = jnp.dot(s, w_ref[...], preferred_element_type=F32, precision=HIGHEST) + b_ref[...]


def _adaln(c, w_ada, b_ada):
    B, D = c.shape
    N = w_ada.shape[1]
    tn = 1024
    return pl.pallas_call(
        _adaln_kernel,
        grid=(N // tn,),
        in_specs=[pl.BlockSpec((B, D), lambda j: (0, 0)),
                  pl.BlockSpec((D, tn), lambda j: (0, j)),
                  pl.BlockSpec((1, tn), lambda j: (0, j))],
        out_specs=pl.BlockSpec((B, tn), lambda j: (0, j)),
        out_shape=jax.ShapeDtypeStruct((B, N), F32),
        compiler_params=_cparams(("arbitrary",)),
        name="adaln",
    )(c, w_ada, b_ada.reshape(1, N))


def _inproj_kernel(x_ref, mod_ref, wsb_ref, wqk_ref, wvo_ref, wg_ref,
                   sb_ref, mqk_ref, mvo_ref, g_ref):
    y = _ln(x_ref[...])
    sh = mod_ref[0:1, :]
    sc = mod_ref[1:2, :]
    u = (y * (1.0 + sc) + sh).astype(BF16)
    sb_ref[...] = _dot(u, wsb_ref[...]).astype(BF16)
    mqk_ref[...] = _dot(u, wqk_ref[...])
    mvo_ref[...] = _dot(u, wvo_ref[...]).astype(BF16)
    g_ref[...] = _dot(u, wg_ref[...])


def _inproj(x, mod, w_sb, w_mqk, w_mvo, w_g, tm):
    B, S, D = x.shape
    nsb, nqk, nvo, ng = w_sb.shape[1], w_mqk.shape[1], w_mvo.shape[1], w_g.shape[1]
    row = lambda n: pl.BlockSpec((None, tm, n), lambda b, i: (b, i, 0))
    full = lambda n: pl.BlockSpec((D, n), lambda b, i: (0, 0))
    return pl.pallas_call(
        _inproj_kernel,
        grid=(B, S // tm),
        in_specs=[row(D), pl.BlockSpec((None, 6, D), lambda b, i: (b, 0, 0)),
                  full(nsb), full(nqk), full(nvo), full(ng)],
        out_specs=[row(nsb), row(nqk), row(nvo), row(ng)],
        out_shape=[jax.ShapeDtypeStruct((B, S, nsb), BF16),
                   jax.ShapeDtypeStruct((B, S, nqk), F32),
                   jax.ShapeDtypeStruct((B, S, nvo), BF16),
                   jax.ShapeDtypeStruct((B, S, ng), F32)],
        compiler_params=_cparams(("arbitrary", "arbitrary")),
        name="inproj",
    )(x, mod, w_sb, w_mqk, w_mvo, w_g)


def _sb_block(z):
    sp = jnp.log(1.0 + jnp.exp(-jnp.abs(z)))
    log_beta = jnp.minimum(z, 0.0) - sp
    return log_beta, log_beta - z


def _sb_kernel(q_ref, k_ref, v_ref, o_ref, *, tq, scale):
    first = pl.program_id(2) * SB_QBLOCKS
    lane = lax.broadcasted_iota(jnp.int32, (1, LANES), 1)
    r = lax.broadcasted_iota(jnp.int32, (tq, tq), 0)
    c = lax.broadcasted_iota(jnp.int32, (tq, tq), 1)
    upper = (r > c).astype(BF16)
    strict = c < r
    hmasks = [(lane // SB_HEAD_DIM) == h for h in range(2)]
    chains = [(u, h) for u in range(SB_QBLOCKS) for h in range(2)]
    qs = {}
    for u in range(SB_QBLOCKS):
        q2 = q_ref[u * tq:(u + 1) * tq, :]
        for h in range(2):
            qs[u, h] = jnp.where(hmasks[h], q2 * scale, jnp.zeros_like(q2))

    def block(qh, kblk, vblk, carry, masked):
        z = _dot_nt(qh, kblk)
        log_beta, log_1m = _sb_block(z)
        if masked:
            log_1m = jnp.where(strict, log_1m, 0.0)
        after = _dot(log_1m.astype(BF16), upper)
        a = jnp.exp(log_beta + after + carry)
        if masked:
            a = jnp.where(strict, a, 0.0)
        pv = _dot(a.astype(BF16), vblk)
        return pv, carry + jnp.sum(log_1m, axis=1, keepdims=True)

    def sweep(back, accs, carries, masked):
        new_accs, new_carries = dict(accs), dict(carries)
        for u in range(SB_QBLOCKS):
            kb = first + u - back
            off = pl.multiple_of(jnp.maximum(kb, 0) * tq, tq)
            kblk = k_ref[pl.ds(off, tq), :]
            vblk = v_ref[pl.ds(off, tq), :]
            for h in range(2):
                pv, carry = block(qs[u, h], kblk, vblk, carries[u, h], masked)
                new_accs[u, h] = jnp.where(kb >= 0, accs[u, h] + pv, accs[u, h])
                new_carries[u, h] = jnp.where(kb >= 0, carry, carries[u, h])
        return new_accs, new_carries

    accs = {ch: jnp.zeros((tq, LANES), F32) for ch in chains}
    carries = {ch: jnp.zeros((tq, 1), F32) for ch in chains}
    accs, carries = sweep(0, accs, carries, True)

    def top_of(carries, back):
        top = jnp.float32(NEG_INF)
        for u in range(SB_QBLOCKS):
            t = jnp.max(jnp.maximum(carries[u, 0], carries[u, 1]))
            top = jnp.maximum(top, jnp.where(first + u - back >= 0, t, NEG_INF))
        return top

    def cond(st):
        _, _, _, top = st
        return top > -SB_CUTOFF

    def body(st):
        back, accs, carries, _ = st
        accs, carries = sweep(back, accs, carries, False)
        return back + 1, accs, carries, top_of(carries, back + 1)

    _, accs, _, _ = lax.while_loop(cond, body, (jnp.int32(1), accs, carries, top_of(carries, 1)))
    for u in range(SB_QBLOCKS):
        o_ref[u * tq:(u + 1) * tq, :] = jnp.where(hmasks[0], accs[u, 0], accs[u, 1]).astype(o_ref.dtype)


def _sb_attention(sbp, tq):
    B, S, W3 = sbp.shape
    W = W3 // 3
    npair = W // LANES
    ts = SB_QBLOCKS * tq
    kern = functools.partial(_sb_kernel, tq=tq, scale=SB_HEAD_DIM ** -0.5)
    return pl.pallas_call(
        kern,
        grid=(B, npair, S // ts),
        in_specs=[pl.BlockSpec((None, ts, LANES), lambda b, p, i: (b, i, p)),
                  pl.BlockSpec((None, S, LANES), lambda b, p, i: (b, 0, npair + p)),
                  pl.BlockSpec((None, S, LANES), lambda b, p, i: (b, 0, 2 * npair + p))],
        out_specs=pl.BlockSpec((None, ts, LANES), lambda b, p, i: (b, i, p)),
        out_shape=jax.ShapeDtypeStruct((B, S, W), BF16),
        compiler_params=_cparams(("arbitrary", "arbitrary", "arbitrary")),
        name="sb_attention",
    )(sbp, sbp, sbp)


def _mlstm_kernel(qk_ref, vo_ref, g_ref, cw_ref, cb_ref, gb_ref, ng_ref, o_ref,
                  xbuf, ct_ref, m_ref, *, L):
    @pl.when(pl.program_id(1) == 0)
    def _():
        xbuf[:, 0:8, :] = jnp.zeros((xbuf.shape[0], 8, xbuf.shape[2]), F32)
        ct_ref[...] = jnp.zeros_like(ct_ref)
        m_ref[...] = jnp.zeros_like(m_ref)

    for b in range(qk_ref.shape[0]):
        _mlstm_chunk(qk_ref.at[b], vo_ref.at[b], g_ref.at[b], cw_ref, cb_ref, gb_ref, ng_ref, o_ref.at[b],
                     xbuf.at[b], ct_ref.at[b], m_ref.at[b], L)


def _mlstm_chunk(qk_ref, vo_ref, g_ref, cw_ref, cb_ref, gb_ref, ng_ref, o_ref, xbuf, ct_ref, m_ref, L):
    H, dk = ML_HEADS, ML_HEAD_DIM
    W = H * dk

    xbuf[8:8 + L, :] = qk_ref[...]
    y = cb_ref[...] + cw_ref[CONV_K - 1:CONV_K, :] * xbuf[8:8 + L, :]
    for j in range(1, CONV_K):
        y = y + cw_ref[CONV_K - 1 - j:CONV_K - j, :] * xbuf[8 - j:8 - j + L, :]
    xbuf[0:8, :] = xbuf[L:L + 8, :]
    qk = y * jax.nn.sigmoid(y)

    g = g_ref[...] + gb_ref[...]
    logf = jax.nn.log_sigmoid(g)
    r = lax.broadcasted_iota(jnp.int32, (L, L), 0)
    c = lax.broadcasted_iota(jnp.int32, (L, L), 1)
    causal = c <= r
    tri = causal.astype(BF16)
    lf_hi = logf.astype(BF16)
    lf_lo = (logf - lf_hi.astype(F32)).astype(BF16)
    bc = _dot(tri, lf_hi) + _dot(tri, lf_lo)
    g_t = g.T
    bc_t = bc.T
    e0 = (lax.broadcasted_iota(jnp.int32, (L, LANES), 1) == 0).astype(F32)

    for h in range(H):
        qh = qk[:, h * dk:(h + 1) * dk].astype(BF16)
        kh = (qk[:, W + h * dk:W + (h + 1) * dk] * (dk ** -0.5)).astype(BF16)
        vh = vo_ref[:, h * dk:(h + 1) * dk].astype(F32)
        oh = vo_ref[:, W + h * dk:W + (h + 1) * dk].astype(F32)
        vaug = jnp.concatenate([vh, e0], axis=1)
        ic_col = g[:, h:h + 1]
        ic_row = g_t[h:h + 1, :]
        bc_col = bc[:, H + h:H + h + 1]
        bc_row = bc_t[H + h:H + h + 1, :]
        m = m_ref[h][:, 0:1]
        ct = ct_ref[h]

        log_d = jnp.where(causal, bc_col - bc_row + ic_row, NEG_INF)
        inter = bc_col + m
        m_t = jnp.maximum(inter, jnp.max(log_d, axis=1, keepdims=True))
        w = _dot_nt(qh, kh) * jnp.exp(log_d - m_t)
        s_inter = jnp.exp(inter - m_t)
        tot = s_inter * _dot(qh, ct.astype(BF16)) + _dot(w.astype(BF16), vaug.astype(BF16))
        num = tot[:, :dk]
        den = tot[:, dk:dk + 1]
        hh = num / jnp.maximum(jnp.abs(den), jnp.exp(-m_t))

        b_last = bc_col[L - 1:L, :]
        log_w = b_last - bc_col + ic_col
        m_new = jnp.maximum(b_last + m, jnp.max(log_w, axis=0, keepdims=True))
        wk = jnp.exp(log_w - m_new)
        decay = jnp.exp(b_last + m - m_new)
        ct_ref[h] = decay * ct + _dot_tn(kh, (wk * vaug).astype(BF16))
        m_ref[h] = jnp.broadcast_to(m_new, (1, LANES))

        hn = _ln(hh) * ng_ref[:, h * dk:(h + 1) * dk]
        o_ref[:, h * dk:(h + 1) * dk] = (jax.nn.sigmoid(oh) * hn).astype(o_ref.dtype)


def _mlstm(mqk, mvo, gates, conv_w, conv_b, gate_b, norm_g, L):
    B, S, W2 = mqk.shape
    W = W2 // 2
    kern = functools.partial(_mlstm_kernel, L=L)
    nb = math.gcd(B, ML_SEQS_PER_STEP)
    row = lambda n: pl.BlockSpec((nb, L, n), lambda b, i: (b, i, 0))
    cst = lambda a: pl.BlockSpec(a.shape, lambda b, i: (0, 0))
    return pl.pallas_call(
        kern,
        grid=(B // nb, S // L),
        in_specs=[row(W2), row(W2), row(LANES), cst(conv_w), cst(conv_b), cst(gate_b), cst(norm_g)],
        out_specs=row(W),
        out_shape=jax.ShapeDtypeStruct((B, S, W), BF16),
        scratch_shapes=[pltpu.VMEM((nb, L + 8, W2), F32),
                        pltpu.VMEM((nb, ML_HEADS, ML_HEAD_DIM, 2 * ML_HEAD_DIM), F32),
                        pltpu.VMEM((nb, ML_HEADS, 1, LANES), F32)],
        compiler_params=_cparams(("arbitrary", "arbitrary")),
        name="mlstm",
    )(mqk, mvo, gates, conv_w, conv_b, gate_b, norm_g)


def _keep_bf16_bits(v):
    bits = lax.bitcast_convert_type(v, jnp.uint32) & jnp.uint32(0xFFFF0000)
    return lax.bitcast_convert_type(bits, F32)


def _split3(v):
    hi = _keep_bf16_bits(v)
    r = v - hi
    mid = _keep_bf16_bits(r)
    lo = r - mid
    return hi.astype(BF16), mid.astype(BF16), lo.astype(BF16)


def _outproj_kernel(sb_ref, ml_ref, x_ref, mod_ref, wo1_ref, wo2_ref, lg_ref, lb_ref, wr_ref,
                    x1_ref, u2_ref, lo_ref):
    mix = _dot(sb_ref[...], wo1_ref[...]) + _dot(ml_ref[...], wo2_ref[...])
    g1 = mod_ref[2:3, :]
    x1 = _ln(DN_ALPHA * x_ref[...] + g1 * mix) * lg_ref[...] + lb_ref[...]
    x1_ref[...] = x1
    u2 = _ln(x1) * (1.0 + mod_ref[4:5, :]) + mod_ref[3:4, :]
    _store_token_rows(u2_ref, u2)
    u_hi, u_mid, _ = _split3(u2)
    w_hi, w_mid = wr_ref[0], wr_ref[1]
    lo_ref[...] = _dot_nt(w_hi, u_hi) + (_dot_nt(w_mid, u_hi) + _dot_nt(w_hi, u_mid))


def _outproj(sb, ml, x, mod, wo1, wo2, ln_g, ln_b, wr_parts, tm):
    B, S, D = x.shape
    W = sb.shape[2]
    E = wr_parts.shape[1]
    nt = S // tm
    row = lambda n: pl.BlockSpec((None, tm, n), lambda b, i: (b, i, 0))
    cst = lambda a: pl.BlockSpec(a.shape, lambda b, i: (0, 0))
    return pl.pallas_call(
        _outproj_kernel,
        grid=(B, nt),
        in_specs=[row(W), row(W), row(D), pl.BlockSpec((None, 6, D), lambda b, i: (b, 0, 0)),
                  cst(wo1), cst(wo2), cst(ln_g), cst(ln_b),
                  pl.BlockSpec(wr_parts.shape, lambda b, i: (0, 0, 0))],
        out_specs=[row(D), pl.BlockSpec((2, None, tm, D // 4), lambda b, i: (0, b, i, 0)),
                   pl.BlockSpec((E, tm), lambda b, i: (0, b * nt + i))],
        out_shape=[jax.ShapeDtypeStruct((B, S, D), F32),
                   jax.ShapeDtypeStruct((2, B, S, D // 4), jnp.uint32),
                   jax.ShapeDtypeStruct((E, B * S), F32)],
        compiler_params=_cparams(("arbitrary", "arbitrary")),
        name="outproj",
    )(sb, ml, x, mod, wo1, wo2, ln_g, ln_b, wr_parts)


def _rows_to_block(rows, dtype):
    n = rows[0].shape[1]
    rid = lax.broadcasted_iota(jnp.int32, (len(rows), n), 0)
    out = jnp.zeros((len(rows), n), dtype)
    for k, v in enumerate(rows):
        out = jnp.where(rid == k, v.astype(dtype), out)
    return out


def _route_kernel(lo_ref, rb_ref, e_ref, r_ref, g_ref, cnt_ref, cnt_scr):
    E, tm = lo_ref.shape

    @pl.when(pl.program_id(0) == 0)
    def _():
        cnt_scr[...] = jnp.zeros_like(cnt_scr)

    scores = jax.nn.sigmoid(lo_ref[...])
    sel = scores + rb_ref[...]
    row_f = lax.broadcasted_iota(jnp.int32, (E, tm), 0).astype(F32)
    groups = [sel[g * GROUP_SIZE:(g + 1) * GROUP_SIZE, :] for g in range(N_GROUPS)]
    gscore = []
    for xg in groups:
        m1 = jnp.max(xg, axis=0, keepdims=True)
        is_max = xg == m1
        cnt = jnp.sum(is_max.astype(F32), axis=0, keepdims=True)
        m2 = jnp.max(jnp.where(is_max, NEG_INF, xg), axis=0, keepdims=True)
        gscore.append(m1 + jnp.where(cnt >= 2.0, m1, m2))
    kept = []
    for g in range(N_GROUPS):
        rank = jnp.zeros((1, tm), jnp.int32)
        for o in range(N_GROUPS):
            if o != g:
                beats = (gscore[o] >= gscore[g]) if o < g else (gscore[o] > gscore[g])
                rank = rank + beats.astype(jnp.int32)
        kept.append(jnp.where(rank < TOPK_GROUPS, groups[g], NEG_INF))
    cur = jnp.concatenate(kept, axis=0)
    idxs = []
    chosen = jnp.zeros((E, tm), F32)
    for k in range(TOP_K):
        m = jnp.max(cur, axis=0, keepdims=True)
        idx = jnp.min(jnp.where(cur == m, row_f, float(E)), axis=0, keepdims=True)
        pick = row_f == idx
        cur = jnp.where(pick, NEG_INF, cur)
        chosen = jnp.where(pick, 1.0, chosen)
        idxs.append(idx)
    r = lax.broadcasted_iota(jnp.int32, (tm, tm), 0)
    c = lax.broadcasted_iota(jnp.int32, (tm, tm), 1)
    before = _dot(chosen.astype(BF16), (r < c).astype(BF16)) + cnt_scr[...]
    gates, ranks = [], []
    for k in range(TOP_K):
        pick = row_f == idxs[k]
        gates.append(jnp.sum(jnp.where(pick, scores, 0.0), axis=0, keepdims=True))
        ranks.append(jnp.sum(jnp.where(pick, before, 0.0), axis=0, keepdims=True))
    gsum = gates[0]
    for gk in gates[1:]:
        gsum = gsum + gk
    e_ref[...] = _rows_to_block(idxs, jnp.int32)
    r_ref[...] = _rows_to_block(ranks, jnp.int32)
    g_ref[...] = _rows_to_block(gates, F32) / gsum * ROUTED_SCALE
    cnt_scr[...] = cnt_scr[...] + jnp.sum(chosen, axis=1, keepdims=True)
    cnt_ref[...] = cnt_scr[...]


def _route(logits_t, router_bias, tm):
    E, T = logits_t.shape
    tok = lambda dt: jax.ShapeDtypeStruct((TOP_K, T), dt)
    return pl.pallas_call(
        _route_kernel,
        grid=(T // tm,),
        in_specs=[pl.BlockSpec((E, tm), lambda i: (0, i)), pl.BlockSpec((E, 1), lambda i: (0, 0))],
        out_specs=[pl.BlockSpec((TOP_K, tm), lambda i: (0, i))] * 3 + [pl.BlockSpec((E, 1), lambda i: (0, 0))],
        out_shape=[tok(jnp.int32), tok(jnp.int32), tok(F32), jax.ShapeDtypeStruct((E, 1), F32)],
        scratch_shapes=[pltpu.VMEM((E, 1), F32)],
        compiler_params=_cparams(("arbitrary",)),
        name="route",
    )(logits_t, router_bias.reshape(E, 1))


def _dest_kernel(e_ref, r_ref, off_ref, d_ref):
    E = off_ref.shape[0]
    tm = e_ref.shape[1]
    row = lax.broadcasted_iota(jnp.int32, (E, tm), 0)
    off = off_ref[...]
    base = [jnp.sum(jnp.where(row == e_ref[k:k + 1, :], off, 0.0), axis=0, keepdims=True) for k in range(TOP_K)]
    d_ref[...] = _rows_to_block(base, jnp.int32) + r_ref[...]


def _dest(e_t, r_t, offsets, tm):
    K, T = e_t.shape
    E = offsets.shape[0]
    tok = pl.BlockSpec((K, tm), lambda i: (0, i))
    return pl.pallas_call(
        _dest_kernel,
        grid=(T // tm,),
        in_specs=[tok, tok, pl.BlockSpec((E, 1), lambda i: (0, 0))],
        out_specs=tok,
        out_shape=jax.ShapeDtypeStruct((K, T), jnp.int32),
        compiler_params=_cparams(("arbitrary",)),
        name="dest",
    )(e_t, r_t, offsets)


def _sc_mesh():
    return plsc.VectorSubcoreMesh(core_axis_name="core", subcore_axis_name="subcore")


def _sc_dispatch(u2p, dest_t, nrows):
    T, W = u2p.shape
    K = dest_t.shape[0]
    win = SC_WINDOW

    @pl.kernel(out_type=jax.ShapeDtypeStruct((nrows, W), u2p.dtype), mesh=_sc_mesh(), scratch_types=[])
    def scatter_rows(x_hbm, i_hbm, o_hbm):
        def body(x_vmem, i_vmem):
            for k in range(K):
                pltpu.sync_copy(x_vmem, o_hbm.at[i_vmem.at[k]])

        pltpu.emit_pipeline(
            body,
            grid=(T // win,),
            in_specs=[pl.BlockSpec((win, W), lambda i: (i, 0)),
                      pl.BlockSpec((K, win), lambda i: (0, i))],
            out_specs=[],
            core_axis_name=("core", "subcore"),
            dimension_semantics=(pltpu.PARALLEL,),
        )(x_hbm, i_hbm)

    return scatter_rows(u2p, dest_t)


def _sc_gather(ys, dest_flat):
    W = ys.shape[1]
    n = dest_flat.shape[1]
    win = SC_WINDOW

    @pl.kernel(out_type=jax.ShapeDtypeStruct((n, W), ys.dtype), mesh=_sc_mesh(), scratch_types=[])
    def gather_rows(y_hbm, i_hbm, o_hbm):
        def body(i_vmem, o_vmem):
            pltpu.sync_copy(y_hbm.at[i_vmem.at[0]], o_vmem)

        pltpu.emit_pipeline(
            body,
            grid=(n // win,),
            in_specs=[pl.BlockSpec((1, win), lambda i: (0, i))],
            out_specs=[pl.BlockSpec((win, W), lambda i: (i, 0))],
            core_axis_name=("core", "subcore"),
            dimension_semantics=(pltpu.PARALLEL,),
        )(i_hbm, o_hbm)

    return gather_rows(ys, dest_flat)


def _expert_kernel(be_ref, bv_ref, nb_ref, first_ref, slot_ref, next_ref, xs_ref, w1_hbm, w3_hbm, w2_hbm,
                   y_ref, w1f, w3f, w2f, w1b, w3b, w2b, sem):
    i = pl.program_id(0)

    def weight_copies(e, s):
        return [pltpu.make_async_copy(w1_hbm.at[e], w1f.at[s], sem.at[s]),
                pltpu.make_async_copy(w3_hbm.at[e], w3f.at[s], sem.at[s]),
                pltpu.make_async_copy(w2_hbm.at[e], w2f.at[s], sem.at[s])]

    @pl.when(i == 0)
    def _():
        for cp in weight_copies(be_ref[0], 0):
            cp.start(priority=1)

    @pl.when(jnp.logical_and(i < nb_ref[0], first_ref[i] == 1))
    def _():
        s = slot_ref[i]
        for cp in weight_copies(be_ref[i], s):
            cp.wait()

        @pl.when(next_ref[i] >= 0)
        def _():
            for cp in weight_copies(next_ref[i], 1 - s):
                cp.start(priority=1)

        w1b[...] = w1f[s].astype(BF16)
        w3b[...] = w3f[s].astype(BF16)
        w2b[...] = w2f[s].astype(BF16)

    @pl.when(i < nb_ref[0])
    def _():
        xb = _load_token_rows(xs_ref, valid=bv_ref[i])
        a = _dot_blocks(xb, w1b)
        b = _dot_blocks(xb, w3b)
        hmid = (a * jax.nn.sigmoid(a) * b).astype(BF16)
        _store_token_rows(y_ref, _dot(hmid, w2b[...]))


def _experts(block_e, block_valid, nb_used, xs, w1, w3, w2, rb):
    _, nrows, W = xs.shape
    nb = nrows // rb
    E, D, FF = w1.shape
    idx = jnp.arange(nb, dtype=jnp.int32)
    first = (idx < nb_used[0]) & (block_e != jnp.concatenate([jnp.full((1,), -1, jnp.int32), block_e[:-1]]))
    slot = (jnp.cumsum(first.astype(jnp.int32)) - 1) % 2
    first_pos = jnp.where(first, idx, nb)
    next_pos = jnp.concatenate([lax.cummin(first_pos, reverse=True)[1:], jnp.full((1,), nb, jnp.int32)])
    next_e = jnp.where(next_pos < nb, block_e[jnp.minimum(next_pos, nb - 1)], -1).astype(jnp.int32)
    rows = pl.BlockSpec((2, rb, W), lambda i, be, bv, n, fi, sl, nx: (0, jnp.minimum(i, n[0] - 1), 0))
    grid_spec = pltpu.PrefetchScalarGridSpec(
        num_scalar_prefetch=6,
        grid=(nb,),
        in_specs=[rows, pl.BlockSpec(memory_space=pl.ANY), pl.BlockSpec(memory_space=pl.ANY),
                  pl.BlockSpec(memory_space=pl.ANY)],
        out_specs=rows,
        scratch_shapes=[pltpu.VMEM((2, D, FF), F32), pltpu.VMEM((2, D, FF), F32), pltpu.VMEM((2, FF, D), F32),
                        pltpu.VMEM((D, FF), BF16), pltpu.VMEM((D, FF), BF16), pltpu.VMEM((FF, D), BF16),
                        pltpu.SemaphoreType.DMA((2,))],
    )
    return pl.pallas_call(
        _expert_kernel,
        grid_spec=grid_spec,
        out_shape=jax.ShapeDtypeStruct((2, nrows, W), jnp.uint32),
        compiler_params=_cparams(("arbitrary",)),
        name="experts",
    )(block_e, block_valid, nb_used, first.astype(jnp.int32), slot.astype(jnp.int32), next_e, xs, w1, w3, w2)


def _combine_kernel(yg_ref, gw_ref, u_ref, x1_ref, mod_ref, s1_ref, s3_ref, s2_ref, lg_ref, lb_ref, o_ref):
    ub = _load_token_rows(u_ref)
    a = _dot_blocks(ub, s1_ref)
    b = _dot_blocks(ub, s3_ref)
    ffn = _dot((a * jax.nn.sigmoid(a) * b).astype(BF16), s2_ref[...])
    gw = gw_ref[...]
    routed = None
    for k in range(TOP_K):
        yk = [gw[:, k:k + 1] * blk.astype(F32) for blk in _load_token_rows(yg_ref.at[k])]
        routed = yk if routed is None else [r + y for r, y in zip(routed, yk)]
    ffn = ffn + jnp.concatenate(routed, axis=1)
    g2 = mod_ref[5:6, :]
    o_ref[...] = _ln(DN_ALPHA * x1_ref[...] + g2 * ffn) * lg_ref[...] + lb_ref[...]


def _combine(yg, gw, u2p, x1, mod, s1, s3, s2, ln_g, ln_b, tm):
    B, S, D = x1.shape
    K = yg.shape[0]
    row = lambda n: pl.BlockSpec((None, tm, n), lambda b, i: (b, i, 0))
    cst = lambda a: pl.BlockSpec(a.shape, lambda b, i: (0, 0))
    return pl.pallas_call(
        _combine_kernel,
        grid=(B, S // tm),
        in_specs=[pl.BlockSpec((K, 2, None, tm, D // 4), lambda b, i: (0, 0, b, i, 0)),
                  row(K), pl.BlockSpec((2, None, tm, D // 4), lambda b, i: (0, b, i, 0)), row(D),
                  pl.BlockSpec((None, 6, D), lambda b, i: (b, 0, 0)),
                  cst(s1), cst(s3), cst(s2), cst(ln_g), cst(ln_b)],
        out_specs=row(D),
        out_shape=jax.ShapeDtypeStruct((B, S, D), F32),
        compiler_params=_cparams(("arbitrary", "arbitrary")),
        name="combine",
    )(yg, gw, u2p, x1, mod, s1, s3, s2, ln_g, ln_b)


def _block_layout(counts, T, rb):
    counts = counts.reshape(-1).astype(jnp.int32)
    pcounts = (counts + rb - 1) // rb * rb
    pend = jnp.cumsum(pcounts)
    poffsets = pend - pcounts
    nb = (T * TOP_K) // rb + N_EXPERTS
    starts = jnp.arange(nb, dtype=jnp.int32) * rb
    block_e = jnp.minimum(jnp.sum(starts[:, None] >= pend[None, :], axis=1), N_EXPERTS - 1).astype(jnp.int32)
    nb_used = (pend[-1] // rb).astype(jnp.int32).reshape(1)
    mine = block_e[:, None] == jnp.arange(N_EXPERTS, dtype=jnp.int32)[None, :]
    seg_start = jnp.sum(jnp.where(mine, poffsets[None, :], 0), axis=1)
    seg_count = jnp.sum(jnp.where(mine, counts[None, :], 0), axis=1)
    block_valid = jnp.clip(seg_count - (starts - seg_start), 0, rb).astype(jnp.int32)
    return poffsets, block_e, block_valid, nb_used, nb * rb


def kernel(x, c, w_ada, b_ada, w_in, ml_conv_w, ml_conv_b, ml_gate_b, ml_norm_g, w_out, ln1_g, ln1_b,
           w_router, router_bias, moe_w1, moe_w3, moe_w2, sh_w1, sh_w3, sh_w2, ln2_g, ln2_b):
    B, S, D = x.shape
    T = B * S
    SBW = SB_HEADS * SB_HEAD_DIM
    MLW = ML_HEADS * ML_HEAD_DIM
    rb = EXPERT_BLOCK_ROWS
    tiles = _tiles(S)
    for l in range(DEPTH):
        mod = _adaln(c, w_ada[l], b_ada[l]).reshape(B, 6, D)

        wi = w_in[l]
        c0 = 3 * SBW
        w_sb = wi[:, :c0].astype(BF16)
        w_mqk = wi[:, c0:c0 + 2 * MLW].astype(BF16)
        w_mvo = wi[:, c0 + 2 * MLW:c0 + 4 * MLW].astype(BF16)
        w_g = jnp.pad(wi[:, c0 + 4 * MLW:], ((0, 0), (0, LANES - 2 * ML_HEADS))).astype(BF16)
        sbp, mqk, mvo, gates = _inproj(x, mod, w_sb, w_mqk, w_mvo, w_g, tm=tiles["proj"])

        sb = _sb_attention(sbp, tq=tiles["sb"])

        gate_b = jnp.pad(ml_gate_b[l], (0, LANES - 2 * ML_HEADS)).reshape(1, LANES)
        ml = _mlstm(mqk, mvo, gates, ml_conv_w[l], ml_conv_b[l].reshape(1, -1), gate_b,
                    ml_norm_g[l].reshape(1, -1), L=tiles["ml"])

        wo = w_out[l].astype(BF16)
        wr_parts = jnp.stack(_split3(w_router[l].T)[:2])
        x1, u2p, logits_t = _outproj(sb, ml, x, mod, wo[:SBW], wo[SBW:], ln1_g[l].reshape(1, D),
                                     ln1_b[l].reshape(1, D), wr_parts, tm=tiles["proj"])

        e_t, r_t, g_t, counts = _route(logits_t, router_bias[l], tm=tiles["proj"])
        poffsets, block_e, block_valid, nb_used, nrows = _block_layout(counts, T, rb)
        dest_t = _dest(e_t, r_t, poffsets.astype(F32).reshape(N_EXPERTS, 1), tm=tiles["proj"])
        gw = g_t.T
        idx2 = jnp.concatenate([dest_t, dest_t + nrows], axis=1)
        xs = _sc_dispatch(u2p.reshape(2 * T, D // 4), idx2, 2 * nrows).reshape(2, nrows, D // 4)
        ys = _experts(block_e, block_valid, nb_used, xs, moe_w1[l], moe_w3[l], moe_w2[l], rb)
        yg = _sc_gather(ys.reshape(2 * nrows, D // 4), idx2.reshape(1, TOP_K * 2 * T))
        yg = yg.reshape(TOP_K, 2, B, S, D // 4)
        x = _combine(yg, gw.reshape(B, S, TOP_K), u2p, x1, mod, sh_w1[l].astype(BF16),
                     sh_w3[l].astype(BF16), sh_w2[l].astype(BF16), ln2_g[l].reshape(1, D),
                     ln2_b[l].reshape(1, D), tm=tiles["combine"])
    return x
```

```python
import functools
import math

import jax
import jax.numpy as jnp
from jax import lax
from jax.experimental import pallas as pl
from jax.experimental.pallas import tpu as pltpu
from jax.experimental.pallas import tpu_sc as plsc

F32 = jnp.float32
BF16 = jnp.bfloat16
HIGHEST = lax.Precision.HIGHEST

SB_HEADS = 8
SB_HEAD_DIM = 64
ML_HEADS = 4
ML_HEAD_DIM = 128
CONV_K = 4
N_EXPERTS = 256
TOP_K = 8
N_GROUPS = 8
TOPK_GROUPS = 4
GROUP_SIZE = N_EXPERTS // N_GROUPS
ROUTED_SCALE = 2.5
EXPERT_BLOCK_ROWS = 512
SC_WINDOW = 128
DEPTH = 1
DN_ALPHA = (2 * DEPTH) ** 0.25
LN_EPS = 1e-5
LANES = 128
NEG_INF = float("-inf")
SB_CUTOFF = 104.0
SB_QBLOCKS = 8
ML_SEQS_PER_STEP = 4
VMEM_LIMIT = 56 * 1024 * 1024

PROJ_ROWS = 512
SB_ROWS = 256
ML_CHUNK_ROWS = 256
COMBINE_ROWS = 256


def _tiles(S):
    return dict(proj=min(PROJ_ROWS, S), sb=min(SB_ROWS, S // SB_QBLOCKS), ml=min(ML_CHUNK_ROWS, S),
                combine=min(COMBINE_ROWS, S))


def _cparams(sem):
    return pltpu.CompilerParams(dimension_semantics=sem, vmem_limit_bytes=VMEM_LIMIT)


def _ln(x):
    mu = jnp.mean(x, axis=-1, keepdims=True)
    xc = x - mu
    var = jnp.mean(xc * xc, axis=-1, keepdims=True)
    return xc * lax.rsqrt(var + LN_EPS)


def _dot(a, b):
    return jnp.dot(a, b, preferred_element_type=F32)


def _dot_nt(a, b):
    return lax.dot_general(a, b, (((1,), (1,)), ((), ())), preferred_element_type=F32)


def _dot_tn(a, b):
    return lax.dot_general(a, b, (((0,), (0,)), ((), ())), preferred_element_type=F32)


def _pack_halves(v):
    w = v.shape[1] // 2
    lo = lax.bitcast_convert_type(v[:, :w].astype(BF16).astype(F32), jnp.uint32) >> 16
    hi = lax.bitcast_convert_type(v[:, w:].astype(BF16).astype(F32), jnp.uint32) & jnp.uint32(0xFFFF0000)
    return hi | lo


def _unpack_halves(p):
    lo = lax.bitcast_convert_type(p << 16, F32).astype(BF16)
    hi = lax.bitcast_convert_type(p & jnp.uint32(0xFFFF0000), F32).astype(BF16)
    return lo, hi


def _store_token_rows(ref, v):
    p = _pack_halves(v)
    q = p.shape[1] // 2
    ref[0] = p[:, :q]
    ref[1] = p[:, q:]


def _load_token_rows(ref, valid=None):
    first, second = ref[0], ref[1]
    if valid is not None:
        row = lax.broadcasted_iota(jnp.int32, first.shape, 0)
        first = jnp.where(row < valid, first, jnp.uint32(0))
        second = jnp.where(row < valid, second, jnp.uint32(0))
    lo_a, hi_a = _unpack_halves(first)
    lo_b, hi_b = _unpack_halves(second)
    return [lo_a, lo_b, hi_a, hi_b]


def _dot_blocks(blocks, w_ref):
    q = blocks[0].shape[1]
    acc = _dot(blocks[0], w_ref[0:q, :])
    for i in range(1, len(blocks)):
        acc = acc + _dot(blocks[i], w_ref[i * q:(i + 1) * q, :])
    return acc


def _adaln_kernel(c_ref, w_ref, b_ref, o_ref):
    c = c_ref[...]
    s = c * jax.nn.sigmoid(c)
    o_ref[...] = jnp.dot(s, w_ref[...], preferred_element_type=F32, precision=HIGHEST) + b_ref[...]


def _adaln(c, w_ada, b_ada):
    B, D = c.shape
    N = w_ada.shape[1]
    tn = 1024
    return pl.pallas_call(
        _adaln_kernel,
        grid=(N // tn,),
        in_specs=[pl.BlockSpec((B, D), lambda j: (0, 0)),
                  pl.BlockSpec((D, tn), lambda j: (0, j)),
                  pl.BlockSpec((1, tn), lambda j: (0, j))],
        out_specs=pl.BlockSpec((B, tn), lambda j: (0, j)),
        out_shape=jax.ShapeDtypeStruct((B, N), F32),
        compiler_params=_cparams(("arbitrary",)),
        name="adaln",
    )(c, w_ada, b_ada.reshape(1, N))


def _inproj_kernel(x_ref, mod_ref, wsb_ref, wqk_ref, wvo_ref, wg_ref, cw_ref, cb_ref,
                   sb_ref, mqk_ref, mvo_ref, g_ref, xbuf):
    tm = x_ref.shape[0]
    y = _ln(x_ref[...])
    sh = mod_ref[0:1, :]
    sc = mod_ref[1:2, :]
    u = (y * (1.0 + sc) + sh).astype(BF16)

    @pl.when(pl.program_id(1) == 0)
    def _():
        xbuf[0:8, :] = jnp.zeros((8, xbuf.shape[1]), F32)

    xbuf[8:8 + tm, :] = _dot(u, wqk_ref[...])
    sb_ref[...] = _dot(u, wsb_ref[...]).astype(BF16)
    mvo_ref[...] = _dot(u, wvo_ref[...]).astype(BF16)
    g_ref[...] = _dot(u, wg_ref[...])
    y = cb_ref[...] + cw_ref[CONV_K - 1:CONV_K, :] * xbuf[8:8 + tm, :]
    for j in range(1, CONV_K):
        y = y + cw_ref[CONV_K - 1 - j:CONV_K - j, :] * xbuf[8 - j:8 - j + tm, :]
    xbuf[0:8, :] = xbuf[tm:tm + 8, :]
    qk = y * jax.nn.sigmoid(y)
    w = qk.shape[1] // 2
    mqk_ref[:, :w] = qk[:, :w].astype(BF16)
    mqk_ref[:, w:] = (qk[:, w:] * (ML_HEAD_DIM ** -0.5)).astype(BF16)


def _inproj(x, mod, w_sb, w_mqk, w_mvo, w_g, conv_w, conv_b, tm):
    B, S, D = x.shape
    nsb, nqk, nvo, ng = w_sb.shape[1], w_mqk.shape[1], w_mvo.shape[1], w_g.shape[1]
    row = lambda n: pl.BlockSpec((None, tm, n), lambda b, i: (b, i, 0))
    full = lambda n: pl.BlockSpec((D, n), lambda b, i: (0, 0))
    cst = lambda a: pl.BlockSpec(a.shape, lambda b, i: (0, 0))
    return pl.pallas_call(
        _inproj_kernel,
        grid=(B, S // tm),
        in_specs=[row(D), pl.BlockSpec((None, 6, D), lambda b, i: (b, 0, 0)),
                  full(nsb), full(nqk), full(nvo), full(ng), cst(conv_w), cst(conv_b)],
        out_specs=[row(nsb), row(nqk), row(nvo), row(ng)],
        out_shape=[jax.ShapeDtypeStruct((B, S, nsb), BF16),
                   jax.ShapeDtypeStruct((B, S, nqk), BF16),
                   jax.ShapeDtypeStruct((B, S, nvo), BF16),
                   jax.ShapeDtypeStruct((B, S, ng), F32)],
        scratch_shapes=[pltpu.VMEM((tm + 8, nqk), F32)],
        compiler_params=_cparams(("arbitrary", "arbitrary")),
        name="inproj",
    )(x, mod, w_sb, w_mqk, w_mvo, w_g, conv_w, conv_b)


def _sb_block(z):
    sp = jnp.log(1.0 + jnp.exp(-jnp.abs(z)))
    log_beta = jnp.minimum(z, 0.0) - sp
    return log_beta, log_beta - z


def _sb_kernel(q_ref, k_ref, v_ref, o_ref, *, tq, scale):
    first = pl.program_id(2) * SB_QBLOCKS
    lane = lax.broadcasted_iota(jnp.int32, (1, LANES), 1)
    r = lax.broadcasted_iota(jnp.int32, (tq, tq), 0)
    c = lax.broadcasted_iota(jnp.int32, (tq, tq), 1)
    upper = (r > c).astype(BF16)
    strict = c < r
    hmasks = [(lane // SB_HEAD_DIM) == h for h in range(2)]
    chains = [(u, h) for u in range(SB_QBLOCKS) for h in range(2)]
    qs = {}
    for u in range(SB_QBLOCKS):
        q2 = q_ref[u * tq:(u + 1) * tq, :]
        for h in range(2):
            qs[u, h] = jnp.where(hmasks[h], q2 * scale, jnp.zeros_like(q2))

    def block(qh, kblk, vblk, carry, masked):
        z = _dot_nt(qh, kblk)
        log_beta, log_1m = _sb_block(z)
        if masked:
            log_1m = jnp.where(strict, log_1m, 0.0)
        after = _dot(log_1m.astype(BF16), upper)
        a = jnp.exp(log_beta + after + carry)
        if masked:
            a = jnp.where(strict, a, 0.0)
        pv = _dot(a.astype(BF16), vblk)
        return pv, carry + jnp.sum(log_1m, axis=1, keepdims=True)

    def sweep(back, accs, carries, masked):
        new_accs, new_carries = dict(accs), dict(carries)
        for u in range(SB_QBLOCKS):
            kb = first + u - back
            off = pl.multiple_of(jnp.maximum(kb, 0) * tq, tq)
            kblk = k_ref[pl.ds(off, tq), :]
            vblk = v_ref[pl.ds(off, tq), :]
            for h in range(2):
                pv, carry = block(qs[u, h], kblk, vblk, carries[u, h], masked)
                new_accs[u, h] = jnp.where(kb >= 0, accs[u, h] + pv, accs[u, h])
                new_carries[u, h] = jnp.where(kb >= 0, carry, carries[u, h])
        return new_accs, new_carries

    accs = {ch: jnp.zeros((tq, LANES), F32) for ch in chains}
    carries = {ch: jnp.zeros((tq, 1), F32) for ch in chains}
    accs, carries = sweep(0, accs, carries, True)

    def top_of(carries, back):
        top = jnp.float32(NEG_INF)
        for u in range(SB_QBLOCKS):
            t = jnp.max(jnp.maximum(carries[u, 0], carries[u, 1]))
            top = jnp.maximum(top, jnp.where(first + u - back >= 0, t, NEG_INF))
        return top

    def cond(st):
        _, _, _, top = st
        return top > -SB_CUTOFF

    def body(st):
        back, accs, carries, _ = st
        accs, carries = sweep(back, accs, carries, False)
        return back + 1, accs, carries, top_of(carries, back + 1)

    _, accs, _, _ = lax.while_loop(cond, body, (jnp.int32(1), accs, carries, top_of(carries, 1)))
    for u in range(SB_QBLOCKS):
        o_ref[u * tq:(u + 1) * tq, :] = jnp.where(hmasks[0], accs[u, 0], accs[u, 1]).astype(o_ref.dtype)


def _sb_attention(sbp, tq):
    B, S, W3 = sbp.shape
    W = W3 // 3
    npair = W // LANES
    ts = SB_QBLOCKS * tq
    kern = functools.partial(_sb_kernel, tq=tq, scale=SB_HEAD_DIM ** -0.5)
    return pl.pallas_call(
        kern,
        grid=(B, npair, S // ts),
        in_specs=[pl.BlockSpec((None, ts, LANES), lambda b, p, i: (b, i, p)),
                  pl.BlockSpec((None, S, LANES), lambda b, p, i: (b, 0, npair + p)),
                  pl.BlockSpec((None, S, LANES), lambda b, p, i: (b, 0, 2 * npair + p))],
        out_specs=pl.BlockSpec((None, ts, LANES), lambda b, p, i: (b, i, p)),
        out_shape=jax.ShapeDtypeStruct((B, S, W), BF16),
        compiler_params=_cparams(("arbitrary", "arbitrary", "arbitrary")),
        name="sb_attention",
    )(sbp, sbp, sbp)


def _mlstm_kernel(qk_ref, vo_ref, g_ref, gb_ref, ng_ref, o_ref, ct_ref, m_ref, *, L):
    @pl.when(pl.program_id(1) == 0)
    def _():
        ct_ref[...] = jnp.zeros_like(ct_ref)
        m_ref[...] = jnp.zeros_like(m_ref)

    for b in range(qk_ref.shape[0]):
        _mlstm_chunk(qk_ref.at[b], vo_ref.at[b], g_ref.at[b], gb_ref, ng_ref, o_ref.at[b],
                     ct_ref.at[b], m_ref.at[b], L)


def _mlstm_chunk(qk_ref, vo_ref, g_ref, gb_ref, ng_ref, o_ref, ct_ref, m_ref, L):
    H, dk = ML_HEADS, ML_HEAD_DIM
    W = H * dk

    g = g_ref[...] + gb_ref[...]
    logf = jax.nn.log_sigmoid(g)
    r = lax.broadcasted_iota(jnp.int32, (L, L), 0)
    c = lax.broadcasted_iota(jnp.int32, (L, L), 1)
    causal = c <= r
    tri = causal.astype(BF16)
    lf_hi = logf.astype(BF16)
    lf_lo = (logf - lf_hi.astype(F32)).astype(BF16)
    bc = _dot(tri, lf_hi) + _dot(tri, lf_lo)
    g_t = g.T
    bc_t = bc.T
    e0 = (lax.broadcasted_iota(jnp.int32, (L, LANES), 1) == 0).astype(F32)

    for h in range(H):
        qh = qk_ref[:, h * dk:(h + 1) * dk]
        kh = qk_ref[:, W + h * dk:W + (h + 1) * dk]
        vh = vo_ref[:, h * dk:(h + 1) * dk].astype(F32)
        oh = vo_ref[:, W + h * dk:W + (h + 1) * dk].astype(F32)
        vaug = jnp.concatenate([vh, e0], axis=1)
        ic_col = g[:, h:h + 1]
        ic_row = g_t[h:h + 1, :]
        bc_col = bc[:, H + h:H + h + 1]
        bc_row = bc_t[H + h:H + h + 1, :]
        m = m_ref[h][:, 0:1]
        ct = ct_ref[h]

        log_d = jnp.where(causal, bc_col - bc_row + ic_row, NEG_INF)
        inter = bc_col + m
        m_t = jnp.maximum(inter, jnp.max(log_d, axis=1, keepdims=True))
        w = _dot_nt(qh, kh) * jnp.exp(log_d - m_t)
        s_inter = jnp.exp(inter - m_t)
        tot = s_inter * _dot(qh, ct.astype(BF16)) + _dot(w.astype(BF16), vaug.astype(BF16))
        num = tot[:, :dk]
        den = tot[:, dk:dk + 1]
        hh = num / jnp.maximum(jnp.abs(den), jnp.exp(-m_t))

        b_last = bc_col[L - 1:L, :]
        log_w = b_last - bc_col + ic_col
        m_new = jnp.maximum(b_last + m, jnp.max(log_w, axis=0, keepdims=True))
        wk = jnp.exp(log_w - m_new)
        decay = jnp.exp(b_last + m - m_new)
        ct_ref[h] = decay * ct + _dot_tn(kh, (wk * vaug).astype(BF16))
        m_ref[h] = jnp.broadcast_to(m_new, (1, LANES))

        hn = _ln(hh) * ng_ref[:, h * dk:(h + 1) * dk]
        o_ref[:, h * dk:(h + 1) * dk] = (jax.nn.sigmoid(oh) * hn).astype(o_ref.dtype)


def _mlstm(mqk, mvo, gates, gate_b, norm_g, L):
    B, S, W2 = mqk.shape
    W = W2 // 2
    kern = functools.partial(_mlstm_kernel, L=L)
    nb = math.gcd(B, ML_SEQS_PER_STEP)
    row = lambda n: pl.BlockSpec((nb, L, n), lambda b, i: (b, i, 0))
    cst = lambda a: pl.BlockSpec(a.shape, lambda b, i: (0, 0))
    return pl.pallas_call(
        kern,
        grid=(B // nb, S // L),
        in_specs=[row(W2), row(W2), row(LANES), cst(gate_b), cst(norm_g)],
        out_specs=row(W),
        out_shape=jax.ShapeDtypeStruct((B, S, W), BF16),
        scratch_shapes=[pltpu.VMEM((nb, ML_HEADS, ML_HEAD_DIM, 2 * ML_HEAD_DIM), F32),
                        pltpu.VMEM((nb, ML_HEADS, 1, LANES), F32)],
        compiler_params=_cparams(("arbitrary", "arbitrary")),
        name="mlstm",
    )(mqk, mvo, gates, gate_b, norm_g)


def _keep_bf16_bits(v):
    bits = lax.bitcast_convert_type(v, jnp.uint32) & jnp.uint32(0xFFFF0000)
    return lax.bitcast_convert_type(bits, F32)


def _split3(v):
    hi = _keep_bf16_bits(v)
    r = v - hi
    mid = _keep_bf16_bits(r)
    lo = r - mid
    return hi.astype(BF16), mid.astype(BF16), lo.astype(BF16)


def _outproj_kernel(sb_ref, ml_ref, x_ref, mod_ref, wo1_ref, wo2_ref, lg_ref, lb_ref, wr_ref,
                    x1_ref, u2_ref, lo_ref):
    mix = _dot(sb_ref[...], wo1_ref[...]) + _dot(ml_ref[...], wo2_ref[...])
    g1 = mod_ref[2:3, :]
    x1 = _ln(DN_ALPHA * x_ref[...] + g1 * mix) * lg_ref[...] + lb_ref[...]
    x1_ref[...] = x1
    u2 = _ln(x1) * (1.0 + mod_ref[4:5, :]) + mod_ref[3:4, :]
    _store_token_rows(u2_ref, u2)
    u_hi, u_mid, _ = _split3(u2)
    w_hi, w_mid = wr_ref[0], wr_ref[1]
    lo_ref[...] = _dot_nt(w_hi, u_hi) + (_dot_nt(w_mid, u_hi) + _dot_nt(w_hi, u_mid))


def _outproj(sb, ml, x, mod, wo1, wo2, ln_g, ln_b, wr_parts, tm):
    B, S, D = x.shape
    W = sb.shape[2]
    E = wr_parts.shape[1]
    nt = S // tm
    row = lambda n: pl.BlockSpec((None, tm, n), lambda b, i: (b, i, 0))
    cst = lambda a: pl.BlockSpec(a.shape, lambda b, i: (0, 0))
    return pl.pallas_call(
        _outproj_kernel,
        grid=(B, nt),
        in_specs=[row(W), row(W), row(D), pl.BlockSpec((None, 6, D), lambda b, i: (b, 0, 0)),
                  cst(wo1), cst(wo2), cst(ln_g), cst(ln_b),
                  pl.BlockSpec(wr_parts.shape, lambda b, i: (0, 0, 0))],
        out_specs=[row(D), pl.BlockSpec((2, None, tm, D // 4), lambda b, i: (0, b, i, 0)),
                   pl.BlockSpec((E, tm), lambda b, i: (0, b * nt + i))],
        out_shape=[jax.ShapeDtypeStruct((B, S, D), F32),
                   jax.ShapeDtypeStruct((2, B, S, D // 4), jnp.uint32),
                   jax.ShapeDtypeStruct((E, B * S), F32)],
        compiler_params=_cparams(("arbitrary", "arbitrary")),
        name="outproj",
    )(sb, ml, x, mod, wo1, wo2, ln_g, ln_b, wr_parts)


def _rows_to_block(rows, dtype):
    n = rows[0].shape[1]
    rid = lax.broadcasted_iota(jnp.int32, (len(rows), n), 0)
    out = jnp.zeros((len(rows), n), dtype)
    for k, v in enumerate(rows):
        out = jnp.where(rid == k, v.astype(dtype), out)
    return out


def _route_kernel(lo_ref, rb_ref, e_ref, r_ref, g_ref, cnt_ref, cnt_scr):
    E, tm = lo_ref.shape

    @pl.when(pl.program_id(0) == 0)
    def _():
        cnt_scr[...] = jnp.zeros_like(cnt_scr)

    scores = jax.nn.sigmoid(lo_ref[...])
    sel = scores + rb_ref[...]
    row_f = lax.broadcasted_iota(jnp.int32, (E, tm), 0).astype(F32)
    groups = [sel[g * GROUP_SIZE:(g + 1) * GROUP_SIZE, :] for g in range(N_GROUPS)]
    gscore = []
    for xg in groups:
        m1 = jnp.max(xg, axis=0, keepdims=True)
        is_max = xg == m1
        cnt = jnp.sum(is_max.astype(F32), axis=0, keepdims=True)
        m2 = jnp.max(jnp.where(is_max, NEG_INF, xg), axis=0, keepdims=True)
        gscore.append(m1 + jnp.where(cnt >= 2.0, m1, m2))
    kept = []
    for g in range(N_GROUPS):
        rank = jnp.zeros((1, tm), jnp.int32)
        for o in range(N_GROUPS):
            if o != g:
                beats = (gscore[o] >= gscore[g]) if o < g else (gscore[o] > gscore[g])
                rank = rank + beats.astype(jnp.int32)
        kept.append(jnp.where(rank < TOPK_GROUPS, groups[g], NEG_INF))
    cur = jnp.concatenate(kept, axis=0)
    idxs = []
    chosen = jnp.zeros((E, tm), F32)
    for k in range(TOP_K):
        m = jnp.max(cur, axis=0, keepdims=True)
        idx = jnp.min(jnp.where(cur == m, row_f, float(E)), axis=0, keepdims=True)
        pick = row_f == idx
        cur = jnp.where(pick, NEG_INF, cur)
        chosen = jnp.where(pick, 1.0, chosen)
        idxs.append(idx)
    r = lax.broadcasted_iota(jnp.int32, (tm, tm), 0)
    c = lax.broadcasted_iota(jnp.int32, (tm, tm), 1)
    before = _dot(chosen.astype(BF16), (r < c).astype(BF16)) + cnt_scr[...]
    gates, ranks = [], []
    for k in range(TOP_K):
        pick = row_f == idxs[k]
        gates.append(jnp.sum(jnp.where(pick, scores, 0.0), axis=0, keepdims=True))
        ranks.append(jnp.sum(jnp.where(pick, before, 0.0), axis=0, keepdims=True))
    gsum = gates[0]
    for gk in gates[1:]:
        gsum = gsum + gk
    e_ref[...] = _rows_to_block(idxs, jnp.int32)
    r_ref[...] = _rows_to_block(ranks, jnp.int32)
    g_ref[...] = _rows_to_block(gates, F32) / gsum * ROUTED_SCALE
    cnt_scr[...] = cnt_scr[...] + jnp.sum(chosen, axis=1, keepdims=True)
    cnt_ref[...] = cnt_scr[...]


def _route(logits_t, router_bias, tm):
    E, T = logits_t.shape
    tok = lambda dt: jax.ShapeDtypeStruct((TOP_K, T), dt)
    return pl.pallas_call(
        _route_kernel,
        grid=(T // tm,),
        in_specs=[pl.BlockSpec((E, tm), lambda i: (0, i)), pl.BlockSpec((E, 1), lambda i: (0, 0))],
        out_specs=[pl.BlockSpec((TOP_K, tm), lambda i: (0, i))] * 3 + [pl.BlockSpec((E, 1), lambda i: (0, 0))],
        out_shape=[tok(jnp.int32), tok(jnp.int32), tok(F32), jax.ShapeDtypeStruct((E, 1), F32)],
        scratch_shapes=[pltpu.VMEM((E, 1), F32)],
        compiler_params=_cparams(("arbitrary",)),
        name="route",
    )(logits_t, router_bias.reshape(E, 1))


def _dest_kernel(e_ref, r_ref, off_ref, d_ref):
    E = off_ref.shape[0]
    tm = e_ref.shape[1]
    row = lax.broadcasted_iota(jnp.int32, (E, tm), 0)
    off = off_ref[...]
    base = [jnp.sum(jnp.where(row == e_ref[k:k + 1, :], off, 0.0), axis=0, keepdims=True) for k in range(TOP_K)]
    d_ref[...] = _rows_to_block(base, jnp.int32) + r_ref[...]


def _dest(e_t, r_t, offsets, tm):
    K, T = e_t.shape
    E = offsets.shape[0]
    tok = pl.BlockSpec((K, tm), lambda i: (0, i))
    return pl.pallas_call(
        _dest_kernel,
        grid=(T // tm,),
        in_specs=[tok, tok, pl.BlockSpec((E, 1), lambda i: (0, 0))],
        out_specs=tok,
        out_shape=jax.ShapeDtypeStruct((K, T), jnp.int32),
        compiler_params=_cparams(("arbitrary",)),
        name="dest",
    )(e_t, r_t, offsets)


def _sc_mesh():
    return plsc.VectorSubcoreMesh(core_axis_name="core", subcore_axis_name="subcore")


def _sc_dispatch(u2p, dest_t, nrows):
    T, W = u2p.shape
    K = dest_t.shape[0]
    win = SC_WINDOW

    @pl.kernel(out_type=jax.ShapeDtypeStruct((nrows, W), u2p.dtype), mesh=_sc_mesh(), scratch_types=[])
    def scatter_rows(x_hbm, i_hbm, o_hbm):
        def body(x_vmem, i_vmem):
            for k in range(K):
                pltpu.sync_copy(x_vmem, o_hbm.at[i_vmem.at[k]])

        pltpu.emit_pipeline(
            body,
            grid=(T // win,),
            in_specs=[pl.BlockSpec((win, W), lambda i: (i, 0)),
                      pl.BlockSpec((K, win), lambda i: (0, i))],
            out_specs=[],
            core_axis_name=("core", "subcore"),
            dimension_semantics=(pltpu.PARALLEL,),
        )(x_hbm, i_hbm)

    return scatter_rows(u2p, dest_t)


def _sc_gather(ys, dest_flat):
    W = ys.shape[1]
    n = dest_flat.shape[1]
    win = SC_WINDOW

    @pl.kernel(out_type=jax.ShapeDtypeStruct((n, W), ys.dtype), mesh=_sc_mesh(), scratch_types=[])
    def gather_rows(y_hbm, i_hbm, o_hbm):
        def body(i_vmem, o_vmem):
            pltpu.sync_copy(y_hbm.at[i_vmem.at[0]], o_vmem)

        pltpu.emit_pipeline(
            body,
            grid=(n // win,),
            in_specs=[pl.BlockSpec((1, win), lambda i: (0, i))],
            out_specs=[pl.BlockSpec((win, W), lambda i: (i, 0))],
            core_axis_name=("core", "subcore"),
            dimension_semantics=(pltpu.PARALLEL,),
        )(i_hbm, o_hbm)

    return gather_rows(ys, dest_flat)


def _expert_kernel(be_ref, bv_ref, nb_ref, first_ref, slot_ref, next_ref, xs_ref, w1_hbm, w3_hbm, w2_hbm,
                   y_ref, w1f, w3f, w2f, w1b, w3b, w2b, sem):
    i = pl.program_id(0)

    def weight_copies(e, s):
        return [pltpu.make_async_copy(w1_hbm.at[e], w1f.at[s], sem.at[s]),
                pltpu.make_async_copy(w3_hbm.at[e], w3f.at[s], sem.at[s]),
                pltpu.make_async_copy(w2_hbm.at[e], w2f.at[s], sem.at[s])]

    @pl.when(i == 0)
    def _():
        for cp in weight_copies(be_ref[0], 0):
            cp.start(priority=1)

    @pl.when(jnp.logical_and(i < nb_ref[0], first_ref[i] == 1))
    def _():
        s = slot_ref[i]
        for cp in weight_copies(be_ref[i], s):
            cp.wait()

        @pl.when(next_ref[i] >= 0)
        def _():
            for cp in weight_copies(next_ref[i], 1 - s):
                cp.start(priority=1)

        w1b[...] = w1f[s].astype(BF16)
        w3b[...] = w3f[s].astype(BF16)
        w2b[...] = w2f[s].astype(BF16)

    @pl.when(i < nb_ref[0])
    def _():
        xb = _load_token_rows(xs_ref, valid=bv_ref[i])
        a = _dot_blocks(xb, w1b)
        b = _dot_blocks(xb, w3b)
        hmid = (a * jax.nn.sigmoid(a) * b).astype(BF16)
        _store_token_rows(y_ref, _dot(hmid, w2b[...]))


def _experts(block_e, block_valid, nb_used, xs, w1, w3, w2, rb):
    _, nrows, W = xs.shape
    nb = nrows // rb
    E, D, FF = w1.shape
    idx = jnp.arange(nb, dtype=jnp.int32)
    first = (idx < nb_used[0]) & (block_e != jnp.concatenate([jnp.full((1,), -1, jnp.int32), block_e[:-1]]))
    slot = (jnp.cumsum(first.astype(jnp.int32)) - 1) % 2
    first_pos = jnp.where(first, idx, nb)
    next_pos = jnp.concatenate([lax.cummin(first_pos, reverse=True)[1:], jnp.full((1,), nb, jnp.int32)])
    next_e = jnp.where(next_pos < nb, block_e[jnp.minimum(next_pos, nb - 1)], -1).astype(jnp.int32)
    rows = pl.BlockSpec((2, rb, W), lambda i, be, bv, n, fi, sl, nx: (0, jnp.minimum(i, n[0] - 1), 0))
    grid_spec = pltpu.PrefetchScalarGridSpec(
        num_scalar_prefetch=6,
        grid=(nb,),
        in_specs=[rows, pl.BlockSpec(memory_space=pl.ANY), pl.BlockSpec(memory_space=pl.ANY),
                  pl.BlockSpec(memory_space=pl.ANY)],
        out_specs=rows,
        scratch_shapes=[pltpu.VMEM((2, D, FF), F32), pltpu.VMEM((2, D, FF), F32), pltpu.VMEM((2, FF, D), F32),
                        pltpu.VMEM((D, FF), BF16), pltpu.VMEM((D, FF), BF16), pltpu.VMEM((FF, D), BF16),
                        pltpu.SemaphoreType.DMA((2,))],
    )
    return pl.pallas_call(
        _expert_kernel,
        grid_spec=grid_spec,
        out_shape=jax.ShapeDtypeStruct((2, nrows, W), jnp.uint32),
        compiler_params=_cparams(("arbitrary",)),
        name="experts",
    )(block_e, block_valid, nb_used, first.astype(jnp.int32), slot.astype(jnp.int32), next_e, xs, w1, w3, w2)


def _combine_kernel(yg_ref, gw_ref, u_ref, x1_ref, mod_ref, s1_ref, s3_ref, s2_ref, lg_ref, lb_ref, o_ref):
    ub = _load_token_rows(u_ref)
    a = _dot_blocks(ub, s1_ref)
    b = _dot_blocks(ub, s3_ref)
    ffn = _dot((a * jax.nn.sigmoid(a) * b).astype(BF16), s2_ref[...])
    gw = gw_ref[...]
    routed = None
    for k in range(TOP_K):
        yk = [gw[:, k:k + 1] * blk.astype(F32) for blk in _load_token_rows(yg_ref.at[k])]
        routed = yk if routed is None else [r + y for r, y in zip(routed, yk)]
    ffn = ffn + jnp.concatenate(routed, axis=1)
    g2 = mod_ref[5:6, :]
    o_ref[...] = _ln(DN_ALPHA * x1_ref[...] + g2 * ffn) * lg_ref[...] + lb_ref[...]


def _combine(yg, gw, u2p, x1, mod, s1, s3, s2, ln_g, ln_b, tm):
    B, S, D = x1.shape
    K = yg.shape[0]
    row = lambda n: pl.BlockSpec((None, tm, n), lambda b, i: (b, i, 0))
    cst = lambda a: pl.BlockSpec(a.shape, lambda b, i: (0, 0))
    return pl.pallas_call(
        _combine_kernel,
        grid=(B, S // tm),
        in_specs=[pl.BlockSpec((K, 2, None, tm, D // 4), lambda b, i: (0, 0, b, i, 0)),
                  row(K), pl.BlockSpec((2, None, tm, D // 4), lambda b, i: (0, b, i, 0)), row(D),
                  pl.BlockSpec((None, 6, D), lambda b, i: (b, 0, 0)),
                  cst(s1), cst(s3), cst(s2), cst(ln_g), cst(ln_b)],
        out_specs=row(D),
        out_shape=jax.ShapeDtypeStruct((B, S, D), F32),
        compiler_params=_cparams(("arbitrary", "arbitrary")),
        name="combine",
    )(yg, gw, u2p, x1, mod, s1, s3, s2, ln_g, ln_b)


def _block_layout(counts, T, rb):
    counts = counts.reshape(-1).astype(jnp.int32)
    pcounts = (counts + rb - 1) // rb * rb
    pend = jnp.cumsum(pcounts)
    poffsets = pend - pcounts
    nb = (T * TOP_K) // rb + N_EXPERTS
    starts = jnp.arange(nb, dtype=jnp.int32) * rb
    block_e = jnp.minimum(jnp.sum(starts[:, None] >= pend[None, :], axis=1), N_EXPERTS - 1).astype(jnp.int32)
    nb_used = (pend[-1] // rb).astype(jnp.int32).reshape(1)
    mine = block_e[:, None] == jnp.arange(N_EXPERTS, dtype=jnp.int32)[None, :]
    seg_start = jnp.sum(jnp.where(mine, poffsets[None, :], 0), axis=1)
    seg_count = jnp.sum(jnp.where(mine, counts[None, :], 0), axis=1)
    block_valid = jnp.clip(seg_count - (starts - seg_start), 0, rb).astype(jnp.int32)
    return poffsets, block_e, block_valid, nb_used, nb * rb


def kernel(x, c, w_ada, b_ada, w_in, ml_conv_w, ml_conv_b, ml_gate_b, ml_norm_g, w_out, ln1_g, ln1_b,
           w_router, router_bias, moe_w1, moe_w3, moe_w2, sh_w1, sh_w3, sh_w2, ln2_g, ln2_b):
    B, S, D = x.shape
    T = B * S
    SBW = SB_HEADS * SB_HEAD_DIM
    MLW = ML_HEADS * ML_HEAD_DIM
    rb = EXPERT_BLOCK_ROWS
    tiles = _tiles(S)
    for l in range(DEPTH):
        mod = _adaln(c, w_ada[l], b_ada[l]).reshape(B, 6, D)

        wi = w_in[l]
        c0 = 3 * SBW
        w_sb = wi[:, :c0].astype(BF16)
        w_mqk = wi[:, c0:c0 + 2 * MLW].astype(BF16)
        w_mvo = wi[:, c0 + 2 * MLW:c0 + 4 * MLW].astype(BF16)
        w_g = jnp.pad(wi[:, c0 + 4 * MLW:], ((0, 0), (0, LANES - 2 * ML_HEADS))).astype(BF16)
        sbp, mqk, mvo, gates = _inproj(x, mod, w_sb, w_mqk, w_mvo, w_g, ml_conv_w[l],
                                       ml_conv_b[l].reshape(1, -1), tm=tiles["proj"])

        sb = _sb_attention(sbp, tq=tiles["sb"])

        gate_b = jnp.pad(ml_gate_b[l], (0, LANES - 2 * ML_HEADS)).reshape(1, LANES)
        ml = _mlstm(mqk, mvo, gates, gate_b, ml_norm_g[l].reshape(1, -1), L=tiles["ml"])

        wo = w_out[l].astype(BF16)
        wr_parts = jnp.stack(_split3(w_router[l].T)[:2])
        x1, u2p, logits_t = _outproj(sb, ml, x, mod, wo[:SBW], wo[SBW:], ln1_g[l].reshape(1, D),
                                     ln1_b[l].reshape(1, D), wr_parts, tm=tiles["proj"])

        e_t, r_t, g_t, counts = _route(logits_t, router_bias[l], tm=tiles["proj"])
        poffsets, block_e, block_valid, nb_used, nrows = _block_layout(counts, T, rb)
        dest_t = _dest(e_t, r_t, poffsets.astype(F32).reshape(N_EXPERTS, 1), tm=tiles["proj"])
        gw = g_t.T
        idx2 = jnp.concatenate([dest_t, dest_t + nrows], axis=1)
        xs = _sc_dispatch(u2p.reshape(2 * T, D // 4), idx2, 2 * nrows).reshape(2, nrows, D // 4)
        ys = _experts(block_e, block_valid, nb_used, xs, moe_w1[l], moe_w3[l], moe_w2[l], rb)
        yg = _sc_gather(ys.reshape(2 * nrows, D // 4), idx2.reshape(1, TOP_K * 2 * T))
        yg = yg.reshape(TOP_K, 2, B, S, D // 4)
        x = _combine(yg, gw.reshape(B, S, TOP_K), u2p, x1, mod, sh_w1[l].astype(BF16),
                     sh_w3[l].astype(BF16), sh_w2[l].astype(BF16), ln2_g[l].reshape(1, D),
                     ln2_b[l].reshape(1, D), tm=tiles["combine"])
    return x
```

```python
import functools
import math

import jax
import jax.numpy as jnp
from jax import lax
from jax.experimental import pallas as pl
from jax.experimental.pallas import tpu as pltpu
from jax.experimental.pallas import tpu_sc as plsc

F32 = jnp.float32
BF16 = jnp.bfloat16
HIGHEST = lax.Precision.HIGHEST

SB_HEADS = 8
SB_HEAD_DIM = 64
ML_HEADS = 4
ML_HEAD_DIM = 128
CONV_K = 4
N_EXPERTS = 256
TOP_K = 8
N_GROUPS = 8
TOPK_GROUPS = 4
GROUP_SIZE = N_EXPERTS // N_GROUPS
ROUTED_SCALE = 2.5
EXPERT_BLOCK_ROWS = 512
SC_WINDOW = 128
DEPTH = 1
DN_ALPHA = (2 * DEPTH) ** 0.25
LN_EPS = 1e-5
LANES = 128
NEG_INF = float("-inf")
SB_CUTOFF = 104.0
SB_QBLOCKS = 8
ML_SEQS_PER_STEP = 4
VMEM_LIMIT = 56 * 1024 * 1024

PROJ_ROWS = 512
SB_ROWS = 256
ML_CHUNK_ROWS = 256
COMBINE_ROWS = 512


def _tiles(S):
    return dict(proj=min(PROJ_ROWS, S), sb=min(SB_ROWS, S // SB_QBLOCKS), ml=min(ML_CHUNK_ROWS, S),
                combine=min(COMBINE_ROWS, S))


def _cparams(sem):
    return pltpu.CompilerParams(dimension_semantics=sem, vmem_limit_bytes=VMEM_LIMIT)


def _ln(x):
    mu = jnp.mean(x, axis=-1, keepdims=True)
    xc = x - mu
    var = jnp.mean(xc * xc, axis=-1, keepdims=True)
    return xc * lax.rsqrt(var + LN_EPS)


def _dot(a, b):
    return jnp.dot(a, b, preferred_element_type=F32)


def _dot_nt(a, b):
    return lax.dot_general(a, b, (((1,), (1,)), ((), ())), preferred_element_type=F32)


def _dot_tn(a, b):
    return lax.dot_general(a, b, (((0,), (0,)), ((), ())), preferred_element_type=F32)


def _pack_halves(v):
    w = v.shape[1] // 2
    lo = lax.bitcast_convert_type(v[:, :w].astype(BF16).astype(F32), jnp.uint32) >> 16
    hi = lax.bitcast_convert_type(v[:, w:].astype(BF16).astype(F32), jnp.uint32) & jnp.uint32(0xFFFF0000)
    return hi | lo


def _unpack_halves(p):
    lo = lax.bitcast_convert_type(p << 16, F32).astype(BF16)
    hi = lax.bitcast_convert_type(p & jnp.uint32(0xFFFF0000), F32).astype(BF16)
    return lo, hi


def _store_token_rows(ref, v):
    p = _pack_halves(v)
    q = p.shape[1] // 2
    ref[0] = p[:, :q]
    ref[1] = p[:, q:]


def _load_token_rows(ref, valid=None):
    first, second = ref[0], ref[1]
    if valid is not None:
        row = lax.broadcasted_iota(jnp.int32, first.shape, 0)
        first = jnp.where(row < valid, first, jnp.uint32(0))
        second = jnp.where(row < valid, second, jnp.uint32(0))
    lo_a, hi_a = _unpack_halves(first)
    lo_b, hi_b = _unpack_halves(second)
    return [lo_a, lo_b, hi_a, hi_b]


def _dot_blocks(blocks, w_ref):
    q = blocks[0].shape[1]
    acc = _dot(blocks[0], w_ref[0:q, :])
    for i in range(1, len(blocks)):
        acc = acc + _dot(blocks[i], w_ref[i * q:(i + 1) * q, :])
    return acc


def _adaln_kernel(c_ref, w_ref, b_ref, o_ref):
    c = c_ref[...]
    s = c * jax.nn.sigmoid(c)
    o_ref[...] = jnp.dot(s, w_ref[...], preferred_element_type=F32, precision=HIGHEST) + b_ref[...]


def _adaln(c, w_ada, b_ada):
    B, D = c.shape
    N = w_ada.shape[1]
    tn = 1024
    return pl.pallas_call(
        _adaln_kernel,
        grid=(N // tn,),
        in_specs=[pl.BlockSpec((B, D), lambda j: (0, 0)),
                  pl.BlockSpec((D, tn), lambda j: (0, j)),
                  pl.BlockSpec((1, tn), lambda j: (0, j))],
        out_specs=pl.BlockSpec((B, tn), lambda j: (0, j)),
        out_shape=jax.ShapeDtypeStruct((B, N), F32),
        compiler_params=_cparams(("arbitrary",)),
        name="adaln",
    )(c, w_ada, b_ada.reshape(1, N))


def _inproj_kernel(x_ref, mod_ref, wsb_ref, wqk_ref, wvo_ref, wg_ref,
                   sb_ref, mqk_ref, mvo_ref, g_ref):
    y = _ln(x_ref[...])
    sh = mod_ref[0:1, :]
    sc = mod_ref[1:2, :]
    u = (y * (1.0 + sc) + sh).astype(BF16)
    sb_ref[...] = _dot(u, wsb_ref[...]).astype(BF16)
    mqk_ref[...] = _dot(u, wqk_ref[...])
    mvo_ref[...] = _dot(u, wvo_ref[...]).astype(BF16)
    g_ref[...] = _dot(u, wg_ref[...])


def _inproj(x, mod, w_sb, w_mqk, w_mvo, w_g, tm):
    B, S, D = x.shape
    nsb, nqk, nvo, ng = w_sb.shape[1], w_mqk.shape[1], w_mvo.shape[1], w_g.shape[1]
    row = lambda n: pl.BlockSpec((None, tm, n), lambda b, i: (b, i, 0))
    full = lambda n: pl.BlockSpec((D, n), lambda b, i: (0, 0))
    return pl.pallas_call(
        _inproj_kernel,
        grid=(B, S // tm),
        in_specs=[row(D), pl.BlockSpec((None, 6, D), lambda b, i: (b, 0, 0)),
                  full(nsb), full(nqk), full(nvo), full(ng)],
        out_specs=[row(nsb), row(nqk), row(nvo), row(ng)],
        out_shape=[jax.ShapeDtypeStruct((B, S, nsb), BF16),
                   jax.ShapeDtypeStruct((B, S, nqk), F32),
                   jax.ShapeDtypeStruct((B, S, nvo), BF16),
                   jax.ShapeDtypeStruct((B, S, ng), F32)],
        compiler_params=_cparams(("arbitrary", "arbitrary")),
        name="inproj",
    )(x, mod, w_sb, w_mqk, w_mvo, w_g)


def _sb_block(z):
    sp = jnp.log(1.0 + jnp.exp(-jnp.abs(z)))
    log_beta = jnp.minimum(z, 0.0) - sp
    return log_beta, log_beta - z


def _sb_kernel(q_ref, k_ref, v_ref, o_ref, *, tq, scale):
    first = pl.program_id(2) * SB_QBLOCKS
    lane = lax.broadcasted_iota(jnp.int32, (1, LANES), 1)
    r = lax.broadcasted_iota(jnp.int32, (tq, tq), 0)
    c = lax.broadcasted_iota(jnp.int32, (tq, tq), 1)
    upper = (r > c).astype(BF16)
    strict = c < r
    hmasks = [(lane // SB_HEAD_DIM) == h for h in range(2)]
    chains = [(u, h) for u in range(SB_QBLOCKS) for h in range(2)]
    qs = {}
    for u in range(SB_QBLOCKS):
        q2 = q_ref[u * tq:(u + 1) * tq, :]
        for h in range(2):
            qs[u, h] = jnp.where(hmasks[h], q2 * scale, jnp.zeros_like(q2))

    def block(qh, kblk, vblk, carry, masked):
        z = _dot_nt(qh, kblk)
        log_beta, log_1m = _sb_block(z)
        if masked:
            log_1m = jnp.where(strict, log_1m, 0.0)
        after = _dot(log_1m.astype(BF16), upper)
        a = jnp.exp(log_beta + after + carry)
        if masked:
            a = jnp.where(strict, a, 0.0)
        pv = _dot(a.astype(BF16), vblk)
        return pv, carry + jnp.sum(log_1m, axis=1, keepdims=True)

    def sweep(back, accs, carries, masked):
        new_accs, new_carries = dict(accs), dict(carries)
        for u in range(SB_QBLOCKS):
            kb = first + u - back
            off = pl.multiple_of(jnp.maximum(kb, 0) * tq, tq)
            kblk = k_ref[pl.ds(off, tq), :]
            vblk = v_ref[pl.ds(off, tq), :]
            for h in range(2):
                pv, carry = block(qs[u, h], kblk, vblk, carries[u, h], masked)
                new_accs[u, h] = jnp.where(kb >= 0, accs[u, h] + pv, accs[u, h])
                new_carries[u, h] = jnp.where(kb >= 0, carry, carries[u, h])
        return new_accs, new_carries

    accs = {ch: jnp.zeros((tq, LANES), F32) for ch in chains}
    carries = {ch: jnp.zeros((tq, 1), F32) for ch in chains}
    accs, carries = sweep(0, accs, carries, True)

    def top_of(carries, back):
        top = jnp.float32(NEG_INF)
        for u in range(SB_QBLOCKS):
            t = jnp.max(jnp.maximum(carries[u, 0], carries[u, 1]))
            top = jnp.maximum(top, jnp.where(first + u - back >= 0, t, NEG_INF))
        return top

    def cond(st):
        _, _, _, top = st
        return top > -SB_CUTOFF

    def body(st):
        back, accs, carries, _ = st
        accs, carries = sweep(back, accs, carries, False)
        return back + 1, accs, carries, top_of(carries, back + 1)

    _, accs, _, _ = lax.while_loop(cond, body, (jnp.int32(1), accs, carries, top_of(carries, 1)))
    for u in range(SB_QBLOCKS):
        o_ref[u * tq:(u + 1) * tq, :] = jnp.where(hmasks[0], accs[u, 0], accs[u, 1]).astype(o_ref.dtype)


def _sb_attention(sbp, tq):
    B, S, W3 = sbp.shape
    W = W3 // 3
    npair = W // LANES
    ts = SB_QBLOCKS * tq
    kern = functools.partial(_sb_kernel, tq=tq, scale=SB_HEAD_DIM ** -0.5)
    return pl.pallas_call(
        kern,
        grid=(B, npair, S // ts),
        in_specs=[pl.BlockSpec((None, ts, LANES), lambda b, p, i: (b, i, p)),
                  pl.BlockSpec((None, S, LANES), lambda b, p, i: (b, 0, npair + p)),
                  pl.BlockSpec((None, S, LANES), lambda b, p, i: (b, 0, 2 * npair + p))],
        out_specs=pl.BlockSpec((None, ts, LANES), lambda b, p, i: (b, i, p)),
        out_shape=jax.ShapeDtypeStruct((B, S, W), BF16),
        compiler_params=_cparams(("arbitrary", "arbitrary", "arbitrary")),
        name="sb_attention",
    )(sbp, sbp, sbp)


def _mlstm_kernel(qk_ref, vo_ref, g_ref, cw_ref, cb_ref, gb_ref, ng_ref, o_ref,
                  xbuf, ct_ref, m_ref, *, L):
    @pl.when(pl.program_id(1) == 0)
    def _():
        xbuf[:, 0:8, :] = jnp.zeros((xbuf.shape[0], 8, xbuf.shape[2]), F32)
        ct_ref[...] = jnp.zeros_like(ct_ref)
        m_ref[...] = jnp.zeros_like(m_ref)

    for b in range(qk_ref.shape[0]):
        _mlstm_chunk(qk_ref.at[b], vo_ref.at[b], g_ref.at[b], cw_ref, cb_ref, gb_ref, ng_ref, o_ref.at[b],
                     xbuf.at[b], ct_ref.at[b], m_ref.at[b], L)


def _mlstm_chunk(qk_ref, vo_ref, g_ref, cw_ref, cb_ref, gb_ref, ng_ref, o_ref, xbuf, ct_ref, m_ref, L):
    H, dk = ML_HEADS, ML_HEAD_DIM
    W = H * dk

    xbuf[8:8 + L, :] = qk_ref[...]
    y = cb_ref[...] + cw_ref[CONV_K - 1:CONV_K, :] * xbuf[8:8 + L, :]
    for j in range(1, CONV_K):
        y = y + cw_ref[CONV_K - 1 - j:CONV_K - j, :] * xbuf[8 - j:8 - j + L, :]
    xbuf[0:8, :] = xbuf[L:L + 8, :]
    qk = y * jax.nn.sigmoid(y)

    g = g_ref[...] + gb_ref[...]
    logf = jax.nn.log_sigmoid(g)
    r = lax.broadcasted_iota(jnp.int32, (L, L), 0)
    c = lax.broadcasted_iota(jnp.int32, (L, L), 1)
    causal = c <= r
    tri = causal.astype(BF16)
    lf_hi = logf.astype(BF16)
    lf_lo = (logf - lf_hi.astype(F32)).astype(BF16)
    bc = _dot(tri, lf_hi) + _dot(tri, lf_lo)
    g_t = g.T
    bc_t = bc.T
    e0 = (lax.broadcasted_iota(jnp.int32, (L, LANES), 1) == 0).astype(F32)

    for h in range(H):
        qh = qk[:, h * dk:(h + 1) * dk].astype(BF16)
        kh = (qk[:, W + h * dk:W + (h + 1) * dk] * (dk ** -0.5)).astype(BF16)
        vh = vo_ref[:, h * dk:(h + 1) * dk].astype(F32)
        oh = vo_ref[:, W + h * dk:W + (h + 1) * dk].astype(F32)
        vaug = jnp.concatenate([vh, e0], axis=1)
        ic_col = g[:, h:h + 1]
        ic_row = g_t[h:h + 1, :]
        bc_col = bc[:, H + h:H + h + 1]
        bc_row = bc_t[H + h:H + h + 1, :]
        m = m_ref[h][:, 0:1]
        ct = ct_ref[h]

        log_d = jnp.where(causal, bc_col - bc_row + ic_row, NEG_INF)
        inter = bc_col + m
        m_t = jnp.maximum(inter, jnp.max(log_d, axis=1, keepdims=True))
        w = _dot_nt(qh, kh) * jnp.exp(log_d - m_t)
        s_inter = jnp.exp(inter - m_t)
        tot = s_inter * _dot(qh, ct.astype(BF16)) + _dot(w.astype(BF16), vaug.astype(BF16))
        num = tot[:, :dk]
        den = tot[:, dk:dk + 1]
        hh = num / jnp.maximum(jnp.abs(den), jnp.exp(-m_t))

        b_last = bc_col[L - 1:L, :]
        log_w = b_last - bc_col + ic_col
        m_new = jnp.maximum(b_last + m, jnp.max(log_w, axis=0, keepdims=True))
        wk = jnp.exp(log_w - m_new)
        decay = jnp.exp(b_last + m - m_new)
        ct_ref[h] = decay * ct + _dot_tn(kh, (wk * vaug).astype(BF16))
        m_ref[h] = jnp.broadcast_to(m_new, (1, LANES))

        hn = _ln(hh) * ng_ref[:, h * dk:(h + 1) * dk]
        o_ref[:, h * dk:(h + 1) * dk] = (jax.nn.sigmoid(oh) * hn).astype(o_ref.dtype)


def _mlstm(mqk, mvo, gates, conv_w, conv_b, gate_b, norm_g, L):
    B, S, W2 = mqk.shape
    W = W2 // 2
    kern = functools.partial(_mlstm_kernel, L=L)
    nb = math.gcd(B, ML_SEQS_PER_STEP)
    row = lambda n: pl.BlockSpec((nb, L, n), lambda b, i: (b, i, 0))
    cst = lambda a: pl.BlockSpec(a.shape, lambda b, i: (0, 0))
    return pl.pallas_call(
        kern,
        grid=(B // nb, S // L),
        in_specs=[row(W2), row(W2), row(LANES), cst(conv_w), cst(conv_b), cst(gate_b), cst(norm_g)],
        out_specs=row(W),
        out_shape=jax.ShapeDtypeStruct((B, S, W), BF16),
        scratch_shapes=[pltpu.VMEM((nb, L + 8, W2), F32),
                        pltpu.VMEM((nb, ML_HEADS, ML_HEAD_DIM, 2 * ML_HEAD_DIM), F32),
                        pltpu.VMEM((nb, ML_HEADS, 1, LANES), F32)],
        compiler_params=_cparams(("arbitrary", "arbitrary")),
        name="mlstm",
    )(mqk, mvo, gates, conv_w, conv_b, gate_b, norm_g)


def _keep_bf16_bits(v):
    bits = lax.bitcast_convert_type(v, jnp.uint32) & jnp.uint32(0xFFFF0000)
    return lax.bitcast_convert_type(bits, F32)


def _split3(v):
    hi = _keep_bf16_bits(v)
    r = v - hi
    mid = _keep_bf16_bits(r)
    lo = r - mid
    return hi.astype(BF16), mid.astype(BF16), lo.astype(BF16)


def _outproj_kernel(sb_ref, ml_ref, x_ref, mod_ref, wo1_ref, wo2_ref, lg_ref, lb_ref, wr_ref,
                    x1_ref, u2_ref, lo_ref):
    mix = _dot(sb_ref[...], wo1_ref[...]) + _dot(ml_ref[...], wo2_ref[...])
    g1 = mod_ref[2:3, :]
    x1 = _ln(DN_ALPHA * x_ref[...] + g1 * mix) * lg_ref[...] + lb_ref[...]
    x1_ref[...] = x1
    u2 = _ln(x1) * (1.0 + mod_ref[4:5, :]) + mod_ref[3:4, :]
    _store_token_rows(u2_ref, u2)
    u_hi, u_mid, _ = _split3(u2)
    w_hi, w_mid = wr_ref[0], wr_ref[1]
    lo_ref[...] = _dot_nt(w_hi, u_hi) + (_dot_nt(w_mid, u_hi) + _dot_nt(w_hi, u_mid))


def _outproj(sb, ml, x, mod, wo1, wo2, ln_g, ln_b, wr_parts, tm):
    B, S, D = x.shape
    W = sb.shape[2]
    E = wr_parts.shape[1]
    nt = S // tm
    row = lambda n: pl.BlockSpec((None, tm, n), lambda b, i: (b, i, 0))
    cst = lambda a: pl.BlockSpec(a.shape, lambda b, i: (0, 0))
    return pl.pallas_call(
        _outproj_kernel,
        grid=(B, nt),
        in_specs=[row(W), row(W), row(D), pl.BlockSpec((None, 6, D), lambda b, i: (b, 0, 0)),
                  cst(wo1), cst(wo2), cst(ln_g), cst(ln_b),
                  pl.BlockSpec(wr_parts.shape, lambda b, i: (0, 0, 0))],
        out_specs=[row(D), pl.BlockSpec((2, None, tm, D // 4), lambda b, i: (0, b, i, 0)),
                   pl.BlockSpec((E, tm), lambda b, i: (0, b * nt + i))],
        out_shape=[jax.ShapeDtypeStruct((B, S, D), F32),
                   jax.ShapeDtypeStruct((2, B, S, D // 4), jnp.uint32),
                   jax.ShapeDtypeStruct((E, B * S), F32)],
        compiler_params=_cparams(("arbitrary", "arbitrary")),
        name="outproj",
    )(sb, ml, x, mod, wo1, wo2, ln_g, ln_b, wr_parts)


def _rows_to_block(rows, dtype):
    n = rows[0].shape[1]
    rid = lax.broadcasted_iota(jnp.int32, (len(rows), n), 0)
    out = jnp.zeros((len(rows), n), dtype)
    for k, v in enumerate(rows):
        out = jnp.where(rid == k, v.astype(dtype), out)
    return out


def _route_kernel(lo_ref, rb_ref, e_ref, r_ref, g_ref, cnt_ref, cnt_scr):
    E, tm = lo_ref.shape

    @pl.when(pl.program_id(0) == 0)
    def _():
        cnt_scr[...] = jnp.zeros_like(cnt_scr)

    scores = jax.nn.sigmoid(lo_ref[...])
    sel = scores + rb_ref[...]
    row_f = lax.broadcasted_iota(jnp.int32, (E, tm), 0).astype(F32)
    groups = [sel[g * GROUP_SIZE:(g + 1) * GROUP_SIZE, :] for g in range(N_GROUPS)]
    gscore = []
    for xg in groups:
        m1 = jnp.max(xg, axis=0, keepdims=True)
        is_max = xg == m1
        cnt = jnp.sum(is_max.astype(F32), axis=0, keepdims=True)
        m2 = jnp.max(jnp.where(is_max, NEG_INF, xg), axis=0, keepdims=True)
        gscore.append(m1 + jnp.where(cnt >= 2.0, m1, m2))
    kept = []
    for g in range(N_GROUPS):
        rank = jnp.zeros((1, tm), jnp.int32)
        for o in range(N_GROUPS):
            if o != g:
                beats = (gscore[o] >= gscore[g]) if o < g else (gscore[o] > gscore[g])
                rank = rank + beats.astype(jnp.int32)
        kept.append(jnp.where(rank < TOPK_GROUPS, groups[g], NEG_INF))
    cur = jnp.concatenate(kept, axis=0)
    idxs = []
    chosen = jnp.zeros((E, tm), F32)
    for k in range(TOP_K):
        m = jnp.max(cur, axis=0, keepdims=True)
        idx = jnp.min(jnp.where(cur == m, row_f, float(E)), axis=0, keepdims=True)
        pick = row_f == idx
        cur = jnp.where(pick, NEG_INF, cur)
        chosen = jnp.where(pick, 1.0, chosen)
        idxs.append(idx)
    r = lax.broadcasted_iota(jnp.int32, (tm, tm), 0)
    c = lax.broadcasted_iota(jnp.int32, (tm, tm), 1)
    before = _dot(chosen.astype(BF16), (r < c).astype(BF16)) + cnt_scr[...]
    gates, ranks = [], []
    for k in range(TOP_K):
        pick = row_f == idxs[k]
        gates.append(jnp.sum(jnp.where(pick, scores, 0.0), axis=0, keepdims=True))
        ranks.append(jnp.sum(jnp.where(pick, before, 0.0), axis=0, keepdims=True))
    gsum = gates[0]
    for gk in gates[1:]:
        gsum = gsum + gk
    e_ref[...] = _rows_to_block(idxs, jnp.int32)
    r_ref[...] = _rows_to_block(ranks, jnp.int32)
    g_ref[...] = _rows_to_block(gates, F32) / gsum * ROUTED_SCALE
    cnt_scr[...] = cnt_scr[...] + jnp.sum(chosen, axis=1, keepdims=True)
    cnt_ref[...] = cnt_scr[...]


def _route(logits_t, router_bias, tm):
    E, T = logits_t.shape
    tok = lambda dt: jax.ShapeDtypeStruct((TOP_K, T), dt)
    return pl.pallas_call(
        _route_kernel,
        grid=(T // tm,),
        in_specs=[pl.BlockSpec((E, tm), lambda i: (0, i)), pl.BlockSpec((E, 1), lambda i: (0, 0))],
        out_specs=[pl.BlockSpec((TOP_K, tm), lambda i: (0, i))] * 3 + [pl.BlockSpec((E, 1), lambda i: (0, 0))],
        out_shape=[tok(jnp.int32), tok(jnp.int32), tok(F32), jax.ShapeDtypeStruct((E, 1), F32)],
        scratch_shapes=[pltpu.VMEM((E, 1), F32)],
        compiler_params=_cparams(("arbitrary",)),
        name="route",
    )(logits_t, router_bias.reshape(E, 1))


def _dest_kernel(e_ref, r_ref, off_ref, d_ref):
    E = off_ref.shape[0]
    tm = e_ref.shape[1]
    row = lax.broadcasted_iota(jnp.int32, (E, tm), 0)
    off = off_ref[...]
    base = [jnp.sum(jnp.where(row == e_ref[k:k + 1, :], off, 0.0), axis=0, keepdims=True) for k in range(TOP_K)]
    d_ref[...] = _rows_to_block(base, jnp.int32) + r_ref[...]


def _dest(e_t, r_t, offsets, tm):
    K, T = e_t.shape
    E = offsets.shape[0]
    tok = pl.BlockSpec((K, tm), lambda i: (0, i))
    return pl.pallas_call(
        _dest_kernel,
        grid=(T // tm,),
        in_specs=[tok, tok, pl.BlockSpec((E, 1), lambda i: (0, 0))],
        out_specs=tok,
        out_shape=jax.ShapeDtypeStruct((K, T), jnp.int32),
        compiler_params=_cparams(("arbitrary",)),
        name="dest",
    )(e_t, r_t, offsets)


def _sc_mesh():
    return plsc.VectorSubcoreMesh(core_axis_name="core", subcore_axis_name="subcore")


def _sc_dispatch(u2p, dest_t, nrows):
    T, W = u2p.shape
    K = dest_t.shape[0]
    win = SC_WINDOW

    @pl.kernel(out_type=jax.ShapeDtypeStruct((nrows, W), u2p.dtype), mesh=_sc_mesh(), scratch_types=[])
    def scatter_rows(x_hbm, i_hbm, o_hbm):
        def body(x_vmem, i_vmem):
            for k in range(K):
                pltpu.sync_copy(x_vmem, o_hbm.at[i_vmem.at[k]])

        pltpu.emit_pipeline(
            body,
            grid=(T // win,),
            in_specs=[pl.BlockSpec((win, W), lambda i: (i, 0)),
                      pl.BlockSpec((K, win), lambda i: (0, i))],
            out_specs=[],
            core_axis_name=("core", "subcore"),
            dimension_semantics=(pltpu.PARALLEL,),
        )(x_hbm, i_hbm)

    return scatter_rows(u2p, dest_t)


def _sc_gather(ys, dest_flat):
    W = ys.shape[1]
    n = dest_flat.shape[1]
    win = SC_WINDOW

    @pl.kernel(out_type=jax.ShapeDtypeStruct((n, W), ys.dtype), mesh=_sc_mesh(), scratch_types=[])
    def gather_rows(y_hbm, i_hbm, o_hbm):
        def body(i_vmem, o_vmem):
            pltpu.sync_copy(y_hbm.at[i_vmem.at[0]], o_vmem)

        pltpu.emit_pipeline(
            body,
            grid=(n // win,),
            in_specs=[pl.BlockSpec((1, win), lambda i: (0, i))],
            out_specs=[pl.BlockSpec((win, W), lambda i: (i, 0))],
            core_axis_name=("core", "subcore"),
            dimension_semantics=(pltpu.PARALLEL,),
        )(i_hbm, o_hbm)

    return gather_rows(ys, dest_flat)


def _expert_kernel(be_ref, bv_ref, nb_ref, first_ref, slot_ref, next_ref, xs_ref, w1_hbm, w3_hbm, w2_hbm,
                   y_ref, w1f, w3f, w2f, w1b, w3b, w2b, sem):
    i = pl.program_id(0)

    def weight_copies(e, s):
        return [pltpu.make_async_copy(w1_hbm.at[e], w1f.at[s], sem.at[s]),
                pltpu.make_async_copy(w3_hbm.at[e], w3f.at[s], sem.at[s]),
                pltpu.make_async_copy(w2_hbm.at[e], w2f.at[s], sem.at[s])]

    @pl.when(i == 0)
    def _():
        for cp in weight_copies(be_ref[0], 0):
            cp.start()

    @pl.when(jnp.logical_and(i < nb_ref[0], first_ref[i] == 1))
    def _():
        s = slot_ref[i]
        for cp in weight_copies(be_ref[i], s):
            cp.wait()

        @pl.when(next_ref[i] >= 0)
        def _():
            for cp in weight_copies(next_ref[i], 1 - s):
                cp.start()

        w1b[...] = w1f[s].astype(BF16)
        w3b[...] = w3f[s].astype(BF16)
        w2b[...] = w2f[s].astype(BF16)

    @pl.when(i < nb_ref[0])
    def _():
        xb = _load_token_rows(xs_ref, valid=bv_ref[i])
        a = _dot_blocks(xb, w1b)
        b = _dot_blocks(xb, w3b)
        hmid = (a * jax.nn.sigmoid(a) * b).astype(BF16)
        _store_token_rows(y_ref, _dot(hmid, w2b[...]))


def _experts(block_e, block_valid, nb_used, xs, w1, w3, w2, rb):
    _, nrows, W = xs.shape
    nb = nrows // rb
    E, D, FF = w1.shape
    idx = jnp.arange(nb, dtype=jnp.int32)
    first = (idx < nb_used[0]) & (block_e != jnp.concatenate([jnp.full((1,), -1, jnp.int32), block_e[:-1]]))
    slot = (jnp.cumsum(first.astype(jnp.int32)) - 1) % 2
    first_pos = jnp.where(first, idx, nb)
    next_pos = jnp.concatenate([lax.cummin(first_pos, reverse=True)[1:], jnp.full((1,), nb, jnp.int32)])
    next_e = jnp.where(next_pos < nb, block_e[jnp.minimum(next_pos, nb - 1)], -1).astype(jnp.int32)
    rows = pl.BlockSpec((2, rb, W), lambda i, be, bv, n, fi, sl, nx: (0, jnp.minimum(i, n[0] - 1), 0))
    grid_spec = pltpu.PrefetchScalarGridSpec(
        num_scalar_prefetch=6,
        grid=(nb,),
        in_specs=[rows, pl.BlockSpec(memory_space=pl.ANY), pl.BlockSpec(memory_space=pl.ANY),
                  pl.BlockSpec(memory_space=pl.ANY)],
        out_specs=rows,
        scratch_shapes=[pltpu.VMEM((2, D, FF), F32), pltpu.VMEM((2, D, FF), F32), pltpu.VMEM((2, FF, D), F32),
                        pltpu.VMEM((D, FF), BF16), pltpu.VMEM((D, FF), BF16), pltpu.VMEM((FF, D), BF16),
                        pltpu.SemaphoreType.DMA((2,))],
    )
    return pl.pallas_call(
        _expert_kernel,
        grid_spec=grid_spec,
        out_shape=jax.ShapeDtypeStruct((2, nrows, W), jnp.uint32),
        compiler_params=_cparams(("arbitrary",)),
        name="experts",
    )(block_e, block_valid, nb_used, first.astype(jnp.int32), slot.astype(jnp.int32), next_e, xs, w1, w3, w2)


def _combine_kernel(yg_ref, gw_ref, u_ref, x1_ref, mod_ref, s1_ref, s3_ref, s2_ref, lg_ref, lb_ref, o_ref):
    ub = _load_token_rows(u_ref)
    a = _dot_blocks(ub, s1_ref)
    b = _dot_blocks(ub, s3_ref)
    ffn = _dot((a * jax.nn.sigmoid(a) * b).astype(BF16), s2_ref[...])
    gw = gw_ref[...]
    routed = None
    for k in range(TOP_K):
        yk = [gw[:, k:k + 1] * blk.astype(F32) for blk in _load_token_rows(yg_ref.at[k])]
        routed = yk if routed is None else [r + y for r, y in zip(routed, yk)]
    ffn = ffn + jnp.concatenate(routed, axis=1)
    g2 = mod_ref[5:6, :]
    o_ref[...] = _ln(DN_ALPHA * x1_ref[...] + g2 * ffn) * lg_ref[...] + lb_ref[...]


def _combine(yg, gw, u2p, x1, mod, s1, s3, s2, ln_g, ln_b, tm):
    B, S, D = x1.shape
    K = yg.shape[0]
    row = lambda n: pl.BlockSpec((None, tm, n), lambda b, i: (b, i, 0))
    cst = lambda a: pl.BlockSpec(a.shape, lambda b, i: (0, 0))
    return pl.pallas_call(
        _combine_kernel,
        grid=(B, S // tm),
        in_specs=[pl.BlockSpec((K, 2, None, tm, D // 4), lambda b, i: (0, 0, b, i, 0)),
                  row(K), pl.BlockSpec((2, None, tm, D // 4), lambda b, i: (0, b, i, 0)), row(D),
                  pl.BlockSpec((None, 6, D), lambda b, i: (b, 0, 0)),
                  cst(s1), cst(s3), cst(s2), cst(ln_g), cst(ln_b)],
        out_specs=row(D),
        out_shape=jax.ShapeDtypeStruct((B, S, D), F32),
        compiler_params=_cparams(("arbitrary", "arbitrary")),
        name="combine",
    )(yg, gw, u2p, x1, mod, s1, s3, s2, ln_g, ln_b)


def _block_layout(counts, T, rb):
    counts = counts.reshape(-1).astype(jnp.int32)
    pcounts = (counts + rb - 1) // rb * rb
    pend = jnp.cumsum(pcounts)
    poffsets = pend - pcounts
    nb = (T * TOP_K) // rb + N_EXPERTS
    starts = jnp.arange(nb, dtype=jnp.int32) * rb
    block_e = jnp.minimum(jnp.sum(starts[:, None] >= pend[None, :], axis=1), N_EXPERTS - 1).astype(jnp.int32)
    nb_used = (pend[-1] // rb).astype(jnp.int32).reshape(1)
    mine = block_e[:, None] == jnp.arange(N_EXPERTS, dtype=jnp.int32)[None, :]
    seg_start = jnp.sum(jnp.where(mine, poffsets[None, :], 0), axis=1)
    seg_count = jnp.sum(jnp.where(mine, counts[None, :], 0), axis=1)
    block_valid = jnp.clip(seg_count - (starts - seg_start), 0, rb).astype(jnp.int32)
    return poffsets, block_e, block_valid, nb_used, nb * rb


def kernel(x, c, w_ada, b_ada, w_in, ml_conv_w, ml_conv_b, ml_gate_b, ml_norm_g, w_out, ln1_g, ln1_b,
           w_router, router_bias, moe_w1, moe_w3, moe_w2, sh_w1, sh_w3, sh_w2, ln2_g, ln2_b):
    B, S, D = x.shape
    T = B * S
    SBW = SB_HEADS * SB_HEAD_DIM
    MLW = ML_HEADS * ML_HEAD_DIM
    rb = EXPERT_BLOCK_ROWS
    tiles = _tiles(S)
    for l in range(DEPTH):
        mod = _adaln(c, w_ada[l], b_ada[l]).reshape(B, 6, D)

        wi = w_in[l]
        c0 = 3 * SBW
        w_sb = wi[:, :c0].astype(BF16)
        w_mqk = wi[:, c0:c0 + 2 * MLW].astype(BF16)
        w_mvo = wi[:, c0 + 2 * MLW:c0 + 4 * MLW].astype(BF16)
        w_g = jnp.pad(wi[:, c0 + 4 * MLW:], ((0, 0), (0, LANES - 2 * ML_HEADS))).astype(BF16)
        sbp, mqk, mvo, gates = _inproj(x, mod, w_sb, w_mqk, w_mvo, w_g, tm=tiles["proj"])

        sb = _sb_attention(sbp, tq=tiles["sb"])

        gate_b = jnp.pad(ml_gate_b[l], (0, LANES - 2 * ML_HEADS)).reshape(1, LANES)
        ml = _mlstm(mqk, mvo, gates, ml_conv_w[l], ml_conv_b[l].reshape(1, -1), gate_b,
                    ml_norm_g[l].reshape(1, -1), L=tiles["ml"])

        wo = w_out[l].astype(BF16)
        wr_parts = jnp.stack(_split3(w_router[l].T)[:2])
        x1, u2p, logits_t = _outproj(sb, ml, x, mod, wo[:SBW], wo[SBW:], ln1_g[l].reshape(1, D),
                                     ln1_b[l].reshape(1, D), wr_parts, tm=tiles["proj"])

        e_t, r_t, g_t, counts = _route(logits_t, router_bias[l], tm=tiles["proj"])
        poffsets, block_e, block_valid, nb_used, nrows = _block_layout(counts, T, rb)
        dest_t = _dest(e_t, r_t, poffsets.astype(F32).reshape(N_EXPERTS, 1), tm=tiles["proj"])
        gw = g_t.T
        idx2 = jnp.concatenate([dest_t, dest_t + nrows], axis=1)
        xs = _sc_dispatch(u2p.reshape(2 * T, D // 4), idx2, 2 * nrows).reshape(2, nrows, D // 4)
        ys = _experts(block_e, block_valid, nb_used, xs, moe_w1[l], moe_w3[l], moe_w2[l], rb)
        yg = _sc_gather(ys.reshape(2 * nrows, D // 4), idx2.reshape(1, TOP_K * 2 * T))
        yg = yg.reshape(TOP_K, 2, B, S, D // 4)
        x = _combine(yg, gw.reshape(B, S, TOP_K), u2p, x1, mod, sh_w1[l].astype(BF16),
                     sh_w3[l].astype(BF16), sh_w2[l].astype(BF16), ln2_g[l].reshape(1, D),
                     ln2_b[l].reshape(1, D), tm=tiles["combine"])
    return x
```

```python
import functools
import math

import jax
import jax.numpy as jnp
from jax import lax
from jax.experimental import pallas as pl
from jax.experimental.pallas import tpu as pltpu
from jax.experimental.pallas import tpu_sc as plsc

F32 = jnp.float32
BF16 = jnp.bfloat16
HIGHEST = lax.Precision.HIGHEST

SB_HEADS = 8
SB_HEAD_DIM = 64
ML_HEADS = 4
ML_HEAD_DIM = 128
CONV_K = 4
N_EXPERTS = 256
TOP_K = 8
N_GROUPS = 8
TOPK_GROUPS = 4
GROUP_SIZE = N_EXPERTS // N_GROUPS
ROUTED_SCALE = 2.5
EXPERT_BLOCK_ROWS = 512
SC_WINDOW = 128
DEPTH = 1
DN_ALPHA = (2 * DEPTH) ** 0.25
LN_EPS = 1e-5
LANES = 128
NEG_INF = float("-inf")
SB_CUTOFF = 104.0
SB_QBLOCKS = 8
ML_SEQS_PER_STEP = 4
VMEM_LIMIT = 56 * 1024 * 1024

PROJ_ROWS = 512
OUTPROJ_ROWS = 1024
SB_ROWS = 256
ML_CHUNK_ROWS = 256
COMBINE_ROWS = 512


def _tiles(S):
    return dict(proj=min(PROJ_ROWS, S), outproj=min(OUTPROJ_ROWS, S), sb=min(SB_ROWS, S // SB_QBLOCKS),
                ml=min(ML_CHUNK_ROWS, S), combine=min(COMBINE_ROWS, S))


def _cparams(sem):
    return pltpu.CompilerParams(dimension_semantics=sem, vmem_limit_bytes=VMEM_LIMIT)


def _ln(x):
    mu = jnp.mean(x, axis=-1, keepdims=True)
    xc = x - mu
    var = jnp.mean(xc * xc, axis=-1, keepdims=True)
    return xc * lax.rsqrt(var + LN_EPS)


def _dot(a, b):
    return jnp.dot(a, b, preferred_element_type=F32)


def _dot_nt(a, b):
    return lax.dot_general(a, b, (((1,), (1,)), ((), ())), preferred_element_type=F32)


def _dot_tn(a, b):
    return lax.dot_general(a, b, (((0,), (0,)), ((), ())), preferred_element_type=F32)


def _pack_halves(v):
    w = v.shape[1] // 2
    lo = lax.bitcast_convert_type(v[:, :w].astype(BF16).astype(F32), jnp.uint32) >> 16
    hi = lax.bitcast_convert_type(v[:, w:].astype(BF16).astype(F32), jnp.uint32) & jnp.uint32(0xFFFF0000)
    return hi | lo


def _unpack_halves(p):
    lo = lax.bitcast_convert_type(p << 16, F32).astype(BF16)
    hi = lax.bitcast_convert_type(p & jnp.uint32(0xFFFF0000), F32).astype(BF16)
    return lo, hi


def _store_token_rows(ref, v):
    p = _pack_halves(v)
    q = p.shape[1] // 2
    ref[0] = p[:, :q]
    ref[1] = p[:, q:]


def _load_token_rows(ref, valid=None):
    first, second = ref[0], ref[1]
    if valid is not None:
        row = lax.broadcasted_iota(jnp.int32, first.shape, 0)
        first = jnp.where(row < valid, first, jnp.uint32(0))
        second = jnp.where(row < valid, second, jnp.uint32(0))
    lo_a, hi_a = _unpack_halves(first)
    lo_b, hi_b = _unpack_halves(second)
    return [lo_a, lo_b, hi_a, hi_b]


def _dot_blocks(blocks, w_ref):
    q = blocks[0].shape[1]
    acc = _dot(blocks[0], w_ref[0:q, :])
    for i in range(1, len(blocks)):
        acc = acc + _dot(blocks[i], w_ref[i * q:(i + 1) * q, :])
    return acc


def _adaln_kernel(c_ref, w_ref, b_ref, o_ref):
    c = c_ref[...]
    s = c * jax.nn.sigmoid(c)
    o_ref[...] = jnp.dot(s, w_ref[...], preferred_element_type=F32, precision=HIGHEST) + b_ref[...]


def _adaln(c, w_ada, b_ada):
    B, D = c.shape
    N = w_ada.shape[1]
    tn = 1024
    return pl.pallas_call(
        _adaln_kernel,
        grid=(N // tn,),
        in_specs=[pl.BlockSpec((B, D), lambda j: (0, 0)),
                  pl.BlockSpec((D, tn), lambda j: (0, j)),
                  pl.BlockSpec((1, tn), lambda j: (0, j))],
        out_specs=pl.BlockSpec((B, tn), lambda j: (0, j)),
        out_shape=jax.ShapeDtypeStruct((B, N), F32),
        compiler_params=_cparams(("arbitrary",)),
        name="adaln",
    )(c, w_ada, b_ada.reshape(1, N))


def _inproj_kernel(x_ref, mod_ref, wsb_ref, wqk_ref, wvo_ref, wg_ref,
                   sb_ref, mqk_ref, mvo_ref, g_ref):
    y = _ln(x_ref[...])
    sh = mod_ref[0:1, :]
    sc = mod_ref[1:2, :]
    u = (y * (1.0 + sc) + sh).astype(BF16)
    sb_ref[...] = _dot(u, wsb_ref[...]).astype(BF16)
    mqk_ref[...] = _dot(u, wqk_ref[...])
    mvo_ref[...] = _dot(u, wvo_ref[...]).astype(BF16)
    g_ref[...] = _dot(u, wg_ref[...])


def _inproj(x, mod, w_sb, w_mqk, w_mvo, w_g, tm):
    B, S, D = x.shape
    nsb, nqk, nvo, ng = w_sb.shape[1], w_mqk.shape[1], w_mvo.shape[1], w_g.shape[1]
    row = lambda n: pl.BlockSpec((None, tm, n), lambda b, i: (b, i, 0))
    full = lambda n: pl.BlockSpec((D, n), lambda b, i: (0, 0))
    return pl.pallas_call(
        _inproj_kernel,
        grid=(B, S // tm),
        in_specs=[row(D), pl.BlockSpec((None, 6, D), lambda b, i: (b, 0, 0)),
                  full(nsb), full(nqk), full(nvo), full(ng)],
        out_specs=[row(nsb), row(nqk), row(nvo), row(ng)],
        out_shape=[jax.ShapeDtypeStruct((B, S, nsb), BF16),
                   jax.ShapeDtypeStruct((B, S, nqk), F32),
                   jax.ShapeDtypeStruct((B, S, nvo), BF16),
                   jax.ShapeDtypeStruct((B, S, ng), F32)],
        compiler_params=_cparams(("arbitrary", "arbitrary")),
        name="inproj",
    )(x, mod, w_sb, w_mqk, w_mvo, w_g)


def _sb_block(z):
    sp = jnp.log(1.0 + jnp.exp(-jnp.abs(z)))
    log_beta = jnp.minimum(z, 0.0) - sp
    return log_beta, log_beta - z


def _sb_kernel(q_ref, k_ref, v_ref, o_ref, *, tq, scale):
    first = pl.program_id(2) * SB_QBLOCKS
    lane = lax.broadcasted_iota(jnp.int32, (1, LANES), 1)
    r = lax.broadcasted_iota(jnp.int32, (tq, tq), 0)
    c = lax.broadcasted_iota(jnp.int32, (tq, tq), 1)
    upper = (r > c).astype(BF16)
    strict = c < r
    hmasks = [(lane // SB_HEAD_DIM) == h for h in range(2)]
    chains = [(u, h) for u in range(SB_QBLOCKS) for h in range(2)]
    qs = {}
    for u in range(SB_QBLOCKS):
        q2 = q_ref[u * tq:(u + 1) * tq, :]
        for h in range(2):
            qs[u, h] = jnp.where(hmasks[h], q2 * scale, jnp.zeros_like(q2))

    def block(qh, kblk, vblk, carry, masked):
        z = _dot_nt(qh, kblk)
        log_beta, log_1m = _sb_block(z)
        if masked:
            log_1m = jnp.where(strict, log_1m, 0.0)
        after = _dot(log_1m.astype(BF16), upper)
        a = jnp.exp(log_beta + after + carry)
        if masked:
            a = jnp.where(strict, a, 0.0)
        pv = _dot(a.astype(BF16), vblk)
        return pv, carry + jnp.sum(log_1m, axis=1, keepdims=True)

    def sweep(back, accs, carries, masked):
        new_accs, new_carries = dict(accs), dict(carries)
        for u in range(SB_QBLOCKS):
            kb = first + u - back
            off = pl.multiple_of(jnp.maximum(kb, 0) * tq, tq)
            kblk = k_ref[pl.ds(off, tq), :]
            vblk = v_ref[pl.ds(off, tq), :]
            for h in range(2):
                pv, carry = block(qs[u, h], kblk, vblk, carries[u, h], masked)
                new_accs[u, h] = jnp.where(kb >= 0, accs[u, h] + pv, accs[u, h])
                new_carries[u, h] = jnp.where(kb >= 0, carry, carries[u, h])
        return new_accs, new_carries

    accs = {ch: jnp.zeros((tq, LANES), F32) for ch in chains}
    carries = {ch: jnp.zeros((tq, 1), F32) for ch in chains}
    accs, carries = sweep(0, accs, carries, True)

    def top_of(carries, back):
        top = jnp.float32(NEG_INF)
        for u in range(SB_QBLOCKS):
            t = jnp.max(jnp.maximum(carries[u, 0], carries[u, 1]))
            top = jnp.maximum(top, jnp.where(first + u - back >= 0, t, NEG_INF))
        return top

    def cond(st):
        _, _, _, top = st
        return top > -SB_CUTOFF

    def body(st):
        back, accs, carries, _ = st
        accs, carries = sweep(back, accs, carries, False)
        return back + 1, accs, carries, top_of(carries, back + 1)

    _, accs, _, _ = lax.while_loop(cond, body, (jnp.int32(1), accs, carries, top_of(carries, 1)))
    for u in range(SB_QBLOCKS):
        o_ref[u * tq:(u + 1) * tq, :] = jnp.where(hmasks[0], accs[u, 0], accs[u, 1]).astype(o_ref.dtype)


def _sb_attention(sbp, tq):
    B, S, W3 = sbp.shape
    W = W3 // 3
    npair = W // LANES
    ts = SB_QBLOCKS * tq
    kern = functools.partial(_sb_kernel, tq=tq, scale=SB_HEAD_DIM ** -0.5)
    return pl.pallas_call(
        kern,
        grid=(B, npair, S // ts),
        in_specs=[pl.BlockSpec((None, ts, LANES), lambda b, p, i: (b, i, p)),
                  pl.BlockSpec((None, S, LANES), lambda b, p, i: (b, 0, npair + p)),
                  pl.BlockSpec((None, S, LANES), lambda b, p, i: (b, 0, 2 * npair + p))],
        out_specs=pl.BlockSpec((None, ts, LANES), lambda b, p, i: (b, i, p)),
        out_shape=jax.ShapeDtypeStruct((B, S, W), BF16),
        compiler_params=_cparams(("arbitrary", "arbitrary", "arbitrary")),
        name="sb_attention",
    )(sbp, sbp, sbp)


def _mlstm_kernel(qk_ref, vo_ref, g_ref, cw_ref, cb_ref, gb_ref, ng_ref, o_ref,
                  xbuf, ct_ref, m_ref, *, L):
    @pl.when(pl.program_id(1) == 0)
    def _():
        xbuf[:, 0:8, :] = jnp.zeros((xbuf.shape[0], 8, xbuf.shape[2]), F32)
        ct_ref[...] = jnp.zeros_like(ct_ref)
        m_ref[...] = jnp.zeros_like(m_ref)

    for b in range(qk_ref.shape[0]):
        _mlstm_chunk(qk_ref.at[b], vo_ref.at[b], g_ref.at[b], cw_ref, cb_ref, gb_ref, ng_ref, o_ref.at[b],
                     xbuf.at[b], ct_ref.at[b], m_ref.at[b], L)


def _mlstm_chunk(qk_ref, vo_ref, g_ref, cw_ref, cb_ref, gb_ref, ng_ref, o_ref, xbuf, ct_ref, m_ref, L):
    H, dk = ML_HEADS, ML_HEAD_DIM
    W = H * dk

    xbuf[8:8 + L, :] = qk_ref[...]
    y = cb_ref[...] + cw_ref[CONV_K - 1:CONV_K, :] * xbuf[8:8 + L, :]
    for j in range(1, CONV_K):
        y = y + cw_ref[CONV_K - 1 - j:CONV_K - j, :] * xbuf[8 - j:8 - j + L, :]
    xbuf[0:8, :] = xbuf[L:L + 8, :]
    qk = y * jax.nn.sigmoid(y)

    g = g_ref[...] + gb_ref[...]
    logf = jax.nn.log_sigmoid(g)
    r = lax.broadcasted_iota(jnp.int32, (L, L), 0)
    c = lax.broadcasted_iota(jnp.int32, (L, L), 1)
    causal = c <= r
    tri = causal.astype(BF16)
    lf_hi = logf.astype(BF16)
    lf_lo = (logf - lf_hi.astype(F32)).astype(BF16)
    bc = _dot(tri, lf_hi) + _dot(tri, lf_lo)
    g_t = g.T
    bc_t = bc.T
    e0 = (lax.broadcasted_iota(jnp.int32, (L, LANES), 1) == 0).astype(F32)

    for h in range(H):
        qh = qk[:, h * dk:(h + 1) * dk].astype(BF16)
        kh = (qk[:, W + h * dk:W + (h + 1) * dk] * (dk ** -0.5)).astype(BF16)
        vh = vo_ref[:, h * dk:(h + 1) * dk].astype(F32)
        oh = vo_ref[:, W + h * dk:W + (h + 1) * dk].astype(F32)
        vaug = jnp.concatenate([vh, e0], axis=1)
        ic_col = g[:, h:h + 1]
        ic_row = g_t[h:h + 1, :]
        bc_col = bc[:, H + h:H + h + 1]
        bc_row = bc_t[H + h:H + h + 1, :]
        m = m_ref[h][:, 0:1]
        ct = ct_ref[h]

        log_d = jnp.where(causal, bc_col - bc_row + ic_row, NEG_INF)
        inter = bc_col + m
        m_t = jnp.maximum(inter, jnp.max(log_d, axis=1, keepdims=True))
        w = _dot_nt(qh, kh) * jnp.exp(log_d - m_t)
        s_inter = jnp.exp(inter - m_t)
        tot = s_inter * _dot(qh, ct.astype(BF16)) + _dot(w.astype(BF16), vaug.astype(BF16))
        num = tot[:, :dk]
        den = tot[:, dk:dk + 1]
        hh = num / jnp.maximum(jnp.abs(den), jnp.exp(-m_t))

        b_last = bc_col[L - 1:L, :]
        log_w = b_last - bc_col + ic_col
        m_new = jnp.maximum(b_last + m, jnp.max(log_w, axis=0, keepdims=True))
        wk = jnp.exp(log_w - m_new)
        decay = jnp.exp(b_last + m - m_new)
        ct_ref[h] = decay * ct + _dot_tn(kh, (wk * vaug).astype(BF16))
        m_ref[h] = jnp.broadcast_to(m_new, (1, LANES))

        hn = _ln(hh) * ng_ref[:, h * dk:(h + 1) * dk]
        o_ref[:, h * dk:(h + 1) * dk] = (jax.nn.sigmoid(oh) * hn).astype(o_ref.dtype)


def _mlstm(mqk, mvo, gates, conv_w, conv_b, gate_b, norm_g, L):
    B, S, W2 = mqk.shape
    W = W2 // 2
    kern = functools.partial(_mlstm_kernel, L=L)
    nb = math.gcd(B, ML_SEQS_PER_STEP)
    row = lambda n: pl.BlockSpec((nb, L, n), lambda b, i: (b, i, 0))
    cst = lambda a: pl.BlockSpec(a.shape, lambda b, i: (0, 0))
    return pl.pallas_call(
        kern,
        grid=(B // nb, S // L),
        in_specs=[row(W2), row(W2), row(LANES), cst(conv_w), cst(conv_b), cst(gate_b), cst(norm_g)],
        out_specs=row(W),
        out_shape=jax.ShapeDtypeStruct((B, S, W), BF16),
        scratch_shapes=[pltpu.VMEM((nb, L + 8, W2), F32),
                        pltpu.VMEM((nb, ML_HEADS, ML_HEAD_DIM, 2 * ML_HEAD_DIM), F32),
                        pltpu.VMEM((nb, ML_HEADS, 1, LANES), F32)],
        compiler_params=_cparams(("arbitrary", "arbitrary")),
        name="mlstm",
    )(mqk, mvo, gates, conv_w, conv_b, gate_b, norm_g)


def _keep_bf16_bits(v):
    bits = lax.bitcast_convert_type(v, jnp.uint32) & jnp.uint32(0xFFFF0000)
    return lax.bitcast_convert_type(bits, F32)


def _split3(v):
    hi = _keep_bf16_bits(v)
    r = v - hi
    mid = _keep_bf16_bits(r)
    lo = r - mid
    return hi.astype(BF16), mid.astype(BF16), lo.astype(BF16)


def _outproj_kernel(sb_ref, ml_ref, x_ref, mod_ref, wo1_ref, wo2_ref, lg_ref, lb_ref, wr_ref,
                    x1_ref, u2_ref, lo_ref):
    mix = _dot(sb_ref[...], wo1_ref[...]) + _dot(ml_ref[...], wo2_ref[...])
    g1 = mod_ref[2:3, :]
    x1 = _ln(DN_ALPHA * x_ref[...] + g1 * mix) * lg_ref[...] + lb_ref[...]
    x1_ref[...] = x1
    u2 = _ln(x1) * (1.0 + mod_ref[4:5, :]) + mod_ref[3:4, :]
    _store_token_rows(u2_ref, u2)
    u_hi, u_mid, _ = _split3(u2)
    w_hi, w_mid = wr_ref[0], wr_ref[1]
    lo_ref[...] = _dot_nt(w_hi, u_hi) + (_dot_nt(w_mid, u_hi) + _dot_nt(w_hi, u_mid))


def _outproj(sb, ml, x, mod, wo1, wo2, ln_g, ln_b, wr_parts, tm):
    B, S, D = x.shape
    W = sb.shape[2]
    E = wr_parts.shape[1]
    nt = S // tm
    row = lambda n: pl.BlockSpec((None, tm, n), lambda b, i: (b, i, 0))
    cst = lambda a: pl.BlockSpec(a.shape, lambda b, i: (0, 0))
    return pl.pallas_call(
        _outproj_kernel,
        grid=(B, nt),
        in_specs=[row(W), row(W), row(D), pl.BlockSpec((None, 6, D), lambda b, i: (b, 0, 0)),
                  cst(wo1), cst(wo2), cst(ln_g), cst(ln_b),
                  pl.BlockSpec(wr_parts.shape, lambda b, i: (0, 0, 0))],
        out_specs=[row(D), pl.BlockSpec((2, None, tm, D // 4), lambda b, i: (0, b, i, 0)),
                   pl.BlockSpec((E, tm), lambda b, i: (0, b * nt + i))],
        out_shape=[jax.ShapeDtypeStruct((B, S, D), F32),
                   jax.ShapeDtypeStruct((2, B, S, D // 4), jnp.uint32),
                   jax.ShapeDtypeStruct((E, B * S), F32)],
        compiler_params=_cparams(("arbitrary", "arbitrary")),
        name="outproj",
    )(sb, ml, x, mod, wo1, wo2, ln_g, ln_b, wr_parts)


def _rows_to_block(rows, dtype):
    n = rows[0].shape[1]
    rid = lax.broadcasted_iota(jnp.int32, (len(rows), n), 0)
    out = jnp.zeros((len(rows), n), dtype)
    for k, v in enumerate(rows):
        out = jnp.where(rid == k, v.astype(dtype), out)
    return out


def _route_kernel(lo_ref, rb_ref, e_ref, r_ref, g_ref, cnt_ref, cnt_scr):
    E, tm = lo_ref.shape

    @pl.when(pl.program_id(0) == 0)
    def _():
        cnt_scr[...] = jnp.zeros_like(cnt_scr)

    scores = jax.nn.sigmoid(lo_ref[...])
    sel = scores + rb_ref[...]
    row_f = lax.broadcasted_iota(jnp.int32, (E, tm), 0).astype(F32)
    groups = [sel[g * GROUP_SIZE:(g + 1) * GROUP_SIZE, :] for g in range(N_GROUPS)]
    gscore = []
    for xg in groups:
        m1 = jnp.max(xg, axis=0, keepdims=True)
        is_max = xg == m1
        cnt = jnp.sum(is_max.astype(F32), axis=0, keepdims=True)
        m2 = jnp.max(jnp.where(is_max, NEG_INF, xg), axis=0, keepdims=True)
        gscore.append(m1 + jnp.where(cnt >= 2.0, m1, m2))
    kept = []
    for g in range(N_GROUPS):
        rank = jnp.zeros((1, tm), jnp.int32)
        for o in range(N_GROUPS):
            if o != g:
                beats = (gscore[o] >= gscore[g]) if o < g else (gscore[o] > gscore[g])
                rank = rank + beats.astype(jnp.int32)
        kept.append(jnp.where(rank < TOPK_GROUPS, groups[g], NEG_INF))
    cur = jnp.concatenate(kept, axis=0)
    idxs = []
    chosen = jnp.zeros((E, tm), F32)
    for k in range(TOP_K):
        m = jnp.max(cur, axis=0, keepdims=True)
        idx = jnp.min(jnp.where(cur == m, row_f, float(E)), axis=0, keepdims=True)
        pick = row_f == idx
        cur = jnp.where(pick, NEG_INF, cur)
        chosen = jnp.where(pick, 1.0, chosen)
        idxs.append(idx)
    r = lax.broadcasted_iota(jnp.int32, (tm, tm), 0)
    c = lax.broadcasted_iota(jnp.int32, (tm, tm), 1)
    before = _dot(chosen.astype(BF16), (r < c).astype(BF16)) + cnt_scr[...]
    gates, ranks = [], []
    for k in range(TOP_K):
        pick = row_f == idxs[k]
        gates.append(jnp.sum(jnp.where(pick, scores, 0.0), axis=0, keepdims=True))
        ranks.append(jnp.sum(jnp.where(pick, before, 0.0), axis=0, keepdims=True))
    gsum = gates[0]
    for gk in gates[1:]:
        gsum = gsum + gk
    e_ref[...] = _rows_to_block(idxs, jnp.int32)
    r_ref[...] = _rows_to_block(ranks, jnp.int32)
    g_ref[...] = _rows_to_block(gates, F32) / gsum * ROUTED_SCALE
    cnt_scr[...] = cnt_scr[...] + jnp.sum(chosen, axis=1, keepdims=True)
    cnt_ref[...] = cnt_scr[...]


def _route(logits_t, router_bias, tm):
    E, T = logits_t.shape
    tok = lambda dt: jax.ShapeDtypeStruct((TOP_K, T), dt)
    return pl.pallas_call(
        _route_kernel,
        grid=(T // tm,),
        in_specs=[pl.BlockSpec((E, tm), lambda i: (0, i)), pl.BlockSpec((E, 1), lambda i: (0, 0))],
        out_specs=[pl.BlockSpec((TOP_K, tm), lambda i: (0, i))] * 3 + [pl.BlockSpec((E, 1), lambda i: (0, 0))],
        out_shape=[tok(jnp.int32), tok(jnp.int32), tok(F32), jax.ShapeDtypeStruct((E, 1), F32)],
        scratch_shapes=[pltpu.VMEM((E, 1), F32)],
        compiler_params=_cparams(("arbitrary",)),
        name="route",
    )(logits_t, router_bias.reshape(E, 1))


def _dest_kernel(e_ref, r_ref, off_ref, d_ref):
    E = off_ref.shape[0]
    tm = e_ref.shape[1]
    row = lax.broadcasted_iota(jnp.int32, (E, tm), 0)
    off = off_ref[...]
    base = [jnp.sum(jnp.where(row == e_ref[k:k + 1, :], off, 0.0), axis=0, keepdims=True) for k in range(TOP_K)]
    d_ref[...] = _rows_to_block(base, jnp.int32) + r_ref[...]


def _dest(e_t, r_t, offsets, tm):
    K, T = e_t.shape
    E = offsets.shape[0]
    tok = pl.BlockSpec((K, tm), lambda i: (0, i))
    return pl.pallas_call(
        _dest_kernel,
        grid=(T // tm,),
        in_specs=[tok, tok, pl.BlockSpec((E, 1), lambda i: (0, 0))],
        out_specs=tok,
        out_shape=jax.ShapeDtypeStruct((K, T), jnp.int32),
        compiler_params=_cparams(("arbitrary",)),
        name="dest",
    )(e_t, r_t, offsets)


def _sc_mesh():
    return plsc.VectorSubcoreMesh(core_axis_name="core", subcore_axis_name="subcore")


def _sc_dispatch(u2p, dest_t, nrows):
    T, W = u2p.shape
    K = dest_t.shape[0]
    win = SC_WINDOW

    @pl.kernel(out_type=jax.ShapeDtypeStruct((nrows, W), u2p.dtype), mesh=_sc_mesh(), scratch_types=[])
    def scatter_rows(x_hbm, i_hbm, o_hbm):
        def body(x_vmem, i_vmem):
            for k in range(K):
                pltpu.sync_copy(x_vmem, o_hbm.at[i_vmem.at[k]])

        pltpu.emit_pipeline(
            body,
            grid=(T // win,),
            in_specs=[pl.BlockSpec((win, W), lambda i: (i, 0)),
                      pl.BlockSpec((K, win), lambda i: (0, i))],
            out_specs=[],
            core_axis_name=("core", "subcore"),
            dimension_semantics=(pltpu.PARALLEL,),
        )(x_hbm, i_hbm)

    return scatter_rows(u2p, dest_t)


def _sc_gather(ys, dest_flat):
    W = ys.shape[1]
    n = dest_flat.shape[1]
    win = SC_WINDOW

    @pl.kernel(out_type=jax.ShapeDtypeStruct((n, W), ys.dtype), mesh=_sc_mesh(), scratch_types=[])
    def gather_rows(y_hbm, i_hbm, o_hbm):
        def body(i_vmem, o_vmem):
            pltpu.sync_copy(y_hbm.at[i_vmem.at[0]], o_vmem)

        pltpu.emit_pipeline(
            body,
            grid=(n // win,),
            in_specs=[pl.BlockSpec((1, win), lambda i: (0, i))],
            out_specs=[pl.BlockSpec((win, W), lambda i: (i, 0))],
            core_axis_name=("core", "subcore"),
            dimension_semantics=(pltpu.PARALLEL,),
        )(i_hbm, o_hbm)

    return gather_rows(ys, dest_flat)


def _expert_kernel(be_ref, bv_ref, nb_ref, first_ref, slot_ref, next_ref, xs_ref, w1_hbm, w3_hbm, w2_hbm,
                   y_ref, w1f, w3f, w2f, w1b, w3b, w2b, sem):
    i = pl.program_id(0)

    def weight_copies(e, s):
        return [pltpu.make_async_copy(w1_hbm.at[e], w1f.at[s], sem.at[s]),
                pltpu.make_async_copy(w3_hbm.at[e], w3f.at[s], sem.at[s]),
                pltpu.make_async_copy(w2_hbm.at[e], w2f.at[s], sem.at[s])]

    @pl.when(i == 0)
    def _():
        for cp in weight_copies(be_ref[0], 0):
            cp.start()

    @pl.when(jnp.logical_and(i < nb_ref[0], first_ref[i] == 1))
    def _():
        s = slot_ref[i]
        for cp in weight_copies(be_ref[i], s):
            cp.wait()

        @pl.when(next_ref[i] >= 0)
        def _():
            for cp in weight_copies(next_ref[i], 1 - s):
                cp.start()

        w1b[...] = w1f[s].astype(BF16)
        w3b[...] = w3f[s].astype(BF16)
        w2b[...] = w2f[s].astype(BF16)

    @pl.when(i < nb_ref[0])
    def _():
        xb = _load_token_rows(xs_ref, valid=bv_ref[i])
        a = _dot_blocks(xb, w1b)
        b = _dot_blocks(xb, w3b)
        hmid = (a * jax.nn.sigmoid(a) * b).astype(BF16)
        _store_token_rows(y_ref, _dot(hmid, w2b[...]))


def _experts(block_e, block_valid, nb_used, xs, w1, w3, w2, rb):
    _, nrows, W = xs.shape
    nb = nrows // rb
    E, D, FF = w1.shape
    idx = jnp.arange(nb, dtype=jnp.int32)
    first = (idx < nb_used[0]) & (block_e != jnp.concatenate([jnp.full((1,), -1, jnp.int32), block_e[:-1]]))
    slot = (jnp.cumsum(first.astype(jnp.int32)) - 1) % 2
    first_pos = jnp.where(first, idx, nb)
    next_pos = jnp.concatenate([lax.cummin(first_pos, reverse=True)[1:], jnp.full((1,), nb, jnp.int32)])
    next_e = jnp.where(next_pos < nb, block_e[jnp.minimum(next_pos, nb - 1)], -1).astype(jnp.int32)
    rows = pl.BlockSpec((2, rb, W), lambda i, be, bv, n, fi, sl, nx: (0, jnp.minimum(i, n[0] - 1), 0))
    grid_spec = pltpu.PrefetchScalarGridSpec(
        num_scalar_prefetch=6,
        grid=(nb,),
        in_specs=[rows, pl.BlockSpec(memory_space=pl.ANY), pl.BlockSpec(memory_space=pl.ANY),
                  pl.BlockSpec(memory_space=pl.ANY)],
        out_specs=rows,
        scratch_shapes=[pltpu.VMEM((2, D, FF), F32), pltpu.VMEM((2, D, FF), F32), pltpu.VMEM((2, FF, D), F32),
                        pltpu.VMEM((D, FF), BF16), pltpu.VMEM((D, FF), BF16), pltpu.VMEM((FF, D), BF16),
                        pltpu.SemaphoreType.DMA((2,))],
    )
    return pl.pallas_call(
        _expert_kernel,
        grid_spec=grid_spec,
        out_shape=jax.ShapeDtypeStruct((2, nrows, W), jnp.uint32),
        compiler_params=_cparams(("arbitrary",)),
        name="experts",
    )(block_e, block_valid, nb_used, first.astype(jnp.int32), slot.astype(jnp.int32), next_e, xs, w1, w3, w2)


def _combine_kernel(yg_ref, gw_ref, u_ref, x1_ref, mod_ref, s1_ref, s3_ref, s2_ref, lg_ref, lb_ref, o_ref):
    ub = _load_token_rows(u_ref)
    a = _dot_blocks(ub, s1_ref)
    b = _dot_blocks(ub, s3_ref)
    ffn = _dot((a * jax.nn.sigmoid(a) * b).astype(BF16), s2_ref[...])
    gw = gw_ref[...]
    routed = None
    for k in range(TOP_K):
        yk = [gw[:, k:k + 1] * blk.astype(F32) for blk in _load_token_rows(yg_ref.at[k])]
        routed = yk if routed is None else [r + y for r, y in zip(routed, yk)]
    ffn = ffn + jnp.concatenate(routed, axis=1)
    g2 = mod_ref[5:6, :]
    o_ref[...] = _ln(DN_ALPHA * x1_ref[...] + g2 * ffn) * lg_ref[...] + lb_ref[...]


def _combine(yg, gw, u2p, x1, mod, s1, s3, s2, ln_g, ln_b, tm):
    B, S, D = x1.shape
    K = yg.shape[0]
    row = lambda n: pl.BlockSpec((None, tm, n), lambda b, i: (b, i, 0))
    cst = lambda a: pl.BlockSpec(a.shape, lambda b, i: (0, 0))
    return pl.pallas_call(
        _combine_kernel,
        grid=(B, S // tm),
        in_specs=[pl.BlockSpec((K, 2, None, tm, D // 4), lambda b, i: (0, 0, b, i, 0)),
                  row(K), pl.BlockSpec((2, None, tm, D // 4), lambda b, i: (0, b, i, 0)), row(D),
                  pl.BlockSpec((None, 6, D), lambda b, i: (b, 0, 0)),
                  cst(s1), cst(s3), cst(s2), cst(ln_g), cst(ln_b)],
        out_specs=row(D),
        out_shape=jax.ShapeDtypeStruct((B, S, D), F32),
        compiler_params=_cparams(("arbitrary", "arbitrary")),
        name="combine",
    )(yg, gw, u2p, x1, mod, s1, s3, s2, ln_g, ln_b)


def _block_layout(counts, T, rb):
    counts = counts.reshape(-1).astype(jnp.int32)
    pcounts = (counts + rb - 1) // rb * rb
    pend = jnp.cumsum(pcounts)
    poffsets = pend - pcounts
    nb = (T * TOP_K) // rb + N_EXPERTS
    starts = jnp.arange(nb, dtype=jnp.int32) * rb
    block_e = jnp.minimum(jnp.sum(starts[:, None] >= pend[None, :], axis=1), N_EXPERTS - 1).astype(jnp.int32)
    nb_used = (pend[-1] // rb).astype(jnp.int32).reshape(1)
    mine = block_e[:, None] == jnp.arange(N_EXPERTS, dtype=jnp.int32)[None, :]
    seg_start = jnp.sum(jnp.where(mine, poffsets[None, :], 0), axis=1)
    seg_count = jnp.sum(jnp.where(mine, counts[None, :], 0), axis=1)
    block_valid = jnp.clip(seg_count - (starts - seg_start), 0, rb).astype(jnp.int32)
    return poffsets, block_e, block_valid, nb_used, nb * rb


def kernel(x, c, w_ada, b_ada, w_in, ml_conv_w, ml_conv_b, ml_gate_b, ml_norm_g, w_out, ln1_g, ln1_b,
           w_router, router_bias, moe_w1, moe_w3, moe_w2, sh_w1, sh_w3, sh_w2, ln2_g, ln2_b):
    B, S, D = x.shape
    T = B * S
    SBW = SB_HEADS * SB_HEAD_DIM
    MLW = ML_HEADS * ML_HEAD_DIM
    rb = EXPERT_BLOCK_ROWS
    tiles = _tiles(S)
    for l in range(DEPTH):
        mod = _adaln(c, w_ada[l], b_ada[l]).reshape(B, 6, D)

        wi = w_in[l]
        c0 = 3 * SBW
        w_sb = wi[:, :c0].astype(BF16)
        w_mqk = wi[:, c0:c0 + 2 * MLW].astype(BF16)
        w_mvo = wi[:, c0 + 2 * MLW:c0 + 4 * MLW].astype(BF16)
        w_g = jnp.pad(wi[:, c0 + 4 * MLW:], ((0, 0), (0, LANES - 2 * ML_HEADS))).astype(BF16)
        sbp, mqk, mvo, gates = _inproj(x, mod, w_sb, w_mqk, w_mvo, w_g, tm=tiles["proj"])

        sb = _sb_attention(sbp, tq=tiles["sb"])

        gate_b = jnp.pad(ml_gate_b[l], (0, LANES - 2 * ML_HEADS)).reshape(1, LANES)
        ml = _mlstm(mqk, mvo, gates, ml_conv_w[l], ml_conv_b[l].reshape(1, -1), gate_b,
                    ml_norm_g[l].reshape(1, -1), L=tiles["ml"])

        wo = w_out[l].astype(BF16)
        wr_parts = jnp.stack(_split3(w_router[l].T)[:2])
        x1, u2p, logits_t = _outproj(sb, ml, x, mod, wo[:SBW], wo[SBW:], ln1_g[l].reshape(1, D),
                                     ln1_b[l].reshape(1, D), wr_parts, tm=tiles["outproj"])

        e_t, r_t, g_t, counts = _route(logits_t, router_bias[l], tm=tiles["proj"])
        poffsets, block_e, block_valid, nb_used, nrows = _block_layout(counts, T, rb)
        dest_t = _dest(e_t, r_t, poffsets.astype(F32).reshape(N_EXPERTS, 1), tm=tiles["proj"])
        gw = g_t.T
        idx2 = jnp.concatenate([dest_t, dest_t + nrows], axis=1)
        xs = _sc_dispatch(u2p.reshape(2 * T, D // 4), idx2, 2 * nrows).reshape(2, nrows, D // 4)
        ys = _experts(block_e, block_valid, nb_used, xs, moe_w1[l], moe_w3[l], moe_w2[l], rb)
        yg = _sc_gather(ys.reshape(2 * nrows, D // 4), idx2.reshape(1, TOP_K * 2 * T))
        yg = yg.reshape(TOP_K, 2, B, S, D // 4)
        x = _combine(yg, gw.reshape(B, S, TOP_K), u2p, x1, mod, sh_w1[l].astype(BF16),
                     sh_w3[l].astype(BF16), sh_w2[l].astype(BF16), ln2_g[l].reshape(1, D),
                     ln2_b[l].reshape(1, D), tm=tiles["combine"])
    return x
```

```python
import functools
import math

import jax
import jax.numpy as jnp
from jax import lax
from jax.experimental import pallas as pl
from jax.experimental.pallas import tpu as pltpu
from jax.experimental.pallas import tpu_sc as plsc

F32 = jnp.float32
BF16 = jnp.bfloat16
HIGHEST = lax.Precision.HIGHEST

SB_HEADS = 8
SB_HEAD_DIM = 64
ML_HEADS = 4
ML_HEAD_DIM = 128
CONV_K = 4
N_EXPERTS = 256
TOP_K = 8
N_GROUPS = 8
TOPK_GROUPS = 4
GROUP_SIZE = N_EXPERTS // N_GROUPS
ROUTED_SCALE = 2.5
EXPERT_BLOCK_ROWS = 512
SC_WINDOW = 128
DEPTH = 1
DN_ALPHA = (2 * DEPTH) ** 0.25
LN_EPS = 1e-5
LANES = 128
NEG_INF = float("-inf")
SB_CUTOFF = 104.0
SB_QBLOCKS = 8
ML_SEQS_PER_STEP = 4
VMEM_LIMIT = 56 * 1024 * 1024

PROJ_ROWS = 512
INPROJ_ROWS = 1024
OUTPROJ_ROWS = 1024
SB_ROWS = 256
ML_CHUNK_ROWS = 256
COMBINE_ROWS = 512


def _tiles(S):
    return dict(proj=min(PROJ_ROWS, S), inproj=min(INPROJ_ROWS, S), outproj=min(OUTPROJ_ROWS, S),
                sb=min(SB_ROWS, S // SB_QBLOCKS),
                ml=min(ML_CHUNK_ROWS, S), combine=min(COMBINE_ROWS, S))


def _cparams(sem):
    return pltpu.CompilerParams(dimension_semantics=sem, vmem_limit_bytes=VMEM_LIMIT)


def _ln(x):
    mu = jnp.mean(x, axis=-1, keepdims=True)
    xc = x - mu
    var = jnp.mean(xc * xc, axis=-1, keepdims=True)
    return xc * lax.rsqrt(var + LN_EPS)


def _dot(a, b):
    return jnp.dot(a, b, preferred_element_type=F32)


def _dot_nt(a, b):
    return lax.dot_general(a, b, (((1,), (1,)), ((), ())), preferred_element_type=F32)


def _dot_tn(a, b):
    return lax.dot_general(a, b, (((0,), (0,)), ((), ())), preferred_element_type=F32)


def _pack_halves(v):
    w = v.shape[1] // 2
    lo = lax.bitcast_convert_type(v[:, :w].astype(BF16).astype(F32), jnp.uint32) >> 16
    hi = lax.bitcast_convert_type(v[:, w:].astype(BF16).astype(F32), jnp.uint32) & jnp.uint32(0xFFFF0000)
    return hi | lo


def _unpack_halves(p):
    lo = lax.bitcast_convert_type(p << 16, F32).astype(BF16)
    hi = lax.bitcast_convert_type(p & jnp.uint32(0xFFFF0000), F32).astype(BF16)
    return lo, hi


def _store_token_rows(ref, v):
    p = _pack_halves(v)
    q = p.shape[1] // 2
    ref[0] = p[:, :q]
    ref[1] = p[:, q:]


def _load_token_rows(ref, valid=None):
    first, second = ref[0], ref[1]
    if valid is not None:
        row = lax.broadcasted_iota(jnp.int32, first.shape, 0)
        first = jnp.where(row < valid, first, jnp.uint32(0))
        second = jnp.where(row < valid, second, jnp.uint32(0))
    lo_a, hi_a = _unpack_halves(first)
    lo_b, hi_b = _unpack_halves(second)
    return [lo_a, lo_b, hi_a, hi_b]


def _dot_blocks(blocks, w_ref):
    q = blocks[0].shape[1]
    acc = _dot(blocks[0], w_ref[0:q, :])
    for i in range(1, len(blocks)):
        acc = acc + _dot(blocks[i], w_ref[i * q:(i + 1) * q, :])
    return acc


def _adaln_kernel(c_ref, w_ref, b_ref, o_ref):
    c = c_ref[...]
    s = c * jax.nn.sigmoid(c)
    o_ref[...] = jnp.dot(s, w_ref[...], preferred_element_type=F32, precision=HIGHEST) + b_ref[...]


def _adaln(c, w_ada, b_ada):
    B, D = c.shape
    N = w_ada.shape[1]
    tn = 1024
    return pl.pallas_call(
        _adaln_kernel,
        grid=(N // tn,),
        in_specs=[pl.BlockSpec((B, D), lambda j: (0, 0)),
                  pl.BlockSpec((D, tn), lambda j: (0, j)),
                  pl.BlockSpec((1, tn), lambda j: (0, j))],
        out_specs=pl.BlockSpec((B, tn), lambda j: (0, j)),
        out_shape=jax.ShapeDtypeStruct((B, N), F32),
        compiler_params=_cparams(("arbitrary",)),
        name="adaln",
    )(c, w_ada, b_ada.reshape(1, N))


def _inproj_kernel(x_ref, mod_ref, wsb_ref, wqk_ref, wvo_ref, wg_ref,
                   sb_ref, mqk_ref, mvo_ref, g_ref):
    y = _ln(x_ref[...])
    sh = mod_ref[0:1, :]
    sc = mod_ref[1:2, :]
    u = (y * (1.0 + sc) + sh).astype(BF16)
    sb_ref[...] = _dot(u, wsb_ref[...]).astype(BF16)
    mqk_ref[...] = _dot(u, wqk_ref[...])
    mvo_ref[...] = _dot(u, wvo_ref[...]).astype(BF16)
    g_ref[...] = _dot(u, wg_ref[...])


def _inproj(x, mod, w_sb, w_mqk, w_mvo, w_g, tm):
    B, S, D = x.shape
    nsb, nqk, nvo, ng = w_sb.shape[1], w_mqk.shape[1], w_mvo.shape[1], w_g.shape[1]
    row = lambda n: pl.BlockSpec((None, tm, n), lambda b, i: (b, i, 0))
    full = lambda n: pl.BlockSpec((D, n), lambda b, i: (0, 0))
    return pl.pallas_call(
        _inproj_kernel,
        grid=(B, S // tm),
        in_specs=[row(D), pl.BlockSpec((None, 6, D), lambda b, i: (b, 0, 0)),
                  full(nsb), full(nqk), full(nvo), full(ng)],
        out_specs=[row(nsb), row(nqk), row(nvo), row(ng)],
        out_shape=[jax.ShapeDtypeStruct((B, S, nsb), BF16),
                   jax.ShapeDtypeStruct((B, S, nqk), F32),
                   jax.ShapeDtypeStruct((B, S, nvo), BF16),
                   jax.ShapeDtypeStruct((B, S, ng), F32)],
        compiler_params=_cparams(("arbitrary", "arbitrary")),
        name="inproj",
    )(x, mod, w_sb, w_mqk, w_mvo, w_g)


def _sb_block(z):
    sp = jnp.log(1.0 + jnp.exp(-jnp.abs(z)))
    log_beta = jnp.minimum(z, 0.0) - sp
    return log_beta, log_beta - z


def _sb_kernel(q_ref, k_ref, v_ref, o_ref, *, tq, scale):
    first = pl.program_id(2) * SB_QBLOCKS
    lane = lax.broadcasted_iota(jnp.int32, (1, LANES), 1)
    r = lax.broadcasted_iota(jnp.int32, (tq, tq), 0)
    c = lax.broadcasted_iota(jnp.int32, (tq, tq), 1)
    upper = (r > c).astype(BF16)
    strict = c < r
    hmasks = [(lane // SB_HEAD_DIM) == h for h in range(2)]
    chains = [(u, h) for u in range(SB_QBLOCKS) for h in range(2)]
    qs = {}
    for u in range(SB_QBLOCKS):
        q2 = q_ref[u * tq:(u + 1) * tq, :]
        for h in range(2):
            qs[u, h] = jnp.where(hmasks[h], q2 * scale, jnp.zeros_like(q2))

    def block(qh, kblk, vblk, carry, masked):
        z = _dot_nt(qh, kblk)
        log_beta, log_1m = _sb_block(z)
        if masked:
            log_1m = jnp.where(strict, log_1m, 0.0)
        after = _dot(log_1m.astype(BF16), upper)
        a = jnp.exp(log_beta + after + carry)
        if masked:
            a = jnp.where(strict, a, 0.0)
        pv = _dot(a.astype(BF16), vblk)
        return pv, carry + jnp.sum(log_1m, axis=1, keepdims=True)

    def sweep(back, accs, carries, masked):
        new_accs, new_carries = dict(accs), dict(carries)
        for u in range(SB_QBLOCKS):
            kb = first + u - back
            off = pl.multiple_of(jnp.maximum(kb, 0) * tq, tq)
            kblk = k_ref[pl.ds(off, tq), :]
            vblk = v_ref[pl.ds(off, tq), :]
            for h in range(2):
                pv, carry = block(qs[u, h], kblk, vblk, carries[u, h], masked)
                new_accs[u, h] = jnp.where(kb >= 0, accs[u, h] + pv, accs[u, h])
                new_carries[u, h] = jnp.where(kb >= 0, carry, carries[u, h])
        return new_accs, new_carries

    accs = {ch: jnp.zeros((tq, LANES), F32) for ch in chains}
    carries = {ch: jnp.zeros((tq, 1), F32) for ch in chains}
    accs, carries = sweep(0, accs, carries, True)

    def top_of(carries, back):
        top = jnp.float32(NEG_INF)
        for u in range(SB_QBLOCKS):
            t = jnp.max(jnp.maximum(carries[u, 0], carries[u, 1]))
            top = jnp.maximum(top, jnp.where(first + u - back >= 0, t, NEG_INF))
        return top

    def cond(st):
        _, _, _, top = st
        return top > -SB_CUTOFF

    def body(st):
        back, accs, carries, _ = st
        accs, carries = sweep(back, accs, carries, False)
        return back + 1, accs, carries, top_of(carries, back + 1)

    _, accs, _, _ = lax.while_loop(cond, body, (jnp.int32(1), accs, carries, top_of(carries, 1)))
    for u in range(SB_QBLOCKS):
        o_ref[u * tq:(u + 1) * tq, :] = jnp.where(hmasks[0], accs[u, 0], accs[u, 1]).astype(o_ref.dtype)


def _sb_attention(sbp, tq):
    B, S, W3 = sbp.shape
    W = W3 // 3
    npair = W // LANES
    ts = SB_QBLOCKS * tq
    kern = functools.partial(_sb_kernel, tq=tq, scale=SB_HEAD_DIM ** -0.5)
    return pl.pallas_call(
        kern,
        grid=(B, npair, S // ts),
        in_specs=[pl.BlockSpec((None, ts, LANES), lambda b, p, i: (b, i, p)),
                  pl.BlockSpec((None, S, LANES), lambda b, p, i: (b, 0, npair + p)),
                  pl.BlockSpec((None, S, LANES), lambda b, p, i: (b, 0, 2 * npair + p))],
        out_specs=pl.BlockSpec((None, ts, LANES), lambda b, p, i: (b, i, p)),
        out_shape=jax.ShapeDtypeStruct((B, S, W), BF16),
        compiler_params=_cparams(("arbitrary", "arbitrary", "arbitrary")),
        name="sb_attention",
    )(sbp, sbp, sbp)


def _mlstm_kernel(qk_ref, vo_ref, g_ref, cw_ref, cb_ref, gb_ref, ng_ref, o_ref,
                  xbuf, ct_ref, m_ref, *, L):
    @pl.when(pl.program_id(1) == 0)
    def _():
        xbuf[:, 0:8, :] = jnp.zeros((xbuf.shape[0], 8, xbuf.shape[2]), F32)
        ct_ref[...] = jnp.zeros_like(ct_ref)
        m_ref[...] = jnp.zeros_like(m_ref)

    for b in range(qk_ref.shape[0]):
        _mlstm_chunk(qk_ref.at[b], vo_ref.at[b], g_ref.at[b], cw_ref, cb_ref, gb_ref, ng_ref, o_ref.at[b],
                     xbuf.at[b], ct_ref.at[b], m_ref.at[b], L)


def _mlstm_chunk(qk_ref, vo_ref, g_ref, cw_ref, cb_ref, gb_ref, ng_ref, o_ref, xbuf, ct_ref, m_ref, L):
    H, dk = ML_HEADS, ML_HEAD_DIM
    W = H * dk

    xbuf[8:8 + L, :] = qk_ref[...]
    y = cb_ref[...] + cw_ref[CONV_K - 1:CONV_K, :] * xbuf[8:8 + L, :]
    for j in range(1, CONV_K):
        y = y + cw_ref[CONV_K - 1 - j:CONV_K - j, :] * xbuf[8 - j:8 - j + L, :]
    xbuf[0:8, :] = xbuf[L:L + 8, :]
    qk = y * jax.nn.sigmoid(y)

    g = g_ref[...] + gb_ref[...]
    logf = jax.nn.log_sigmoid(g)
    r = lax.broadcasted_iota(jnp.int32, (L, L), 0)
    c = lax.broadcasted_iota(jnp.int32, (L, L), 1)
    causal = c <= r
    tri = causal.astype(BF16)
    lf_hi = logf.astype(BF16)
    lf_lo = (logf - lf_hi.astype(F32)).astype(BF16)
    bc = _dot(tri, lf_hi) + _dot(tri, lf_lo)
    g_t = g.T
    bc_t = bc.T
    e0 = (lax.broadcasted_iota(jnp.int32, (L, LANES), 1) == 0).astype(F32)

    for h in range(H):
        qh = qk[:, h * dk:(h + 1) * dk].astype(BF16)
        kh = (qk[:, W + h * dk:W + (h + 1) * dk] * (dk ** -0.5)).astype(BF16)
        vh = vo_ref[:, h * dk:(h + 1) * dk].astype(F32)
        oh = vo_ref[:, W + h * dk:W + (h + 1) * dk].astype(F32)
        vaug = jnp.concatenate([vh, e0], axis=1)
        ic_col = g[:, h:h + 1]
        ic_row = g_t[h:h + 1, :]
        bc_col = bc[:, H + h:H + h + 1]
        bc_row = bc_t[H + h:H + h + 1, :]
        m = m_ref[h][:, 0:1]
        ct = ct_ref[h]

        log_d = jnp.where(causal, bc_col - bc_row + ic_row, NEG_INF)
        inter = bc_col + m
        m_t = jnp.maximum(inter, jnp.max(log_d, axis=1, keepdims=True))
        w = _dot_nt(qh, kh) * jnp.exp(log_d - m_t)
        s_inter = jnp.exp(inter - m_t)
        tot = s_inter * _dot(qh, ct.astype(BF16)) + _dot(w.astype(BF16), vaug.astype(BF16))
        num = tot[:, :dk]
        den = tot[:, dk:dk + 1]
        hh = num / jnp.maximum(jnp.abs(den), jnp.exp(-m_t))

        b_last = bc_col[L - 1:L, :]
        log_w = b_last - bc_col + ic_col
        m_new = jnp.maximum(b_last + m, jnp.max(log_w, axis=0, keepdims=True))
        wk = jnp.exp(log_w - m_new)
        decay = jnp.exp(b_last + m - m_new)
        ct_ref[h] = decay * ct + _dot_tn(kh, (wk * vaug).astype(BF16))
        m_ref[h] = jnp.broadcast_to(m_new, (1, LANES))

        hn = _ln(hh) * ng_ref[:, h * dk:(h + 1) * dk]
        o_ref[:, h * dk:(h + 1) * dk] = (jax.nn.sigmoid(oh) * hn).astype(o_ref.dtype)


def _mlstm(mqk, mvo, gates, conv_w, conv_b, gate_b, norm_g, L):
    B, S, W2 = mqk.shape
    W = W2 // 2
    kern = functools.partial(_mlstm_kernel, L=L)
    nb = math.gcd(B, ML_SEQS_PER_STEP)
    row = lambda n: pl.BlockSpec((nb, L, n), lambda b, i: (b, i, 0))
    cst = lambda a: pl.BlockSpec(a.shape, lambda b, i: (0, 0))
    return pl.pallas_call(
        kern,
        grid=(B // nb, S // L),
        in_specs=[row(W2), row(W2), row(LANES), cst(conv_w), cst(conv_b), cst(gate_b), cst(norm_g)],
        out_specs=row(W),
        out_shape=jax.ShapeDtypeStruct((B, S, W), BF16),
        scratch_shapes=[pltpu.VMEM((nb, L + 8, W2), F32),
                        pltpu.VMEM((nb, ML_HEADS, ML_HEAD_DIM, 2 * ML_HEAD_DIM), F32),
                        pltpu.VMEM((nb, ML_HEADS, 1, LANES), F32)],
        compiler_params=_cparams(("arbitrary", "arbitrary")),
        name="mlstm",
    )(mqk, mvo, gates, conv_w, conv_b, gate_b, norm_g)


def _keep_bf16_bits(v):
    bits = lax.bitcast_convert_type(v, jnp.uint32) & jnp.uint32(0xFFFF0000)
    return lax.bitcast_convert_type(bits, F32)


def _split3(v):
    hi = _keep_bf16_bits(v)
    r = v - hi
    mid = _keep_bf16_bits(r)
    lo = r - mid
    return hi.astype(BF16), mid.astype(BF16), lo.astype(BF16)


def _outproj_kernel(sb_ref, ml_ref, x_ref, mod_ref, wo1_ref, wo2_ref, lg_ref, lb_ref, wr_ref,
                    x1_ref, u2_ref, lo_ref):
    mix = _dot(sb_ref[...], wo1_ref[...]) + _dot(ml_ref[...], wo2_ref[...])
    g1 = mod_ref[2:3, :]
    x1 = _ln(DN_ALPHA * x_ref[...] + g1 * mix) * lg_ref[...] + lb_ref[...]
    x1_ref[...] = x1
    u2 = _ln(x1) * (1.0 + mod_ref[4:5, :]) + mod_ref[3:4, :]
    _store_token_rows(u2_ref, u2)
    u_hi, u_mid, _ = _split3(u2)
    w_hi, w_mid = wr_ref[0], wr_ref[1]
    lo_ref[...] = _dot_nt(w_hi, u_hi) + (_dot_nt(w_mid, u_hi) + _dot_nt(w_hi, u_mid))


def _outproj(sb, ml, x, mod, wo1, wo2, ln_g, ln_b, wr_parts, tm):
    B, S, D = x.shape
    W = sb.shape[2]
    E = wr_parts.shape[1]
    nt = S // tm
    row = lambda n: pl.BlockSpec((None, tm, n), lambda b, i: (b, i, 0))
    cst = lambda a: pl.BlockSpec(a.shape, lambda b, i: (0, 0))
    return pl.pallas_call(
        _outproj_kernel,
        grid=(B, nt),
        in_specs=[row(W), row(W), row(D), pl.BlockSpec((None, 6, D), lambda b, i: (b, 0, 0)),
                  cst(wo1), cst(wo2), cst(ln_g), cst(ln_b),
                  pl.BlockSpec(wr_parts.shape, lambda b, i: (0, 0, 0))],
        out_specs=[row(D), pl.BlockSpec((2, None, tm, D // 4), lambda b, i: (0, b, i, 0)),
                   pl.BlockSpec((E, tm), lambda b, i: (0, b * nt + i))],
        out_shape=[jax.ShapeDtypeStruct((B, S, D), F32),
                   jax.ShapeDtypeStruct((2, B, S, D // 4), jnp.uint32),
                   jax.ShapeDtypeStruct((E, B * S), F32)],
        compiler_params=_cparams(("arbitrary", "arbitrary")),
        name="outproj",
    )(sb, ml, x, mod, wo1, wo2, ln_g, ln_b, wr_parts)


def _rows_to_block(rows, dtype):
    n = rows[0].shape[1]
    rid = lax.broadcasted_iota(jnp.int32, (len(rows), n), 0)
    out = jnp.zeros((len(rows), n), dtype)
    for k, v in enumerate(rows):
        out = jnp.where(rid == k, v.astype(dtype), out)
    return out


def _route_kernel(lo_ref, rb_ref, e_ref, r_ref, g_ref, cnt_ref, cnt_scr):
    E, tm = lo_ref.shape

    @pl.when(pl.program_id(0) == 0)
    def _():
        cnt_scr[...] = jnp.zeros_like(cnt_scr)

    scores = jax.nn.sigmoid(lo_ref[...])
    sel = scores + rb_ref[...]
    row_f = lax.broadcasted_iota(jnp.int32, (E, tm), 0).astype(F32)
    groups = [sel[g * GROUP_SIZE:(g + 1) * GROUP_SIZE, :] for g in range(N_GROUPS)]
    gscore = []
    for xg in groups:
        m1 = jnp.max(xg, axis=0, keepdims=True)
        is_max = xg == m1
        cnt = jnp.sum(is_max.astype(F32), axis=0, keepdims=True)
        m2 = jnp.max(jnp.where(is_max, NEG_INF, xg), axis=0, keepdims=True)
        gscore.append(m1 + jnp.where(cnt >= 2.0, m1, m2))
    kept = []
    for g in range(N_GROUPS):
        rank = jnp.zeros((1, tm), jnp.int32)
        for o in range(N_GROUPS):
            if o != g:
                beats = (gscore[o] >= gscore[g]) if o < g else (gscore[o] > gscore[g])
                rank = rank + beats.astype(jnp.int32)
        kept.append(jnp.where(rank < TOPK_GROUPS, groups[g], NEG_INF))
    cur = jnp.concatenate(kept, axis=0)
    idxs = []
    chosen = jnp.zeros((E, tm), F32)
    for k in range(TOP_K):
        m = jnp.max(cur, axis=0, keepdims=True)
        idx = jnp.min(jnp.where(cur == m, row_f, float(E)), axis=0, keepdims=True)
        pick = row_f == idx
        cur = jnp.where(pick, NEG_INF, cur)
        chosen = jnp.where(pick, 1.0, chosen)
        idxs.append(idx)
    r = lax.broadcasted_iota(jnp.int32, (tm, tm), 0)
    c = lax.broadcasted_iota(jnp.int32, (tm, tm), 1)
    before = _dot(chosen.astype(BF16), (r < c).astype(BF16)) + cnt_scr[...]
    gates, ranks = [], []
    for k in range(TOP_K):
        pick = row_f == idxs[k]
        gates.append(jnp.sum(jnp.where(pick, scores, 0.0), axis=0, keepdims=True))
        ranks.append(jnp.sum(jnp.where(pick, before, 0.0), axis=0, keepdims=True))
    gsum = gates[0]
    for gk in gates[1:]:
        gsum = gsum + gk
    e_ref[...] = _rows_to_block(idxs, jnp.int32)
    r_ref[...] = _rows_to_block(ranks, jnp.int32)
    g_ref[...] = _rows_to_block(gates, F32) / gsum * ROUTED_SCALE
    cnt_scr[...] = cnt_scr[...] + jnp.sum(chosen, axis=1, keepdims=True)
    cnt_ref[...] = cnt_scr[...]


def _route(logits_t, router_bias, tm):
    E, T = logits_t.shape
    tok = lambda dt: jax.ShapeDtypeStruct((TOP_K, T), dt)
    return pl.pallas_call(
        _route_kernel,
        grid=(T // tm,),
        in_specs=[pl.BlockSpec((E, tm), lambda i: (0, i)), pl.BlockSpec((E, 1), lambda i: (0, 0))],
        out_specs=[pl.BlockSpec((TOP_K, tm), lambda i: (0, i))] * 3 + [pl.BlockSpec((E, 1), lambda i: (0, 0))],
        out_shape=[tok(jnp.int32), tok(jnp.int32), tok(F32), jax.ShapeDtypeStruct((E, 1), F32)],
        scratch_shapes=[pltpu.VMEM((E, 1), F32)],
        compiler_params=_cparams(("arbitrary",)),
        name="route",
    )(logits_t, router_bias.reshape(E, 1))


def _dest_kernel(e_ref, r_ref, off_ref, d_ref):
    E = off_ref.shape[0]
    tm = e_ref.shape[1]
    row = lax.broadcasted_iota(jnp.int32, (E, tm), 0)
    off = off_ref[...]
    base = [jnp.sum(jnp.where(row == e_ref[k:k + 1, :], off, 0.0), axis=0, keepdims=True) for k in range(TOP_K)]
    d_ref[...] = _rows_to_block(base, jnp.int32) + r_ref[...]


def _dest(e_t, r_t, offsets, tm):
    K, T = e_t.shape
    E = offsets.shape[0]
    tok = pl.BlockSpec((K, tm), lambda i: (0, i))
    return pl.pallas_call(
        _dest_kernel,
        grid=(T // tm,),
        in_specs=[tok, tok, pl.BlockSpec((E, 1), lambda i: (0, 0))],
        out_specs=tok,
        out_shape=jax.ShapeDtypeStruct((K, T), jnp.int32),
        compiler_params=_cparams(("arbitrary",)),
        name="dest",
    )(e_t, r_t, offsets)


def _sc_mesh():
    return plsc.VectorSubcoreMesh(core_axis_name="core", subcore_axis_name="subcore")


def _sc_dispatch(u2p, dest_t, nrows):
    T, W = u2p.shape
    K = dest_t.shape[0]
    win = SC_WINDOW

    @pl.kernel(out_type=jax.ShapeDtypeStruct((nrows, W), u2p.dtype), mesh=_sc_mesh(), scratch_types=[])
    def scatter_rows(x_hbm, i_hbm, o_hbm):
        def body(x_vmem, i_vmem):
            for k in range(K):
                pltpu.sync_copy(x_vmem, o_hbm.at[i_vmem.at[k]])

        pltpu.emit_pipeline(
            body,
            grid=(T // win,),
            in_specs=[pl.BlockSpec((win, W), lambda i: (i, 0)),
                      pl.BlockSpec((K, win), lambda i: (0, i))],
            out_specs=[],
            core_axis_name=("core", "subcore"),
            dimension_semantics=(pltpu.PARALLEL,),
        )(x_hbm, i_hbm)

    return scatter_rows(u2p, dest_t)


def _sc_gather(ys, dest_flat):
    W = ys.shape[1]
    n = dest_flat.shape[1]
    win = SC_WINDOW

    @pl.kernel(out_type=jax.ShapeDtypeStruct((n, W), ys.dtype), mesh=_sc_mesh(), scratch_types=[])
    def gather_rows(y_hbm, i_hbm, o_hbm):
        def body(i_vmem, o_vmem):
            pltpu.sync_copy(y_hbm.at[i_vmem.at[0]], o_vmem)

        pltpu.emit_pipeline(
            body,
            grid=(n // win,),
            in_specs=[pl.BlockSpec((1, win), lambda i: (0, i))],
            out_specs=[pl.BlockSpec((win, W), lambda i: (i, 0))],
            core_axis_name=("core", "subcore"),
            dimension_semantics=(pltpu.PARALLEL,),
        )(i_hbm, o_hbm)

    return gather_rows(ys, dest_flat)


def _expert_kernel(be_ref, bv_ref, nb_ref, first_ref, slot_ref, next_ref, xs_ref, w1_hbm, w3_hbm, w2_hbm,
                   y_ref, w1f, w3f, w2f, w1b, w3b, w2b, sem):
    i = pl.program_id(0)

    def weight_copies(e, s):
        return [pltpu.make_async_copy(w1_hbm.at[e], w1f.at[s], sem.at[s]),
                pltpu.make_async_copy(w3_hbm.at[e], w3f.at[s], sem.at[s]),
                pltpu.make_async_copy(w2_hbm.at[e], w2f.at[s], sem.at[s])]

    @pl.when(i == 0)
    def _():
        for cp in weight_copies(be_ref[0], 0):
            cp.start()

    @pl.when(jnp.logical_and(i < nb_ref[0], first_ref[i] == 1))
    def _():
        s = slot_ref[i]
        for cp in weight_copies(be_ref[i], s):
            cp.wait()

        @pl.when(next_ref[i] >= 0)
        def _():
            for cp in weight_copies(next_ref[i], 1 - s):
                cp.start()

        w1b[...] = w1f[s].astype(BF16)
        w3b[...] = w3f[s].astype(BF16)
        w2b[...] = w2f[s].astype(BF16)

    @pl.when(i < nb_ref[0])
    def _():
        xb = _load_token_rows(xs_ref, valid=bv_ref[i])
        a = _dot_blocks(xb, w1b)
        b = _dot_blocks(xb, w3b)
        hmid = (a * jax.nn.sigmoid(a) * b).astype(BF16)
        _store_token_rows(y_ref, _dot(hmid, w2b[...]))


def _experts(block_e, block_valid, nb_used, xs, w1, w3, w2, rb):
    _, nrows, W = xs.shape
    nb = nrows // rb
    E, D, FF = w1.shape
    idx = jnp.arange(nb, dtype=jnp.int32)
    first = (idx < nb_used[0]) & (block_e != jnp.concatenate([jnp.full((1,), -1, jnp.int32), block_e[:-1]]))
    slot = (jnp.cumsum(first.astype(jnp.int32)) - 1) % 2
    first_pos = jnp.where(first, idx, nb)
    next_pos = jnp.concatenate([lax.cummin(first_pos, reverse=True)[1:], jnp.full((1,), nb, jnp.int32)])
    next_e = jnp.where(next_pos < nb, block_e[jnp.minimum(next_pos, nb - 1)], -1).astype(jnp.int32)
    rows = pl.BlockSpec((2, rb, W), lambda i, be, bv, n, fi, sl, nx: (0, jnp.minimum(i, n[0] - 1), 0))
    grid_spec = pltpu.PrefetchScalarGridSpec(
        num_scalar_prefetch=6,
        grid=(nb,),
        in_specs=[rows, pl.BlockSpec(memory_space=pl.ANY), pl.BlockSpec(memory_space=pl.ANY),
                  pl.BlockSpec(memory_space=pl.ANY)],
        out_specs=rows,
        scratch_shapes=[pltpu.VMEM((2, D, FF), F32), pltpu.VMEM((2, D, FF), F32), pltpu.VMEM((2, FF, D), F32),
                        pltpu.VMEM((D, FF), BF16), pltpu.VMEM((D, FF), BF16), pltpu.VMEM((FF, D), BF16),
                        pltpu.SemaphoreType.DMA((2,))],
    )
    return pl.pallas_call(
        _expert_kernel,
        grid_spec=grid_spec,
        out_shape=jax.ShapeDtypeStruct((2, nrows, W), jnp.uint32),
        compiler_params=_cparams(("arbitrary",)),
        name="experts",
    )(block_e, block_valid, nb_used, first.astype(jnp.int32), slot.astype(jnp.int32), next_e, xs, w1, w3, w2)


def _combine_kernel(yg_ref, gw_ref, u_ref, x1_ref, mod_ref, s1_ref, s3_ref, s2_ref, lg_ref, lb_ref, o_ref):
    ub = _load_token_rows(u_ref)
    a = _dot_blocks(ub, s1_ref)
    b = _dot_blocks(ub, s3_ref)
    ffn = _dot((a * jax.nn.sigmoid(a) * b).astype(BF16), s2_ref[...])
    gw = gw_ref[...]
    routed = None
    for k in range(TOP_K):
        yk = [gw[:, k:k + 1] * blk.astype(F32) for blk in _load_token_rows(yg_ref.at[k])]
        routed = yk if routed is None else [r + y for r, y in zip(routed, yk)]
    ffn = ffn + jnp.concatenate(routed, axis=1)
    g2 = mod_ref[5:6, :]
    o_ref[...] = _ln(DN_ALPHA * x1_ref[...] + g2 * ffn) * lg_ref[...] + lb_ref[...]


def _combine(yg, gw, u2p, x1, mod, s1, s3, s2, ln_g, ln_b, tm):
    B, S, D = x1.shape
    K = yg.shape[0]
    row = lambda n: pl.BlockSpec((None, tm, n), lambda b, i: (b, i, 0))
    cst = lambda a: pl.BlockSpec(a.shape, lambda b, i: (0, 0))
    return pl.pallas_call(
        _combine_kernel,
        grid=(B, S // tm),
        in_specs=[pl.BlockSpec((K, 2, None, tm, D // 4), lambda b, i: (0, 0, b, i, 0)),
                  row(K), pl.BlockSpec((2, None, tm, D // 4), lambda b, i: (0, b, i, 0)), row(D),
                  pl.BlockSpec((None, 6, D), lambda b, i: (b, 0, 0)),
                  cst(s1), cst(s3), cst(s2), cst(ln_g), cst(ln_b)],
        out_specs=row(D),
        out_shape=jax.ShapeDtypeStruct((B, S, D), F32),
        compiler_params=_cparams(("arbitrary", "arbitrary")),
        name="combine",
    )(yg, gw, u2p, x1, mod, s1, s3, s2, ln_g, ln_b)


def _block_layout(counts, T, rb):
    counts = counts.reshape(-1).astype(jnp.int32)
    pcounts = (counts + rb - 1) // rb * rb
    pend = jnp.cumsum(pcounts)
    poffsets = pend - pcounts
    nb = (T * TOP_K) // rb + N_EXPERTS
    starts = jnp.arange(nb, dtype=jnp.int32) * rb
    block_e = jnp.minimum(jnp.sum(starts[:, None] >= pend[None, :], axis=1), N_EXPERTS - 1).astype(jnp.int32)
    nb_used = (pend[-1] // rb).astype(jnp.int32).reshape(1)
    mine = block_e[:, None] == jnp.arange(N_EXPERTS, dtype=jnp.int32)[None, :]
    seg_start = jnp.sum(jnp.where(mine, poffsets[None, :], 0), axis=1)
    seg_count = jnp.sum(jnp.where(mine, counts[None, :], 0), axis=1)
    block_valid = jnp.clip(seg_count - (starts - seg_start), 0, rb).astype(jnp.int32)
    return poffsets, block_e, block_valid, nb_used, nb * rb


def kernel(x, c, w_ada, b_ada, w_in, ml_conv_w, ml_conv_b, ml_gate_b, ml_norm_g, w_out, ln1_g, ln1_b,
           w_router, router_bias, moe_w1, moe_w3, moe_w2, sh_w1, sh_w3, sh_w2, ln2_g, ln2_b):
    B, S, D = x.shape
    T = B * S
    SBW = SB_HEADS * SB_HEAD_DIM
    MLW = ML_HEADS * ML_HEAD_DIM
    rb = EXPERT_BLOCK_ROWS
    tiles = _tiles(S)
    for l in range(DEPTH):
        mod = _adaln(c, w_ada[l], b_ada[l]).reshape(B, 6, D)

        wi = w_in[l]
        c0 = 3 * SBW
        w_sb = wi[:, :c0].astype(BF16)
        w_mqk = wi[:, c0:c0 + 2 * MLW].astype(BF16)
        w_mvo = wi[:, c0 + 2 * MLW:c0 + 4 * MLW].astype(BF16)
        w_g = jnp.pad(wi[:, c0 + 4 * MLW:], ((0, 0), (0, LANES - 2 * ML_HEADS))).astype(BF16)
        sbp, mqk, mvo, gates = _inproj(x, mod, w_sb, w_mqk, w_mvo, w_g, tm=tiles["inproj"])

        sb = _sb_attention(sbp, tq=tiles["sb"])

        gate_b = jnp.pad(ml_gate_b[l], (0, LANES - 2 * ML_HEADS)).reshape(1, LANES)
        ml = _mlstm(mqk, mvo, gates, ml_conv_w[l], ml_conv_b[l].reshape(1, -1), gate_b,
                    ml_norm_g[l].reshape(1, -1), L=tiles["ml"])

        wo = w_out[l].astype(BF16)
        wr_parts = jnp.stack(_split3(w_router[l].T)[:2])
        x1, u2p, logits_t = _outproj(sb, ml, x, mod, wo[:SBW], wo[SBW:], ln1_g[l].reshape(1, D),
                                     ln1_b[l].reshape(1, D), wr_parts, tm=tiles["outproj"])

        e_t, r_t, g_t, counts = _route(logits_t, router_bias[l], tm=tiles["proj"])
        poffsets, block_e, block_valid, nb_used, nrows = _block_layout(counts, T, rb)
        dest_t = _dest(e_t, r_t, poffsets.astype(F32).reshape(N_EXPERTS, 1), tm=tiles["proj"])
        gw = g_t.T
        idx2 = jnp.concatenate([dest_t, dest_t + nrows], axis=1)
        xs = _sc_dispatch(u2p.reshape(2 * T, D // 4), idx2, 2 * nrows).reshape(2, nrows, D // 4)
        ys = _experts(block_e, block_valid, nb_used, xs, moe_w1[l], moe_w3[l], moe_w2[l], rb)
        yg = _sc_gather(ys.reshape(2 * nrows, D // 4), idx2.reshape(1, TOP_K * 2 * T))
        yg = yg.reshape(TOP_K, 2, B, S, D // 4)
        x = _combine(yg, gw.reshape(B, S, TOP_K), u2p, x1, mod, sh_w1[l].astype(BF16),
                     sh_w3[l].astype(BF16), sh_w2[l].astype(BF16), ln2_g[l].reshape(1, D),
                     ln2_b[l].reshape(1, D), tm=tiles["combine"])
    return x
```
